```python
import jax
import jax.numpy as jnp
from jax import lax
import numpy as np

D_MODEL = 1024
BATCH = 2
SEQ = 8192
DEPTH = 2

GRID_W = 64
CTX_LEN = 256
N_EVEN = (DEPTH + 1) // 2
N_ODD = DEPTH // 2
EPS = 1e-6

NA_HEADS = 8
NA_HEAD_DIM = 64
NA_WIN_R = 8
NA_WIN_C = 16
SG_GROUPS = 8
SG_GROUP_DIM = 64
SG_CHUNK = 128
A_WIDTH = NA_HEADS * NA_HEAD_DIM
B_WIDTH = SG_GROUPS * SG_GROUP_DIM
EVEN_IN = 3 * A_WIDTH + 2 * B_WIDTH
EVEN_MIX = A_WIDTH + B_WIDTH

MLA_HEADS = 16
MLA_NOPE = 64
MLA_ROPE = 32
MLA_V = 64
MLA_Q_RANK = 384
MLA_KV_RANK = 256
MLA_IN = MLA_Q_RANK + MLA_KV_RANK + MLA_ROPE
MLA_SCALE = (MLA_NOPE + MLA_ROPE) ** -0.5
Q_BLOCK = 128
ROPE_BASE = 10000.0

FFN_DIM = 2816
N_EXPERTS = 8
TOP_K = 2
EXPERT_DIM = 3584

kernel_name = 'hybrid_natten_gmlp_mla_moe_dit'


def rms_norm(x, g):
    xf = x.astype(jnp.float32)
    y = xf * lax.rsqrt(jnp.mean(xf * xf, axis=-1, keepdims=True) + EPS)
    return (y * g.astype(jnp.float32)).astype(x.dtype)


def ada_mod(cond, w, b):
    return jnp.split(jax.nn.silu(cond) @ w + b, 6, axis=-1)


def modulate(h, shift, scale):
    return h * (1 + scale[..., None, :]) + shift[..., None, :]


def softmax_f32(s, dtype):
    return jax.nn.softmax(s.astype(jnp.float32), axis=-1).astype(dtype)


def rope_tables(n_tokens, dtype):
    t = jnp.arange(n_tokens, dtype=jnp.int32)
    pos = jnp.stack([t // GRID_W, t % GRID_W], axis=-1).astype(jnp.float32)
    n_freq = MLA_ROPE // 4
    inv = jnp.power(ROPE_BASE, -jnp.arange(n_freq, dtype=jnp.float32) / n_freq)
    ang = pos[:, :, None] * inv
    return jnp.cos(ang).astype(dtype), jnp.sin(ang).astype(dtype)


def apply_rope(x, cos, sin):
    xr = x.reshape(*x.shape[:-1], 2, 2, MLA_ROPE // 4)
    x1, x2 = xr[..., 0, :], xr[..., 1, :]
    out = jnp.stack([x1 * cos - x2 * sin, x1 * sin + x2 * cos], axis=-2)
    return out.reshape(x.shape)


def context_attention(q, k, v):
    s = jnp.einsum('bqhd,bkhd->bhqk', q, k) * (q.shape[-1] ** -0.5)
    o = jnp.einsum('bhqk,bkhd->bqhd', softmax_f32(s, v.dtype), v)
    return o.reshape(*o.shape[:2], -1)


def neighbourhood_attention(q, k, v, k_ctx, v_ctx, rpb):
    B, S, H, Dh = q.shape
    rows = S // GRID_W
    kr = min(NA_WIN_R, rows)
    scale = Dh ** -0.5
    qg = q.reshape(B, rows, GRID_W, H, Dh)
    kg = k.reshape(B, rows, GRID_W, H, Dh)
    vg = v.reshape(B, rows, GRID_W, H, Dh)
    cols = jnp.arange(GRID_W)
    col_start = jnp.clip(cols - NA_WIN_C // 2, 0, GRID_W - NA_WIN_C)
    col_idx = col_start[:, None] + jnp.arange(NA_WIN_C)[None, :]
    col_bias_idx = col_idx - cols[:, None] + (NA_WIN_C - 1)
    n_win = kr * NA_WIN_C

    def one_row(r):
        row_start = jnp.clip(r - kr // 2, 0, rows - kr)
        q_r = lax.dynamic_index_in_dim(qg, r, axis=1, keepdims=False)
        k_rows = lax.dynamic_slice_in_dim(kg, row_start, kr, axis=1)
        v_rows = lax.dynamic_slice_in_dim(vg, row_start, kr, axis=1)
        k_win = k_rows[:, :, col_idx]
        v_win = v_rows[:, :, col_idx]
        row_bias_idx = row_start + jnp.arange(kr) - r + (NA_WIN_R - 1)
        bias = rpb[:, row_bias_idx[None, :, None], col_bias_idx[:, None, :]]
        s_win = jnp.einsum('bwhd,bkwjhd->bhwkj', q_r, k_win) * scale + bias
        s_ctx = jnp.einsum('bwhd,bchd->bhwc', q_r, k_ctx) * scale
        s = jnp.concatenate([s_win.reshape(B, H, GRID_W, n_win), s_ctx], axis=-1)
        p = softmax_f32(s, v.dtype)
        p_win = p[..., :n_win].reshape(B, H, GRID_W, kr, NA_WIN_C)
        o = jnp.einsum('bhwkj,bkwjhd->bwhd', p_win, v_win)
        return o + jnp.einsum('bhwc,bchd->bwhd', p[..., n_win:], v_ctx)

    out = lax.map(one_row, jnp.arange(rows))
    return jnp.moveaxis(out, 0, 1).reshape(B, S, H * Dh)


def spatial_gating(u, g, w_s, b_s, norm_g):
    B, S, _ = u.shape
    u = jax.nn.gelu(u)
    g = jax.nn.gelu(g).reshape(B, S // SG_CHUNK, SG_CHUNK, SG_GROUPS, SG_GROUP_DIM)
    gf = g.astype(jnp.float32)
    mu = jnp.mean(gf, axis=-1, keepdims=True)
    var = jnp.mean(jnp.square(gf - mu), axis=-1, keepdims=True)
    gn = ((gf - mu) * lax.rsqrt(var + EPS)).astype(g.dtype) * norm_g.reshape(SG_GROUPS, SG_GROUP_DIM)
    mixed = jnp.einsum('gpq,bnqgc->bnpgc', w_s, gn) + b_s.T[:, :, None]
    return u * mixed.reshape(B, S, B_WIDTH)


def even_mixer(h_lat, h_ctx, w_in, rpb, w_s, b_s, sg_g, w_out, need_ctx_out):
    heads = lambda t: t.reshape(*t.shape[:-1], NA_HEADS, NA_HEAD_DIM)
    cuts = [A_WIDTH, 2 * A_WIDTH, 3 * A_WIDTH, 3 * A_WIDTH + B_WIDTH]
    q, k, v, u, g = jnp.split(h_lat @ w_in, cuts, axis=-1)
    if need_ctx_out:
        qc, kc, vc, uc, gc = jnp.split(h_ctx @ w_in, cuts, axis=-1)
    else:
        kc, vc = jnp.split(h_ctx @ w_in[:, A_WIDTH:3 * A_WIDTH], 2, axis=-1)
    a_lat = neighbourhood_attention(heads(q), heads(k), heads(v), heads(kc), heads(vc), rpb)
    b_lat = spatial_gating(u, g, w_s, b_s, sg_g)
    y_lat = jnp.concatenate([a_lat, b_lat], axis=-1) @ w_out
    if not need_ctx_out:
        return y_lat, None
    a_ctx = context_attention(heads(qc), heads(kc), heads(vc))
    b_ctx = spatial_gating(uc, gc, w_s, b_s, sg_g)
    y_ctx = jnp.concatenate([a_ctx, b_ctx], axis=-1) @ w_out
    return y_lat, y_ctx


def mla_queries(c_q, q_g, w_uq):
    q = (rms_norm(c_q, q_g) @ w_uq).reshape(*c_q.shape[:-1], MLA_HEADS, MLA_NOPE + MLA_ROPE)
    return q[..., :MLA_NOPE], q[..., MLA_NOPE:]


def mla_keys_values(c_kv, kv_g, w_ukv):
    kv = (rms_norm(c_kv, kv_g) @ w_ukv).reshape(*c_kv.shape[:-1], MLA_HEADS, MLA_NOPE + MLA_V)
    return kv[..., :MLA_NOPE], kv[..., MLA_NOPE:]


def mla_scores_out(qn, qr, kn, kr, v):
    s = (jnp.einsum('bqhd,bkhd->bhqk', qn, kn) + jnp.einsum('bqhr,bkr->bhqk', qr, kr)) * MLA_SCALE
    return jnp.einsum('bhqk,bkhd->bqhd', softmax_f32(s, v.dtype), v)


def mla_latent_attention(q_nope, q_rope, k_nope, k_rope, v):
    B, S = q_nope.shape[:2]
    nb = S // Q_BLOCK
    blocks = lambda t: jnp.moveaxis(t.reshape(B, nb, Q_BLOCK, *t.shape[2:]), 1, 0)
    out = lax.map(lambda qs: mla_scores_out(qs[0], qs[1], k_nope, k_rope, v), (blocks(q_nope), blocks(q_rope)))
    return jnp.moveaxis(out, 0, 1).reshape(B, S, MLA_HEADS * MLA_V)


def odd_mixer(h_lat, h_ctx, w_in, q_g, kv_g, w_uq, w_ukv, w_out, cos, sin, need_ctx_out):
    cuts = [MLA_Q_RANK, MLA_Q_RANK + MLA_KV_RANK]
    c_q, c_kv, k_rope = jnp.split(h_lat @ w_in, cuts, axis=-1)
    q_nope, q_rope = mla_queries(c_q, q_g, w_uq)
    q_rope = apply_rope(q_rope, cos[:, None], sin[:, None])
    k_rope = apply_rope(k_rope, cos, sin)
    k_nope, v = mla_keys_values(c_kv, kv_g, w_ukv)
    if need_ctx_out:
        cq_c, ckv_c, kr_c = jnp.split(h_ctx @ w_in, cuts, axis=-1)
    else:
        ckv_c, kr_c = jnp.split(h_ctx @ w_in[:, MLA_Q_RANK:], [MLA_KV_RANK], axis=-1)
    kn_c, v_c = mla_keys_values(ckv_c, kv_g, w_ukv)
    k_nope_all = jnp.concatenate([k_nope, kn_c], axis=1)
    k_rope_all = jnp.concatenate([k_rope, kr_c], axis=1)
    v_all = jnp.concatenate([v, v_c], axis=1)
    y_lat = mla_latent_attention(q_nope, q_rope, k_nope_all, k_rope_all, v_all) @ w_out
    if not need_ctx_out:
        return y_lat, None
    qn_c, qr_c = mla_queries(cq_c, q_g, w_uq)
    o_c = mla_scores_out(qn_c, qr_c, kn_c, kr_c, v_c)
    y_ctx = o_c.reshape(*o_c.shape[:2], MLA_HEADS * MLA_V) @ w_out
    return y_lat, y_ctx


def swiglu(h, w_gate, w_up, w_down):
    return (jax.nn.silu(h @ w_gate) * (h @ w_up)) @ w_down


def moe_swiglu(h, w_router, b_router, w_gate, w_up, w_down):
    logits = (h @ w_router + b_router).astype(jnp.float32)
    top_v, top_i = lax.top_k(logits, TOP_K)
    weights = jax.nn.softmax(top_v, axis=-1)
    combine = jnp.sum(jax.nn.one_hot(top_i, N_EXPERTS, dtype=jnp.float32) * weights[..., None], axis=-2).astype(h.dtype)
    y = jnp.zeros_like(h)
    for e in range(N_EXPERTS):
        y = y + combine[..., e:e + 1] * swiglu(h, w_gate[e], w_up[e], w_down[e])
    return y


def setup_inputs(seed: int = 0) -> dict:
    key = jax.random.key(seed)
    ks = iter(jax.random.split(key, 32))
    nrm = lambda shape, scale: jax.random.normal(next(ks), shape, jnp.float32) * scale
    gain = lambda shape: 1.0 + nrm(shape, 0.05)
    D = D_MODEL
    return {
        'x': nrm((BATCH, SEQ, D), 1.0),
        'c': nrm((BATCH, D), 1.0),
        'ctx': nrm((BATCH, CTX_LEN, D), 1.0),
        'c_ctx': nrm((D,), 1.0),
        'ada_w': nrm((DEPTH, D, 6 * D), 0.5 * D ** -0.5),
        'ada_b': nrm((DEPTH, 6 * D), 0.02),
        'norm_g': gain((DEPTH, 2, D)),
        'final_g': gain((D,)),
        'na_w_in': nrm((N_EVEN, D, EVEN_IN), D ** -0.5),
        'na_rpb': nrm((N_EVEN, NA_HEADS, 2 * NA_WIN_R - 1, 2 * NA_WIN_C - 1), 0.5),
        'sg_w': nrm((N_EVEN, SG_GROUPS, SG_CHUNK, SG_CHUNK), SG_CHUNK ** -0.5),
        'sg_b': nrm((N_EVEN, SG_GROUPS, SG_CHUNK), 0.02),
        'sg_norm_g': gain((N_EVEN, B_WIDTH)),
        'even_w_out': nrm((N_EVEN, EVEN_MIX, D), EVEN_MIX ** -0.5),
        'ffn_w_gate': nrm((N_EVEN, D, FFN_DIM), D ** -0.5),
        'ffn_w_up': nrm((N_EVEN, D, FFN_DIM), D ** -0.5),
        'ffn_w_down': nrm((N_EVEN, FFN_DIM, D), FFN_DIM ** -0.5),
        'mla_w_in': nrm((N_ODD, D, MLA_IN), D ** -0.5),
        'mla_q_norm_g': gain((N_ODD, MLA_Q_RANK)),
        'mla_kv_norm_g': gain((N_ODD, MLA_KV_RANK)),
        'mla_w_uq': nrm((N_ODD, MLA_Q_RANK, MLA_HEADS * (MLA_NOPE + MLA_ROPE)), MLA_Q_RANK ** -0.5),
        'mla_w_ukv': nrm((N_ODD, MLA_KV_RANK, MLA_HEADS * (MLA_NOPE + MLA_V)), MLA_KV_RANK ** -0.5),
        'mla_w_out': nrm((N_ODD, MLA_HEADS * MLA_V, D), (MLA_HEADS * MLA_V) ** -0.5),
        'moe_w_router': nrm((N_ODD, D, N_EXPERTS), D ** -0.5),
        'moe_b_router': nrm((N_ODD, N_EXPERTS), 0.01),
        'moe_w_gate': nrm((N_ODD, N_EXPERTS, D, EXPERT_DIM), D ** -0.5),
        'moe_w_up': nrm((N_ODD, N_EXPERTS, D, EXPERT_DIM), D ** -0.5),
        'moe_w_down': nrm((N_ODD, N_EXPERTS, EXPERT_DIM, D), EXPERT_DIM ** -0.5),
    }


def reference(x, c, ctx, c_ctx, ada_w, ada_b, norm_g, final_g, na_w_in, na_rpb, sg_w, sg_b, sg_norm_g,
              even_w_out, ffn_w_gate, ffn_w_up, ffn_w_down, mla_w_in, mla_q_norm_g, mla_kv_norm_g,
              mla_w_uq, mla_w_ukv, mla_w_out, moe_w_router, moe_b_router, moe_w_gate, moe_w_up, moe_w_down):
    S = x.shape[1]
    n_ctx = ctx.shape[1]
    cos, sin = rope_tables(S, x.dtype)
    for layer in range(DEPTH):
        last = layer == DEPTH - 1
        i = layer // 2
        sh_m, sc_m, g_m, sh_f, sc_f, g_f = ada_mod(c, ada_w[layer], ada_b[layer])
        csh_m, csc_m, cg_m, csh_f, csc_f, cg_f = ada_mod(c_ctx, ada_w[layer], ada_b[layer])
        h_lat = modulate(rms_norm(x, norm_g[layer, 0]), sh_m, sc_m)
        h_ctx = modulate(rms_norm(ctx, norm_g[layer, 0]), csh_m, csc_m)
        if layer % 2 == 0:
            y_lat, y_ctx = even_mixer(h_lat, h_ctx, na_w_in[i], na_rpb[i], sg_w[i], sg_b[i], sg_norm_g[i],
                                      even_w_out[i], not last)
            ffn = lambda h: swiglu(h, ffn_w_gate[i], ffn_w_up[i], ffn_w_down[i])
        else:
            y_lat, y_ctx = odd_mixer(h_lat, h_ctx, mla_w_in[i], mla_q_norm_g[i], mla_kv_norm_g[i], mla_w_uq[i],
                                     mla_w_ukv[i], mla_w_out[i], cos, sin, not last)
            ffn = lambda h: moe_swiglu(h, moe_w_router[i], moe_b_router[i], moe_w_gate[i], moe_w_up[i], moe_w_down[i])
        x = x + g_m[:, None, :] * y_lat
        h_lat = modulate(rms_norm(x, norm_g[layer, 1]), sh_f, sc_f)
        if last:
            x = x + g_f[:, None, :] * ffn(h_lat)
        else:
            ctx = ctx + cg_m[None, :] * y_ctx
            h_ctx = modulate(rms_norm(ctx, norm_g[layer, 1]), csh_f, csc_f)
            f = ffn(jnp.concatenate([h_ctx, h_lat], axis=1))
            ctx = ctx + cg_f[None, :] * f[:, :n_ctx]
            x = x + g_f[:, None, :] * f[:, n_ctx:]
    return rms_norm(x, final_g)
```

```python
import functools

import numpy as np
import jax
import jax.numpy as jnp
from jax import lax
from jax.experimental import pallas as pl
from jax.experimental.pallas import tpu as pltpu

F32 = jnp.float32
BF16 = jnp.bfloat16
EPS = 1e-6
NEG = -1e30

LANES = 128
VMEM_LIMIT_BYTES = 56 * 1024 * 1024

GRID_W = 64
NA_HEADS = 8
NA_HEAD_DIM = 64
NA_WIN_R = 8
NA_WIN_C = 16
SG_GROUPS = 8
SG_GROUP_DIM = 64
SG_CHUNK = 128
A_WIDTH = NA_HEADS * NA_HEAD_DIM
B_WIDTH = SG_GROUPS * SG_GROUP_DIM
MLA_HEADS = 16
MLA_NOPE = 64
MLA_ROPE = 32
MLA_V = 64
MLA_Q_RANK = 384
MLA_KV_RANK = 256
MLA_SCALE = (MLA_NOPE + MLA_ROPE) ** -0.5
ROPE_BASE = 10000.0
N_EXPERTS = 8

ROW_TILE = 512
NA_QROWS = 4
NA_QB = NA_QROWS * GRID_W
NA_KB = 3 * NA_QB
FLASH_TQ = 512
FLASH_TK = 1024
MOE_TILE = 512
MOE_FCHUNKS = 2
COMBINE_TILE = 256


def _params(sem):
    return pltpu.CompilerParams(dimension_semantics=sem, vmem_limit_bytes=VMEM_LIMIT_BYTES)


def _dot(a, b):
    return jnp.dot(a, b, preferred_element_type=F32)


def _dot_nt(a, b):
    return lax.dot_general(a, b, (((1,), (1,)), ((), ())), preferred_element_type=F32)


def _silu(x):
    return x / (1.0 + jnp.exp(-x))


def _gelu_tanh(x):
    return 0.5 * x * (1.0 + jnp.tanh(0.7978845608028654 * (x + 0.044715 * (x * x * x))))


def _rms(x, g):
    return x * lax.rsqrt(jnp.mean(x * x, axis=-1, keepdims=True) + EPS) * g


def _norm_mod(x, g, sh, sc):
    return _rms(x, g) * (1.0 + sc) + sh


def _full(shape):
    n = len(shape)
    return pl.BlockSpec(shape, lambda *_: (0,) * n)


def _ada_kernel(cond_ref, w_ref, b_ref, o_ref):
    c = cond_ref[...]
    o_ref[...] = jnp.dot(_silu(c), w_ref[...], preferred_element_type=F32,
                         precision=lax.Precision.HIGHEST) + b_ref[...]


def _ada_mod(cond8, ada_w, ada_b):
    depth, d, n = ada_w.shape
    tn = n // 4
    return pl.pallas_call(
        _ada_kernel,
        grid=(depth, n // tn),
        in_specs=[_full((8, d)),
                  pl.BlockSpec((None, d, tn), lambda l, j: (l, 0, j)),
                  pl.BlockSpec((None, 1, tn), lambda l, j: (l, 0, j))],
        out_specs=pl.BlockSpec((None, 8, tn), lambda l, j: (l, 0, j)),
        out_shape=jax.ShapeDtypeStruct((depth, 8, n), F32),
        compiler_params=_params(("parallel", "parallel")),
        name="ada_mod",
    )(cond8, ada_w, ada_b.reshape(depth, 1, n))


def _group_map(tiles_per_batch, n_batch):
    return lambda t: (jnp.minimum(t // tiles_per_batch, n_batch), 0, 0)


def _mod_spec(d, tiles_per_batch, n_batch):
    return pl.BlockSpec((None, 1, d), _group_map(tiles_per_batch, n_batch))


def _even_in_kernel(x_ref, g_ref, sh_ref, sc_ref, wqkv_ref, wug_ref, qkv_ref, ug_ref):
    h = _norm_mod(x_ref[...], g_ref[...], sh_ref[...], sc_ref[...]).astype(BF16)
    qkv_ref[...] = _dot(h, wqkv_ref[...]).astype(BF16)
    ug_ref[...] = _dot(h, wug_ref[...])


def _even_in(x, g, sh, sc, wqkv, wug, tiles_per_batch, n_batch):
    t, d = x.shape
    tm = ROW_TILE
    mod = _mod_spec(d, tiles_per_batch, n_batch)
    return pl.pallas_call(
        _even_in_kernel,
        grid=(t // tm,),
        in_specs=[pl.BlockSpec((tm, d), lambda i: (i, 0)), _full((1, d)), mod, mod,
                  _full(wqkv.shape), _full(wug.shape)],
        out_specs=[pl.BlockSpec((tm, wqkv.shape[1]), lambda i: (i, 0)),
                   pl.BlockSpec((tm, wug.shape[1]), lambda i: (i, 0))],
        out_shape=[jax.ShapeDtypeStruct((t, wqkv.shape[1]), BF16),
                   jax.ShapeDtypeStruct((t, wug.shape[1]), F32)],
        compiler_params=_params(("parallel",)),
        name="even_in",
    )(x, g, sh, sc, wqkv, wug)


def _na_bias_table(rpb, rows):
    nblk = rows // NA_QROWS
    q = np.arange(NA_QB)
    kk = np.arange(NA_KB)
    qc = q % GRID_W
    kc = kk % GRID_W
    col_start = np.clip(qc - NA_WIN_C // 2, 0, GRID_W - NA_WIN_C)
    ris, cis, valids = [], [], []
    for j in (0, 1, nblk - 1):
        r = NA_QROWS * j + q // GRID_W
        kr = NA_QROWS * int(np.clip(j - 1, 0, nblk - 3)) + kk // GRID_W
        row_start = np.clip(r - NA_WIN_R // 2, 0, rows - NA_WIN_R)
        valid = ((kr[None, :] >= row_start[:, None]) & (kr[None, :] < row_start[:, None] + NA_WIN_R)
                 & (kc[None, :] >= col_start[:, None]) & (kc[None, :] < col_start[:, None] + NA_WIN_C))
        ris.append(np.clip(kr[None, :] - r[:, None] + NA_WIN_R - 1, 0, 2 * NA_WIN_R - 2))
        cis.append(np.clip(kc[None, :] - qc[:, None] + NA_WIN_C - 1, 0, 2 * NA_WIN_C - 2))
        valids.append(valid)
    ri, ci, valid = np.stack(ris), np.stack(cis), np.stack(valids)
    tab = rpb[:, ri, ci]
    tab = jnp.where(valid[None], tab, NEG)
    return jnp.transpose(tab, (1, 0, 2, 3)).astype(F32)


def _na_kernel(q_ref, k0_ref, k1_ref, k2_ref, v0_ref, v1_ref, v2_ref, kc_ref, vc_ref, bias_ref, o_ref):
    jj = pl.program_id(2)
    lane = lax.broadcasted_iota(jnp.int32, (1, LANES), 1)
    first = lane < NA_HEAD_DIM
    q = q_ref[...]
    zero = jnp.zeros_like(q)
    scale = NA_HEAD_DIM ** -0.5

    def head_q(hh):
        return jnp.where(first if hh == 0 else jnp.logical_not(first), q, zero)

    @pl.when(jj == 0)
    def _ctx_queries():
        outs = []
        for hh in range(2):
            s = _dot_nt(head_q(hh), kc_ref[...]) * scale
            p = jnp.exp(s - jnp.max(s, axis=-1, keepdims=True))
            l = jnp.sum(p, axis=-1, keepdims=True)
            outs.append(_dot(p.astype(BF16), vc_ref[...]) / l)
        o_ref[...] = jnp.where(first, outs[0], outs[1]).astype(o_ref.dtype)

    @pl.when(jj > 0)
    def _latent_queries():
        k_refs = (k0_ref, k1_ref, k2_ref)
        v_refs = (v0_ref, v1_ref, v2_ref)
        outs = []
        for hh in range(2):
            qh = head_q(hh)
            s = [_dot_nt(qh, k_refs[i][...]) * scale + bias_ref[hh, :, NA_QB * i:NA_QB * (i + 1)]
                 for i in range(3)]
            s.append(_dot_nt(qh, kc_ref[...]) * scale)
            m = functools.reduce(jnp.maximum, [jnp.max(x, axis=-1, keepdims=True) for x in s])
            p = [jnp.exp(x - m) for x in s]
            l = functools.reduce(jnp.add, [jnp.sum(x, axis=-1, keepdims=True) for x in p])
            vals = [v_refs[i][...] for i in range(3)] + [vc_ref[...]]
            o = functools.reduce(jnp.add, [_dot(p[i].astype(BF16), vals[i]) for i in range(4)])
            outs.append(o / l)
        o_ref[...] = jnp.where(first, outs[0], outs[1]).astype(o_ref.dtype)


def _na_attention(qkv, bias, n_batch, seq, n_ctx):
    t = qkv.shape[0]
    assert n_ctx == NA_QB and seq % NA_QB == 0
    nblk = seq // NA_QB
    assert nblk >= 3
    hp = NA_HEADS // 2
    kcol, vcol = A_WIDTH // LANES, 2 * A_WIDTH // LANES
    ctx_blk = n_batch * nblk

    def q_map(b, h, j):
        return (jnp.where(j == 0, ctx_blk + b, b * nblk + j - 1), h)

    def win_map(i, col):
        return lambda b, h, j: (b * nblk + jnp.clip(j - 2, 0, nblk - 3) + i, col + h)

    def ctx_map(col):
        return lambda b, h, j: (ctx_blk + b, col + h)

    def bias_map(b, h, j):
        return (jnp.where(j <= 1, 0, jnp.where(j == nblk, 2, 1)), h, 0, 0)

    blk = (NA_QB, LANES)
    in_specs = ([pl.BlockSpec(blk, q_map)]
                + [pl.BlockSpec(blk, win_map(i, kcol)) for i in range(3)]
                + [pl.BlockSpec(blk, win_map(i, vcol)) for i in range(3)]
                + [pl.BlockSpec(blk, ctx_map(kcol)), pl.BlockSpec(blk, ctx_map(vcol)),
                   pl.BlockSpec((None, 2, NA_QB, NA_KB), bias_map)])
    return pl.pallas_call(
        _na_kernel,
        grid=(n_batch, hp, nblk + 1),
        in_specs=in_specs,
        out_specs=pl.BlockSpec(blk, q_map),
        out_shape=jax.ShapeDtypeStruct((t, A_WIDTH), BF16),
        compiler_params=_params(("parallel", "parallel", "arbitrary")),
        name="na_attention",
    )(*([qkv] * 9), bias)


def _seg_mean(x, avg):
    hi = x.astype(BF16)
    lo = (x - hi.astype(F32)).astype(BF16)
    return _dot(hi, avg) + _dot(lo, avg)


def _sg_kernel(ug_ref, ws_ref, bm_ref, ng_ref, avg_ref, o_ref):
    lane = lax.broadcasted_iota(jnp.int32, (1, LANES), 1)
    first = lane < SG_GROUP_DIM
    avg = avg_ref[...]
    for c in range(ROW_TILE // SG_CHUNK):
        rows = slice(SG_CHUNK * c, SG_CHUNK * (c + 1))
        u = _gelu_tanh(ug_ref[rows, 0:B_WIDTH])
        g = _gelu_tanh(ug_ref[rows, B_WIDTH:2 * B_WIDTH])
        d = g - _seg_mean(g, avg)
        var = _seg_mean(d * d, avg)
        gn = (d * lax.rsqrt(var + EPS) * ng_ref[...]).astype(BF16)
        parts = []
        for j in range(SG_GROUPS // 2):
            gj = gn[:, LANES * j:LANES * (j + 1)]
            parts.append(jnp.where(first, _dot(ws_ref[2 * j], gj), _dot(ws_ref[2 * j + 1], gj)))
        mixed = jnp.concatenate(parts, axis=1) + bm_ref[...]
        o_ref[rows, :] = (u * mixed).astype(o_ref.dtype)


def _spatial_gating(ug, ws, bmat, ng, avg):
    t = ug.shape[0]
    tm = ROW_TILE
    return pl.pallas_call(
        _sg_kernel,
        grid=(t // tm,),
        in_specs=[pl.BlockSpec((tm, 2 * B_WIDTH), lambda i: (i, 0)), _full(ws.shape), _full(bmat.shape),
                  _full(ng.shape), _full(avg.shape)],
        out_specs=pl.BlockSpec((tm, B_WIDTH), lambda i: (i, 0)),
        out_shape=jax.ShapeDtypeStruct((t, B_WIDTH), BF16),
        compiler_params=_params(("parallel",)),
        name="spatial_gating",
    )(ug, ws, bmat, ng, avg)


def _proj_res_kernel(n_in, x_ref, gate_ref, *refs):
    a_refs, w_refs, o_ref = refs[:n_in], refs[n_in:2 * n_in], refs[2 * n_in]
    y = functools.reduce(jnp.add, [_dot(a[...], w[...]) for a, w in zip(a_refs, w_refs)])
    o_ref[...] = x_ref[...] + gate_ref[...] * y


def _proj_residual(x, gate, acts, weights, n_rows, tiles_per_batch, n_batch):
    d = x.shape[1]
    tm = ROW_TILE
    row = lambda i: (i, 0)
    return pl.pallas_call(
        functools.partial(_proj_res_kernel, len(acts)),
        grid=(n_rows // tm,),
        in_specs=([pl.BlockSpec((tm, d), row), _mod_spec(d, tiles_per_batch, n_batch)]
                  + [pl.BlockSpec((tm, a.shape[1]), row) for a in acts]
                  + [_full(w.shape) for w in weights]),
        out_specs=pl.BlockSpec((tm, d), row),
        out_shape=jax.ShapeDtypeStruct((n_rows, d), F32),
        compiler_params=_params(("parallel",)),
        name="proj_residual",
    )(x, gate, *acts, *weights)


def _ffn_kernel(n_chunks, x_ref, g_ref, sh_ref, sc_ref, gate_ref, wg_ref, wu_ref, wd_ref, o_ref):
    x = x_ref[...]
    h = _norm_mod(x, g_ref[...], sh_ref[...], sc_ref[...]).astype(BF16)
    fc = wg_ref.shape[1] // n_chunks
    acc = None
    for c in range(n_chunks):
        cs = slice(fc * c, fc * (c + 1))
        act = (_silu(_dot(h, wg_ref[:, cs])) * _dot(h, wu_ref[:, cs])).astype(BF16)
        part = _dot(act, wd_ref[cs, :])
        acc = part if acc is None else acc + part
    o_ref[...] = x + gate_ref[...] * acc


def _ffn(x, g, sh, sc, gate, wg, wu, wd, tiles_per_batch, n_batch):
    t, d = x.shape
    tm = ROW_TILE
    f = wg.shape[1]
    n_chunks = 2 if f % (2 * LANES) == 0 else 1
    mod = _mod_spec(d, tiles_per_batch, n_batch)
    resident = lambda w: pl.BlockSpec(w.shape, lambda i: (0, 0), pipeline_mode=pl.Buffered(1))
    return pl.pallas_call(
        functools.partial(_ffn_kernel, n_chunks),
        grid=(t // tm,),
        in_specs=[pl.BlockSpec((tm, d), lambda i: (i, 0)), _full((1, d)), mod, mod, mod,
                  resident(wg), resident(wu), resident(wd)],
        out_specs=pl.BlockSpec((tm, d), lambda i: (i, 0)),
        out_shape=jax.ShapeDtypeStruct((t, d), F32),
        compiler_params=_params(("parallel",)),
        name="ffn",
    )(x, g, sh, sc, gate, wg, wu, wd)


def _rope_tables(n_batch, seq, n_ctx):
    t = np.arange(seq)
    pos = np.stack([t // GRID_W, t % GRID_W], axis=-1).astype(np.float32)
    n_freq = MLA_ROPE // 4
    inv = jnp.power(ROPE_BASE, -jnp.arange(n_freq, dtype=F32) / n_freq)
    ang = jnp.asarray(pos)[:, :, None] * inv
    cos, sin = jnp.cos(ang), jnp.sin(ang)
    cos_r = jnp.concatenate([cos, cos], axis=-1).reshape(seq, MLA_ROPE)
    sin_r = jnp.concatenate([-sin, sin], axis=-1).reshape(seq, MLA_ROPE)
    ones = jnp.ones((seq, MLA_NOPE), F32)
    pad = LANES - MLA_NOPE - MLA_ROPE
    cos_l = jnp.concatenate([ones, cos_r, jnp.ones((seq, pad), F32)], axis=-1)
    sin_l = jnp.concatenate([0 * ones, sin_r, jnp.zeros((seq, pad), F32)], axis=-1)
    cos_all = jnp.concatenate([jnp.tile(cos_l, (n_batch, 1)), jnp.ones((n_batch * n_ctx, LANES), F32)])
    sin_all = jnp.concatenate([jnp.tile(sin_l, (n_batch, 1)), jnp.zeros((n_batch * n_ctx, LANES), F32)])
    return cos_all, sin_all


def _swap_rope_cols(w_rope):
    q = MLA_ROPE // 4
    return jnp.concatenate([w_rope[..., q:2 * q], w_rope[..., 0:q], w_rope[..., 3 * q:4 * q],
                            w_rope[..., 2 * q:3 * q]], axis=-1)


def _mla_weights(w_in, w_uq, w_ukv, w_out):
    d = w_in.shape[0]
    pad = LANES - MLA_NOPE - MLA_ROPE
    w_kr = w_in[:, MLA_Q_RANK + MLA_KV_RANK:]
    lanes_kr = lambda w: jnp.concatenate([jnp.zeros((d, MLA_NOPE), F32), w, jnp.zeros((d, pad), F32)], axis=1)
    win = jnp.concatenate([w_in[:, :MLA_Q_RANK + MLA_KV_RANK], lanes_kr(w_kr), lanes_kr(_swap_rope_cols(w_kr))],
                          axis=1)
    uq = w_uq.reshape(MLA_Q_RANK, MLA_HEADS, MLA_NOPE + MLA_ROPE)
    zq = jnp.zeros((MLA_Q_RANK, MLA_HEADS, pad), F32)
    uq_a = jnp.concatenate([uq, zq], axis=-1).reshape(MLA_Q_RANK, MLA_HEADS * LANES)
    uq_b = jnp.concatenate([jnp.zeros((MLA_Q_RANK, MLA_HEADS, MLA_NOPE), F32),
                            _swap_rope_cols(uq[..., MLA_NOPE:]), zq], axis=-1).reshape(MLA_Q_RANK, MLA_HEADS * LANES)
    ukv = w_ukv.reshape(MLA_KV_RANK, MLA_HEADS, MLA_NOPE + MLA_V)
    zk = jnp.zeros((MLA_KV_RANK, MLA_HEADS, LANES - MLA_NOPE), F32)
    uk = jnp.concatenate([ukv[..., :MLA_NOPE], zk], axis=-1).reshape(MLA_KV_RANK, MLA_HEADS * LANES)
    zv = jnp.zeros((MLA_KV_RANK, MLA_HEADS, LANES - MLA_V), F32)
    uv = jnp.concatenate([ukv[..., MLA_NOPE:], zv], axis=-1).reshape(MLA_KV_RANK, MLA_HEADS * LANES)
    wo = w_out.reshape(MLA_HEADS, MLA_V, -1)
    wo = jnp.concatenate([wo, jnp.zeros((MLA_HEADS, LANES - MLA_V, wo.shape[-1]), F32)], axis=1)
    return (win.astype(BF16), jnp.concatenate([uq_a, uq_b], axis=1).astype(BF16),
            jnp.concatenate([uk, uv], axis=1).astype(BF16), wo.reshape(MLA_HEADS * LANES, -1).astype(BF16))


def _mla_pre_kernel(x_ref, g_ref, sh_ref, sc_ref, win_ref, qg_ref, kvg_ref, wuq_ref, wukv_ref, cos_ref, sin_ref,
                    q_ref, k_ref, v_ref):
    h = _norm_mod(x_ref[...], g_ref[...], sh_ref[...], sc_ref[...]).astype(BF16)
    p = _dot(h, win_ref[...])
    cq = _rms(p[:, :MLA_Q_RANK], qg_ref[...]).astype(BF16)
    ckv = _rms(p[:, MLA_Q_RANK:MLA_Q_RANK + MLA_KV_RANK], kvg_ref[...]).astype(BF16)
    o_kr = MLA_Q_RANK + MLA_KV_RANK
    cos, sin = cos_ref[...], sin_ref[...]
    kr_rot = p[:, o_kr:o_kr + LANES] * cos + p[:, o_kr + LANES:o_kr + 2 * LANES] * sin
    lane = lax.broadcasted_iota(jnp.int32, (1, 2 * LANES), 1)
    ones_col = jnp.where(jnp.bitwise_and(lane, LANES - 1) == MLA_V, 1.0, 0.0)
    cos2 = jnp.concatenate([cos, cos], axis=1)
    sin2 = jnp.concatenate([sin, sin], axis=1)
    kr2 = jnp.concatenate([kr_rot, kr_rot], axis=1)
    half = MLA_HEADS * LANES
    for c in range(MLA_HEADS // 2):
        cs = slice(2 * LANES * c, 2 * LANES * (c + 1))
        cs_b = slice(half + 2 * LANES * c, half + 2 * LANES * (c + 1))
        q = _dot(cq, wuq_ref[:, cs]) * cos2 + _dot(cq, wuq_ref[:, cs_b]) * sin2
        q_ref[:, cs] = (q * MLA_SCALE).astype(q_ref.dtype)
        k_ref[:, cs] = (_dot(ckv, wukv_ref[:, cs]) + kr2).astype(k_ref.dtype)
        v_ref[:, cs] = (_dot(ckv, wukv_ref[:, cs_b]) + ones_col).astype(v_ref.dtype)


def _mla_pre(x, g, sh, sc, win, qg, kvg, wuq, wukv, cos, sin, tiles_per_batch, n_batch):
    t, d = x.shape
    tm = ROW_TILE
    n = MLA_HEADS * LANES
    mod = _mod_spec(d, tiles_per_batch, n_batch)
    row = lambda i: (i, 0)
    return pl.pallas_call(
        _mla_pre_kernel,
        grid=(t // tm,),
        in_specs=[pl.BlockSpec((tm, d), row), _full((1, d)), mod, mod, _full(win.shape), _full(qg.shape),
                  _full(kvg.shape), _full(wuq.shape), _full(wukv.shape),
                  pl.BlockSpec((tm, LANES), row), pl.BlockSpec((tm, LANES), row)],
        out_specs=[pl.BlockSpec((tm, n), row)] * 3,
        out_shape=[jax.ShapeDtypeStruct((t, n), BF16)] * 3,
        compiler_params=_params(("parallel",)),
        name="mla_pre",
    )(x, g, sh, sc, win, qg, kvg, wuq, wukv, cos, sin)


def _flash_step(q, kb, vb, m, acc):
    s = _dot_nt(q, kb)
    m_new = jnp.maximum(m, jnp.max(s, axis=-1, keepdims=True))
    p = jnp.exp(s - m_new)
    return m_new, acc * jnp.exp(m - m_new) + _dot(p.astype(BF16), vb)


def _flash_kernel(tk, q_ref, k_ref, v_ref, kc_ref, vc_ref, o_ref):
    q = q_ref[...]
    tq = q.shape[0]

    def body(i, carry):
        off = pl.multiple_of(i * tk, tk)
        return _flash_step(q, k_ref[pl.ds(off, tk), :], v_ref[pl.ds(off, tk), :], *carry)

    carry = (jnp.full((tq, 1), NEG, F32), jnp.zeros((tq, LANES), F32))
    carry = lax.fori_loop(0, k_ref.shape[0] // tk, body, carry)
    _, acc = _flash_step(q, kc_ref[...], vc_ref[...], *carry)
    o_ref[...] = (acc / acc[:, MLA_V:MLA_V + 1]).astype(o_ref.dtype)


def _mla_attention(q, k, v, n_batch, seq, n_ctx):
    tq = min(FLASH_TQ, seq)
    tk = min(FLASH_TK, seq)
    nq = seq // tq
    ctx0 = n_batch * seq // n_ctx
    blk = lambda rows, f: pl.BlockSpec((rows, LANES), f)
    return pl.pallas_call(
        functools.partial(_flash_kernel, tk),
        grid=(n_batch, MLA_HEADS, nq),
        in_specs=[blk(tq, lambda b, h, i: (b * nq + i, h)),
                  blk(seq, lambda b, h, i: (b, h)), blk(seq, lambda b, h, i: (b, h)),
                  blk(n_ctx, lambda b, h, i: (ctx0 + b, h)), blk(n_ctx, lambda b, h, i: (ctx0 + b, h))],
        out_specs=blk(tq, lambda b, h, i: (b * nq + i, h)),
        out_shape=jax.ShapeDtypeStruct((n_batch * seq, MLA_HEADS * LANES), BF16),
        compiler_params=_params(("parallel", "parallel", "arbitrary")),
        name="mla_flash",
    )(q, k, v, k, v)


def _router_kernel(x_ref, g_ref, sh_ref, sc_ref, wr_ref, br_ref, tri_ref, h_ref, mi_ref, mf_ref, cnt_ref, run_ref):
    @pl.when(pl.program_id(0) == 0)
    def _init():
        run_ref[...] = jnp.zeros_like(run_ref)

    h = _norm_mod(x_ref[...], g_ref[...], sh_ref[...], sc_ref[...])
    h_ref[...] = h
    logits = jnp.dot(h, wr_ref[...], preferred_element_type=F32, precision=lax.Precision.HIGHEST) + br_ref[...]
    lane = lax.broadcasted_iota(jnp.int32, logits.shape, 1).astype(F32)
    m1 = jnp.max(logits, axis=-1, keepdims=True)
    i1 = jnp.min(jnp.where(logits == m1, lane, float(LANES)), axis=-1, keepdims=True)
    rest = jnp.where(lane == i1, NEG, logits)
    m2 = jnp.max(rest, axis=-1, keepdims=True)
    i2 = jnp.min(jnp.where(rest == m2, lane, float(LANES)), axis=-1, keepdims=True)
    e = jnp.exp(m2 - m1)
    w1 = 1.0 / (1.0 + e)
    w2 = e / (1.0 + e)
    hit1, hit2 = lane == i1, lane == i2
    onehot = jnp.where(jnp.logical_or(hit1, hit2), 1.0, 0.0)
    before = _dot(tri_ref[...], onehot.astype(BF16)) + run_ref[0:1, :]
    r1 = jnp.sum(jnp.where(hit1, before, 0.0), axis=-1, keepdims=True)
    r2 = jnp.sum(jnp.where(hit2, before, 0.0), axis=-1, keepdims=True)
    run_ref[...] = run_ref[...] + jnp.sum(onehot, axis=0, keepdims=True)
    meta = jnp.where(lane == 0.0, i1, jnp.where(lane == 1.0, i2, jnp.where(lane == 2.0, r1, jnp.where(lane == 3.0, r2, 0.0))))
    mi_ref[...] = meta.astype(jnp.int32)
    mf_ref[...] = jnp.where(lane == 0.0, w1, jnp.where(lane == 1.0, w2, 0.0))
    cnt_ref[...] = run_ref[...]


def _router(x, g, sh, sc, wr, br, tri, n_rows, tiles_per_batch, n_batch):
    d = x.shape[1]
    tm = ROW_TILE
    mod = _mod_spec(d, tiles_per_batch, n_batch)
    row = lambda i: (i, 0)
    return pl.pallas_call(
        _router_kernel,
        grid=(n_rows // tm,),
        in_specs=[pl.BlockSpec((tm, d), row), _full((1, d)), mod, mod, _full(wr.shape), _full(br.shape),
                  _full(tri.shape)],
        out_specs=[pl.BlockSpec((tm, d), row), pl.BlockSpec((tm, LANES), row), pl.BlockSpec((tm, LANES), row),
                   _full((8, LANES))],
        out_shape=[jax.ShapeDtypeStruct((n_rows, d), F32), jax.ShapeDtypeStruct((n_rows, LANES), jnp.int32),
                   jax.ShapeDtypeStruct((n_rows, LANES), F32), jax.ShapeDtypeStruct((8, LANES), F32)],
        scratch_shapes=[pltpu.VMEM((8, LANES), F32)],
        compiler_params=_params(("arbitrary",)),
        name="moe_router",
    )(x, g, sh, sc, wr, br, tri)


def _row_gather(src_hbm, idx_ref, base, n, buf, sem):
    def issue(j, carry):
        pltpu.make_async_copy(src_hbm.at[pl.ds(idx_ref[base + j], 1), :], buf.at[pl.ds(j, 1), :], sem).start()
        return carry

    lax.fori_loop(0, n, issue, 0)
    pltpu.make_async_copy(src_hbm.at[pl.ds(0, n), :], buf, sem).wait()


def _moe_kernel(te_ref, src_ref, nv_ref, h_hbm, wg_ref, wu_ref, wd_ref, y_ref, hbuf, sem):
    i, f = pl.program_id(0), pl.program_id(1)
    tm = hbuf.shape[0]
    valid = i < nv_ref[0]

    @pl.when(jnp.logical_and(valid, f == 0))
    def _gather():
        _row_gather(h_hbm, src_ref, i * tm, tm, hbuf, sem)

    @pl.when(valid)
    def _compute():
        hb = hbuf[...].astype(BF16)
        act = (_silu(_dot(hb, wg_ref[...])) * _dot(hb, wu_ref[...])).astype(BF16)
        part = _dot(act, wd_ref[...])

        @pl.when(f == 0)
        def _():
            y_ref[...] = part

        @pl.when(f > 0)
        def _():
            y_ref[...] = y_ref[...] + part

    @pl.when(jnp.logical_and(jnp.logical_not(valid), f == 0))
    def _unused_tile():
        y_ref[...] = jnp.zeros_like(y_ref)


def _moe_experts(tile_expert, src, n_valid, h, wg, wu, wd, n_tiles):
    d = h.shape[1]
    tm = MOE_TILE
    nf = MOE_FCHUNKS
    fc = wg.shape[2] // nf

    def f_idx(i, f, nv):
        return jnp.where(i < nv[0], f, nf - 1)

    grid_spec = pltpu.PrefetchScalarGridSpec(
        num_scalar_prefetch=3,
        grid=(n_tiles, nf),
        in_specs=[pl.BlockSpec(memory_space=pl.ANY),
                  pl.BlockSpec((None, d, fc), lambda i, f, te, src, nv: (te[i], 0, f_idx(i, f, nv))),
                  pl.BlockSpec((None, d, fc), lambda i, f, te, src, nv: (te[i], 0, f_idx(i, f, nv))),
                  pl.BlockSpec((None, fc, d), lambda i, f, te, src, nv: (te[i], f_idx(i, f, nv), 0))],
        out_specs=pl.BlockSpec((tm, d), lambda i, f, te, src, nv: (i, 0)),
        scratch_shapes=[pltpu.VMEM((tm, d), F32), pltpu.SemaphoreType.DMA(())],
    )
    return pl.pallas_call(
        _moe_kernel,
        grid_spec=grid_spec,
        out_shape=jax.ShapeDtypeStruct((n_tiles * tm, d), F32),
        compiler_params=_params(("arbitrary", "arbitrary")),
        name="moe_experts",
    )(tile_expert, src, n_valid, h, wg, wu, wd)


def _combine_kernel(d1_ref, d2_ref, x_ref, gate_ref, mf_ref, fg_ref, y_hbm, o_ref, buf1, buf2, sem1, sem2):
    i = pl.program_id(0)
    tm = buf1.shape[0]

    def issue(j, carry):
        pltpu.make_async_copy(y_hbm.at[pl.ds(d1_ref[i * tm + j], 1), :], buf1.at[pl.ds(j, 1), :], sem1).start()
        pltpu.make_async_copy(y_hbm.at[pl.ds(d2_ref[i * tm + j], 1), :], buf2.at[pl.ds(j, 1), :], sem2).start()
        return carry

    lax.fori_loop(0, tm, issue, 0)
    pltpu.make_async_copy(y_hbm.at[pl.ds(0, tm), :], buf1, sem1).wait()
    pltpu.make_async_copy(y_hbm.at[pl.ds(0, tm), :], buf2, sem2).wait()
    w = mf_ref[...]
    y = w[:, 0:1] * buf1[...] + w[:, 1:2] * buf2[...]
    o_ref[...] = _rms(x_ref[...] + gate_ref[...] * y, fg_ref[...])


def _moe_combine(d1, d2, x, gate, mf, fg, y, tiles_per_batch_512, n_batch):
    t, d = x.shape
    tm = COMBINE_TILE
    per = tiles_per_batch_512 * (ROW_TILE // tm)
    grid_spec = pltpu.PrefetchScalarGridSpec(
        num_scalar_prefetch=2,
        grid=(t // tm,),
        in_specs=[pl.BlockSpec((tm, d), lambda i, a, b: (i, 0)),
                  pl.BlockSpec((None, 1, d), lambda i, a, b: (jnp.minimum(i // per, n_batch), 0, 0)),
                  pl.BlockSpec((tm, LANES), lambda i, a, b: (i, 0)),
                  pl.BlockSpec((1, d), lambda i, a, b: (0, 0)),
                  pl.BlockSpec(memory_space=pl.ANY)],
        out_specs=pl.BlockSpec((tm, d), lambda i, a, b: (i, 0)),
        scratch_shapes=[pltpu.VMEM((tm, d), F32), pltpu.VMEM((tm, d), F32),
                        pltpu.SemaphoreType.DMA(()), pltpu.SemaphoreType.DMA(())],
    )
    return pl.pallas_call(
        _combine_kernel,
        grid_spec=grid_spec,
        out_shape=jax.ShapeDtypeStruct((t, d), F32),
        compiler_params=_params(("arbitrary",)),
        name="moe_combine",
    )(d1, d2, x, gate, mf, fg, y)


def kernel(x, c, ctx, c_ctx, ada_w, ada_b, norm_g, final_g, na_w_in, na_rpb, sg_w, sg_b, sg_norm_g, even_w_out,
           ffn_w_gate, ffn_w_up, ffn_w_down, mla_w_in, mla_q_norm_g, mla_kv_norm_g, mla_w_uq, mla_w_ukv, mla_w_out,
           moe_w_router, moe_b_router, moe_w_gate, moe_w_up, moe_w_down):
    n_batch, seq, d = x.shape
    n_ctx = ctx.shape[1]
    n_lat = n_batch * seq
    assert ada_w.shape[0] == 2 and seq % ROW_TILE == 0 and (n_batch * n_ctx) % ROW_TILE == 0
    assert n_batch + 1 <= 8 and seq % GRID_W == 0
    tpb = seq // ROW_TILE

    cond8 = jnp.concatenate([c, c_ctx[None, :], jnp.zeros((8 - n_batch - 1, d), F32)], axis=0)
    mods = _ada_mod(cond8, ada_w, ada_b)[:, :n_batch + 1]
    mod = lambda layer, k: mods[layer, :, k * d:(k + 1) * d].reshape(n_batch + 1, 1, d)
    xs = jnp.concatenate([x.reshape(n_lat, d), ctx.reshape(n_batch * n_ctx, d)], axis=0)

    w_in = na_w_in[0].astype(BF16)
    qkv, ug = _even_in(xs, norm_g[0, 0][None], mod(0, 0), mod(0, 1), w_in[:, :3 * A_WIDTH], w_in[:, 3 * A_WIDTH:],
                       tpb, n_batch)
    bias = _na_bias_table(na_rpb[0], seq // GRID_W)
    attn = _na_attention(qkv, bias, n_batch, seq, n_ctx)
    bmat = jnp.repeat(sg_b[0].T, SG_GROUP_DIM, axis=1)
    avg = jnp.asarray(np.kron(np.eye(SG_GROUPS), np.full((SG_GROUP_DIM, SG_GROUP_DIM), 1.0 / SG_GROUP_DIM)), BF16)
    gated = _spatial_gating(ug, sg_w[0].astype(BF16), bmat, sg_norm_g[0][None], avg)
    w_out = even_w_out[0].astype(BF16)
    xs = _proj_residual(xs, mod(0, 2), [attn, gated], [w_out[:A_WIDTH], w_out[A_WIDTH:]], xs.shape[0], tpb, n_batch)
    xs = _ffn(xs, norm_g[0, 1][None], mod(0, 3), mod(0, 4), mod(0, 5), ffn_w_gate[0].astype(BF16),
              ffn_w_up[0].astype(BF16), ffn_w_down[0].astype(BF16), tpb, n_batch)

    win, wuq, wukv, wo = _mla_weights(mla_w_in[0], mla_w_uq[0], mla_w_ukv[0], mla_w_out[0])
    cos, sin = _rope_tables(n_batch, seq, n_ctx)
    q, k, v = _mla_pre(xs, norm_g[1, 0][None], mod(1, 0), mod(1, 1), win, mla_q_norm_g[0][None],
                       mla_kv_norm_g[0][None], wuq, wukv, cos, sin, tpb, n_batch)
    o = _mla_attention(q, k, v, n_batch, seq, n_ctx)
    x1 = _proj_residual(xs, mod(1, 2), [o], [wo], n_lat, tpb, n_batch)

    wr = jnp.concatenate([moe_w_router[0], jnp.zeros((d, LANES - N_EXPERTS), F32)], axis=1)
    br = jnp.concatenate([moe_b_router[0], jnp.full((LANES - N_EXPERTS,), NEG, F32)])[None]
    tri = jnp.asarray(np.tril(np.ones((ROW_TILE, ROW_TILE), np.float32), -1), BF16)
    h, mi, mf, cnt = _router(x1, norm_g[1, 1][None], mod(1, 3), mod(1, 4), wr, br, tri, n_lat, tpb, n_batch)

    counts = cnt[0, :N_EXPERTS].astype(jnp.int32)
    tiles_e = (counts + MOE_TILE - 1) // MOE_TILE
    tile_end = jnp.cumsum(tiles_e)
    start = (tile_end - tiles_e) * MOE_TILE
    d1 = start[mi[:, 0]] + mi[:, 2]
    d2 = start[mi[:, 1]] + mi[:, 3]
    n_tiles = 2 * n_lat // MOE_TILE + N_EXPERTS
    tok = jnp.arange(n_lat, dtype=jnp.int32)
    src = jnp.zeros((n_tiles * MOE_TILE,), jnp.int32).at[d1].set(tok).at[d2].set(tok)
    n_valid = tile_end[-1:]
    last_tile = jnp.maximum(n_valid[0] - 1, 0)
    tile_ids = jnp.minimum(jnp.arange(n_tiles, dtype=jnp.int32), last_tile)
    tile_expert = jnp.minimum(jnp.searchsorted(tile_end, tile_ids, side="right"), N_EXPERTS - 1).astype(jnp.int32)
    y = _moe_experts(tile_expert, src, n_valid.astype(jnp.int32), h, moe_w_gate[0].astype(BF16),
                     moe_w_up[0].astype(BF16), moe_w_down[0].astype(BF16), n_tiles)
    out = _moe_combine(d1.astype(jnp.int32), d2.astype(jnp.int32), x1, mod(1, 5), mf, final_g[None], y, tpb, n_batch)
    return out.reshape(n_batch, seq, d)
```

```python
import functools

import numpy as np
import jax
import jax.numpy as jnp
from jax import lax
from jax.experimental import pallas as pl
from jax.experimental.pallas import tpu as pltpu

F32 = jnp.float32
BF16 = jnp.bfloat16
EPS = 1e-6
NEG = -1e30

LANES = 128
VMEM_LIMIT_BYTES = 56 * 1024 * 1024

GRID_W = 64
NA_HEADS = 8
NA_HEAD_DIM = 64
NA_WIN_R = 8
NA_WIN_C = 16
SG_GROUPS = 8
SG_GROUP_DIM = 64
SG_CHUNK = 128
A_WIDTH = NA_HEADS * NA_HEAD_DIM
B_WIDTH = SG_GROUPS * SG_GROUP_DIM
MLA_HEADS = 16
MLA_NOPE = 64
MLA_ROPE = 32
MLA_V = 64
MLA_Q_RANK = 384
MLA_KV_RANK = 256
MLA_SCALE = (MLA_NOPE + MLA_ROPE) ** -0.5
ROPE_BASE = 10000.0
N_EXPERTS = 8

ROW_TILE = 512
NA_QROWS = 4
NA_QB = NA_QROWS * GRID_W
NA_KB = 3 * NA_QB
FLASH_TQ = 512
FLASH_TK = 1024
MOE_TILE = 512
MOE_FCHUNKS = 2
COMBINE_TILE = 256


def _params(sem):
    return pltpu.CompilerParams(dimension_semantics=sem, vmem_limit_bytes=VMEM_LIMIT_BYTES)


def _dot(a, b):
    return jnp.dot(a, b, preferred_element_type=F32)


def _dot_nt(a, b):
    return lax.dot_general(a, b, (((1,), (1,)), ((), ())), preferred_element_type=F32)


def _silu(x):
    return x / (1.0 + jnp.exp(-x))


def _gelu_tanh(x):
    return 0.5 * x * (1.0 + jnp.tanh(0.7978845608028654 * (x + 0.044715 * (x * x * x))))


def _rms(x, g):
    return x * lax.rsqrt(jnp.mean(x * x, axis=-1, keepdims=True) + EPS) * g


def _norm_mod(x, g, sh, sc):
    return _rms(x, g) * (1.0 + sc) + sh


def _full(shape):
    n = len(shape)
    return pl.BlockSpec(shape, lambda *_: (0,) * n)


def _ada_kernel(cond_ref, w_ref, b_ref, o_ref):
    c = cond_ref[...]
    o_ref[...] = jnp.dot(_silu(c), w_ref[...], preferred_element_type=F32,
                         precision=lax.Precision.HIGHEST) + b_ref[...]


def _ada_mod(cond8, ada_w, ada_b):
    depth, d, n = ada_w.shape
    tn = n // 4
    return pl.pallas_call(
        _ada_kernel,
        grid=(depth, n // tn),
        in_specs=[_full((8, d)),
                  pl.BlockSpec((None, d, tn), lambda l, j: (l, 0, j)),
                  pl.BlockSpec((None, 1, tn), lambda l, j: (l, 0, j))],
        out_specs=pl.BlockSpec((None, 8, tn), lambda l, j: (l, 0, j)),
        out_shape=jax.ShapeDtypeStruct((depth, 8, n), F32),
        compiler_params=_params(("parallel", "parallel")),
        name="ada_mod",
    )(cond8, ada_w, ada_b.reshape(depth, 1, n))


def _group_map(tiles_per_batch, n_batch):
    return lambda t: (jnp.minimum(t // tiles_per_batch, n_batch), 0, 0)


def _mod_spec(d, tiles_per_batch, n_batch):
    return pl.BlockSpec((None, 1, d), _group_map(tiles_per_batch, n_batch))


def _even_in_kernel(x_ref, g_ref, sh_ref, sc_ref, wqkv_ref, wug_ref, qkv_ref, ug_ref):
    h = _norm_mod(x_ref[...], g_ref[...], sh_ref[...], sc_ref[...]).astype(BF16)
    qkv_ref[...] = _dot(h, wqkv_ref[...]).astype(BF16)
    ug_ref[...] = _dot(h, wug_ref[...])


def _even_in(x, g, sh, sc, wqkv, wug, tiles_per_batch, n_batch):
    t, d = x.shape
    tm = ROW_TILE
    mod = _mod_spec(d, tiles_per_batch, n_batch)
    return pl.pallas_call(
        _even_in_kernel,
        grid=(t // tm,),
        in_specs=[pl.BlockSpec((tm, d), lambda i: (i, 0)), _full((1, d)), mod, mod,
                  _full(wqkv.shape), _full(wug.shape)],
        out_specs=[pl.BlockSpec((tm, wqkv.shape[1]), lambda i: (i, 0)),
                   pl.BlockSpec((tm, wug.shape[1]), lambda i: (i, 0))],
        out_shape=[jax.ShapeDtypeStruct((t, wqkv.shape[1]), BF16),
                   jax.ShapeDtypeStruct((t, wug.shape[1]), F32)],
        compiler_params=_params(("parallel",)),
        name="even_in",
    )(x, g, sh, sc, wqkv, wug)


def _na_bias_table(rpb, rows):
    nblk = rows // NA_QROWS
    n_heads = rpb.shape[0]
    win_rows = NA_KB // GRID_W
    qc = np.arange(GRID_W)
    col_start = np.clip(qc - NA_WIN_C // 2, 0, GRID_W - NA_WIN_C)
    col_ok = (qc[None, :] >= col_start[:, None]) & (qc[None, :] < col_start[:, None] + NA_WIN_C)
    col_j = qc[None, :] - qc[:, None] + NA_WIN_C - 1
    sel_c = ((col_j[None] == np.arange(2 * NA_WIN_C - 1)[:, None, None]) & col_ok[None]).astype(np.float32)
    toeplitz = jnp.einsum("hij,jqk->hiqk", rpb, sel_c, precision=lax.Precision.HIGHEST)
    sel_r = np.zeros((3, NA_QROWS, win_rows, 2 * NA_WIN_R - 1), np.float32)
    for kind, j in enumerate((0, 1, nblk - 1)):
        first_row = NA_QROWS * int(np.clip(j - 1, 0, nblk - 3))
        for a in range(NA_QROWS):
            r = NA_QROWS * j + a
            row_start = int(np.clip(r - NA_WIN_R // 2, 0, rows - NA_WIN_R))
            for b in range(win_rows):
                kr = first_row + b
                if row_start <= kr < row_start + NA_WIN_R:
                    sel_r[kind, a, b, kr - r + NA_WIN_R - 1] = 1.0
    tab = jnp.einsum("cabi,hiqk->chaqbk", sel_r, toeplitz, precision=lax.Precision.HIGHEST)
    row_ok = sel_r.sum(-1) > 0
    ok = jnp.asarray(row_ok)[:, None, :, None, :, None] & jnp.asarray(col_ok)[None, None, None, :, None, :]
    tab = jnp.where(ok, tab, NEG)
    return tab.reshape(3, n_heads, NA_QB, NA_KB)


def _na_kernel(q_ref, k0_ref, k1_ref, k2_ref, v0_ref, v1_ref, v2_ref, kc_ref, vc_ref, bias_ref, o_ref):
    jj = pl.program_id(2)
    lane = lax.broadcasted_iota(jnp.int32, (1, LANES), 1)
    first = lane < NA_HEAD_DIM
    q = q_ref[...]
    zero = jnp.zeros_like(q)
    scale = NA_HEAD_DIM ** -0.5

    def head_q(hh):
        return jnp.where(first if hh == 0 else jnp.logical_not(first), q, zero)

    @pl.when(jj == 0)
    def _ctx_queries():
        outs = []
        for hh in range(2):
            s = _dot_nt(head_q(hh), kc_ref[...]) * scale
            p = jnp.exp(s - jnp.max(s, axis=-1, keepdims=True))
            l = jnp.sum(p, axis=-1, keepdims=True)
            outs.append(_dot(p.astype(BF16), vc_ref[...]) / l)
        o_ref[...] = jnp.where(first, outs[0], outs[1]).astype(o_ref.dtype)

    @pl.when(jj > 0)
    def _latent_queries():
        k_refs = (k0_ref, k1_ref, k2_ref)
        v_refs = (v0_ref, v1_ref, v2_ref)
        outs = []
        for hh in range(2):
            qh = head_q(hh)
            s = [_dot_nt(qh, k_refs[i][...]) * scale + bias_ref[hh, :, NA_QB * i:NA_QB * (i + 1)]
                 for i in range(3)]
            s.append(_dot_nt(qh, kc_ref[...]) * scale)
            m = functools.reduce(jnp.maximum, [jnp.max(x, axis=-1, keepdims=True) for x in s])
            p = [jnp.exp(x - m) for x in s]
            l = functools.reduce(jnp.add, [jnp.sum(x, axis=-1, keepdims=True) for x in p])
            vals = [v_refs[i][...] for i in range(3)] + [vc_ref[...]]
            o = functools.reduce(jnp.add, [_dot(p[i].astype(BF16), vals[i]) for i in range(4)])
            outs.append(o / l)
        o_ref[...] = jnp.where(first, outs[0], outs[1]).astype(o_ref.dtype)


def _na_attention(qkv, bias, n_batch, seq, n_ctx):
    t = qkv.shape[0]
    assert n_ctx == NA_QB and seq % NA_QB == 0
    nblk = seq // NA_QB
    assert nblk >= 3
    hp = NA_HEADS // 2
    kcol, vcol = A_WIDTH // LANES, 2 * A_WIDTH // LANES
    ctx_blk = n_batch * nblk

    def q_map(b, h, j):
        return (jnp.where(j == 0, ctx_blk + b, b * nblk + j - 1), h)

    def win_map(i, col):
        return lambda b, h, j: (b * nblk + jnp.clip(j - 2, 0, nblk - 3) + i, col + h)

    def ctx_map(col):
        return lambda b, h, j: (ctx_blk + b, col + h)

    def bias_map(b, h, j):
        return (jnp.where(j <= 1, 0, jnp.where(j == nblk, 2, 1)), h, 0, 0)

    blk = (NA_QB, LANES)
    in_specs = ([pl.BlockSpec(blk, q_map)]
                + [pl.BlockSpec(blk, win_map(i, kcol)) for i in range(3)]
                + [pl.BlockSpec(blk, win_map(i, vcol)) for i in range(3)]
                + [pl.BlockSpec(blk, ctx_map(kcol)), pl.BlockSpec(blk, ctx_map(vcol)),
                   pl.BlockSpec((None, 2, NA_QB, NA_KB), bias_map)])
    return pl.pallas_call(
        _na_kernel,
        grid=(n_batch, hp, nblk + 1),
        in_specs=in_specs,
        out_specs=pl.BlockSpec(blk, q_map),
        out_shape=jax.ShapeDtypeStruct((t, A_WIDTH), BF16),
        compiler_params=_params(("parallel", "parallel", "arbitrary")),
        name="na_attention",
    )(*([qkv] * 9), bias)


def _seg_mean(x, avg):
    hi = x.astype(BF16)
    lo = (x - hi.astype(F32)).astype(BF16)
    return _dot(hi, avg) + _dot(lo, avg)


def _sg_kernel(ug_ref, ws_ref, bm_ref, ng_ref, avg_ref, o_ref):
    lane = lax.broadcasted_iota(jnp.int32, (1, LANES), 1)
    first = lane < SG_GROUP_DIM
    avg = avg_ref[...]
    for c in range(ROW_TILE // SG_CHUNK):
        rows = slice(SG_CHUNK * c, SG_CHUNK * (c + 1))
        u = _gelu_tanh(ug_ref[rows, 0:B_WIDTH])
        g = _gelu_tanh(ug_ref[rows, B_WIDTH:2 * B_WIDTH])
        d = g - _seg_mean(g, avg)
        var = _seg_mean(d * d, avg)
        gn = (d * lax.rsqrt(var + EPS) * ng_ref[...]).astype(BF16)
        parts = []
        for j in range(SG_GROUPS // 2):
            gj = gn[:, LANES * j:LANES * (j + 1)]
            parts.append(jnp.where(first, _dot(ws_ref[2 * j], gj), _dot(ws_ref[2 * j + 1], gj)))
        mixed = jnp.concatenate(parts, axis=1) + bm_ref[...]
        o_ref[rows, :] = (u * mixed).astype(o_ref.dtype)


def _spatial_gating(ug, ws, bmat, ng, avg):
    t = ug.shape[0]
    tm = ROW_TILE
    return pl.pallas_call(
        _sg_kernel,
        grid=(t // tm,),
        in_specs=[pl.BlockSpec((tm, 2 * B_WIDTH), lambda i: (i, 0)), _full(ws.shape), _full(bmat.shape),
                  _full(ng.shape), _full(avg.shape)],
        out_specs=pl.BlockSpec((tm, B_WIDTH), lambda i: (i, 0)),
        out_shape=jax.ShapeDtypeStruct((t, B_WIDTH), BF16),
        compiler_params=_params(("parallel",)),
        name="spatial_gating",
    )(ug, ws, bmat, ng, avg)


def _proj_res_kernel(n_in, x_ref, gate_ref, *refs):
    a_refs, w_refs, o_ref = refs[:n_in], refs[n_in:2 * n_in], refs[2 * n_in]
    y = functools.reduce(jnp.add, [_dot(a[...], w[...]) for a, w in zip(a_refs, w_refs)])
    o_ref[...] = x_ref[...] + gate_ref[...] * y


def _proj_residual(x, gate, acts, weights, n_rows, tiles_per_batch, n_batch):
    d = x.shape[1]
    tm = ROW_TILE
    row = lambda i: (i, 0)
    return pl.pallas_call(
        functools.partial(_proj_res_kernel, len(acts)),
        grid=(n_rows // tm,),
        in_specs=([pl.BlockSpec((tm, d), row), _mod_spec(d, tiles_per_batch, n_batch)]
                  + [pl.BlockSpec((tm, a.shape[1]), row) for a in acts]
                  + [_full(w.shape) for w in weights]),
        out_specs=pl.BlockSpec((tm, d), row),
        out_shape=jax.ShapeDtypeStruct((n_rows, d), F32),
        compiler_params=_params(("parallel",)),
        name="proj_residual",
    )(x, gate, *acts, *weights)


def _ffn_kernel(n_chunks, x_ref, g_ref, sh_ref, sc_ref, gate_ref, wg_ref, wu_ref, wd_ref, o_ref):
    x = x_ref[...]
    h = _norm_mod(x, g_ref[...], sh_ref[...], sc_ref[...]).astype(BF16)
    fc = wg_ref.shape[1] // n_chunks
    acc = None
    for c in range(n_chunks):
        cs = slice(fc * c, fc * (c + 1))
        act = (_silu(_dot(h, wg_ref[:, cs])) * _dot(h, wu_ref[:, cs])).astype(BF16)
        part = _dot(act, wd_ref[cs, :])
        acc = part if acc is None else acc + part
    o_ref[...] = x + gate_ref[...] * acc


def _ffn(x, g, sh, sc, gate, wg, wu, wd, tiles_per_batch, n_batch):
    t, d = x.shape
    tm = ROW_TILE
    f = wg.shape[1]
    n_chunks = 2 if f % (2 * LANES) == 0 else 1
    mod = _mod_spec(d, tiles_per_batch, n_batch)
    resident = lambda w: pl.BlockSpec(w.shape, lambda i: (0, 0), pipeline_mode=pl.Buffered(1))
    return pl.pallas_call(
        functools.partial(_ffn_kernel, n_chunks),
        grid=(t // tm,),
        in_specs=[pl.BlockSpec((tm, d), lambda i: (i, 0)), _full((1, d)), mod, mod, mod,
                  resident(wg), resident(wu), resident(wd)],
        out_specs=pl.BlockSpec((tm, d), lambda i: (i, 0)),
        out_shape=jax.ShapeDtypeStruct((t, d), F32),
        compiler_params=_params(("parallel",)),
        name="ffn",
    )(x, g, sh, sc, gate, wg, wu, wd)


def _rope_tables(n_batch, seq, n_ctx):
    t = np.arange(seq)
    pos = np.stack([t // GRID_W, t % GRID_W], axis=-1).astype(np.float32)
    n_freq = MLA_ROPE // 4
    inv = jnp.power(ROPE_BASE, -jnp.arange(n_freq, dtype=F32) / n_freq)
    ang = jnp.asarray(pos)[:, :, None] * inv
    cos, sin = jnp.cos(ang), jnp.sin(ang)
    cos_r = jnp.concatenate([cos, cos], axis=-1).reshape(seq, MLA_ROPE)
    sin_r = jnp.concatenate([-sin, sin], axis=-1).reshape(seq, MLA_ROPE)
    ones = jnp.ones((seq, MLA_NOPE), F32)
    pad = LANES - MLA_NOPE - MLA_ROPE
    cos_l = jnp.concatenate([ones, cos_r, jnp.ones((seq, pad), F32)], axis=-1)
    sin_l = jnp.concatenate([0 * ones, sin_r, jnp.zeros((seq, pad), F32)], axis=-1)
    cos_all = jnp.concatenate([jnp.tile(cos_l, (n_batch, 1)), jnp.ones((n_batch * n_ctx, LANES), F32)])
    sin_all = jnp.concatenate([jnp.tile(sin_l, (n_batch, 1)), jnp.zeros((n_batch * n_ctx, LANES), F32)])
    return cos_all, sin_all


def _swap_rope_cols(w_rope):
    q = MLA_ROPE // 4
    return jnp.concatenate([w_rope[..., q:2 * q], w_rope[..., 0:q], w_rope[..., 3 * q:4 * q],
                            w_rope[..., 2 * q:3 * q]], axis=-1)


def _mla_weights(w_in, w_uq, w_ukv, w_out):
    d = w_in.shape[0]
    pad = LANES - MLA_NOPE - MLA_ROPE
    w_kr = w_in[:, MLA_Q_RANK + MLA_KV_RANK:]
    lanes_kr = lambda w: jnp.concatenate([jnp.zeros((d, MLA_NOPE), F32), w, jnp.zeros((d, pad), F32)], axis=1)
    win = jnp.concatenate([w_in[:, :MLA_Q_RANK + MLA_KV_RANK], lanes_kr(w_kr), lanes_kr(_swap_rope_cols(w_kr))],
                          axis=1)
    uq = w_uq.reshape(MLA_Q_RANK, MLA_HEADS, MLA_NOPE + MLA_ROPE)
    zq = jnp.zeros((MLA_Q_RANK, MLA_HEADS, pad), F32)
    uq_a = jnp.concatenate([uq, zq], axis=-1).reshape(MLA_Q_RANK, MLA_HEADS * LANES)
    uq_b = jnp.concatenate([jnp.zeros((MLA_Q_RANK, MLA_HEADS, MLA_NOPE), F32),
                            _swap_rope_cols(uq[..., MLA_NOPE:]), zq], axis=-1).reshape(MLA_Q_RANK, MLA_HEADS * LANES)
    ukv = w_ukv.reshape(MLA_KV_RANK, MLA_HEADS, MLA_NOPE + MLA_V)
    zk = jnp.zeros((MLA_KV_RANK, MLA_HEADS, LANES - MLA_NOPE), F32)
    uk = jnp.concatenate([ukv[..., :MLA_NOPE], zk], axis=-1).reshape(MLA_KV_RANK, MLA_HEADS * LANES)
    zv = jnp.zeros((MLA_KV_RANK, MLA_HEADS, LANES - MLA_V), F32)
    uv = jnp.concatenate([ukv[..., MLA_NOPE:], zv], axis=-1).reshape(MLA_KV_RANK, MLA_HEADS * LANES)
    wo = w_out.reshape(MLA_HEADS, MLA_V, -1)
    wo = jnp.concatenate([wo, jnp.zeros((MLA_HEADS, LANES - MLA_V, wo.shape[-1]), F32)], axis=1)
    return (win.astype(BF16), jnp.concatenate([uq_a, uq_b], axis=1).astype(BF16),
            jnp.concatenate([uk, uv], axis=1).astype(BF16), wo.reshape(MLA_HEADS * LANES, -1).astype(BF16))


def _mla_pre_kernel(x_ref, g_ref, sh_ref, sc_ref, win_ref, qg_ref, kvg_ref, wuq_ref, wukv_ref, cos_ref, sin_ref,
                    q_ref, k_ref, v_ref):
    h = _norm_mod(x_ref[...], g_ref[...], sh_ref[...], sc_ref[...]).astype(BF16)
    p = _dot(h, win_ref[...])
    cq = _rms(p[:, :MLA_Q_RANK], qg_ref[...]).astype(BF16)
    ckv = _rms(p[:, MLA_Q_RANK:MLA_Q_RANK + MLA_KV_RANK], kvg_ref[...]).astype(BF16)
    o_kr = MLA_Q_RANK + MLA_KV_RANK
    cos, sin = cos_ref[...], sin_ref[...]
    kr_rot = p[:, o_kr:o_kr + LANES] * cos + p[:, o_kr + LANES:o_kr + 2 * LANES] * sin
    lane = lax.broadcasted_iota(jnp.int32, (1, 2 * LANES), 1)
    ones_col = jnp.where(jnp.bitwise_and(lane, LANES - 1) == MLA_V, 1.0, 0.0)
    cos2 = jnp.concatenate([cos, cos], axis=1)
    sin2 = jnp.concatenate([sin, sin], axis=1)
    kr2 = jnp.concatenate([kr_rot, kr_rot], axis=1)
    half = MLA_HEADS * LANES
    for c in range(MLA_HEADS // 2):
        cs = slice(2 * LANES * c, 2 * LANES * (c + 1))
        cs_b = slice(half + 2 * LANES * c, half + 2 * LANES * (c + 1))
        q = _dot(cq, wuq_ref[:, cs]) * cos2 + _dot(cq, wuq_ref[:, cs_b]) * sin2
        q_ref[:, cs] = (q * MLA_SCALE).astype(q_ref.dtype)
        k_ref[:, cs] = (_dot(ckv, wukv_ref[:, cs]) + kr2).astype(k_ref.dtype)
        v_ref[:, cs] = (_dot(ckv, wukv_ref[:, cs_b]) + ones_col).astype(v_ref.dtype)


def _mla_pre(x, g, sh, sc, win, qg, kvg, wuq, wukv, cos, sin, tiles_per_batch, n_batch):
    t, d = x.shape
    tm = ROW_TILE
    n = MLA_HEADS * LANES
    mod = _mod_spec(d, tiles_per_batch, n_batch)
    row = lambda i: (i, 0)
    return pl.pallas_call(
        _mla_pre_kernel,
        grid=(t // tm,),
        in_specs=[pl.BlockSpec((tm, d), row), _full((1, d)), mod, mod, _full(win.shape), _full(qg.shape),
                  _full(kvg.shape), _full(wuq.shape), _full(wukv.shape),
                  pl.BlockSpec((tm, LANES), row), pl.BlockSpec((tm, LANES), row)],
        out_specs=[pl.BlockSpec((tm, n), row)] * 3,
        out_shape=[jax.ShapeDtypeStruct((t, n), BF16)] * 3,
        compiler_params=_params(("parallel",)),
        name="mla_pre",
    )(x, g, sh, sc, win, qg, kvg, wuq, wukv, cos, sin)


def _flash_step(q, kb, vb, m, acc):
    s = _dot_nt(q, kb)
    m_new = jnp.maximum(m, jnp.max(s, axis=-1, keepdims=True))
    p = jnp.exp(s - m_new)
    return m_new, acc * jnp.exp(m - m_new) + _dot(p.astype(BF16), vb)


def _flash_kernel(tk, q_ref, k_ref, v_ref, kc_ref, vc_ref, o_ref):
    q = q_ref[...]
    tq = q.shape[0]

    def body(i, carry):
        off = pl.multiple_of(i * tk, tk)
        return _flash_step(q, k_ref[pl.ds(off, tk), :], v_ref[pl.ds(off, tk), :], *carry)

    carry = (jnp.full((tq, 1), NEG, F32), jnp.zeros((tq, LANES), F32))
    carry = lax.fori_loop(0, k_ref.shape[0] // tk, body, carry)
    _, acc = _flash_step(q, kc_ref[...], vc_ref[...], *carry)
    o_ref[...] = (acc / acc[:, MLA_V:MLA_V + 1]).astype(o_ref.dtype)


def _mla_attention(q, k, v, n_batch, seq, n_ctx):
    tq = min(FLASH_TQ, seq)
    tk = min(FLASH_TK, seq)
    nq = seq // tq
    ctx0 = n_batch * seq // n_ctx
    blk = lambda rows, f: pl.BlockSpec((rows, LANES), f)
    return pl.pallas_call(
        functools.partial(_flash_kernel, tk),
        grid=(n_batch, MLA_HEADS, nq),
        in_specs=[blk(tq, lambda b, h, i: (b * nq + i, h)),
                  blk(seq, lambda b, h, i: (b, h)), blk(seq, lambda b, h, i: (b, h)),
                  blk(n_ctx, lambda b, h, i: (ctx0 + b, h)), blk(n_ctx, lambda b, h, i: (ctx0 + b, h))],
        out_specs=blk(tq, lambda b, h, i: (b * nq + i, h)),
        out_shape=jax.ShapeDtypeStruct((n_batch * seq, MLA_HEADS * LANES), BF16),
        compiler_params=_params(("parallel", "parallel", "arbitrary")),
        name="mla_flash",
    )(q, k, v, k, v)


def _router_kernel(x_ref, g_ref, sh_ref, sc_ref, wr_ref, br_ref, tri_ref, h_ref, mi_ref, mf_ref, cnt_ref, run_ref):
    @pl.when(pl.program_id(0) == 0)
    def _init():
        run_ref[...] = jnp.zeros_like(run_ref)

    h = _norm_mod(x_ref[...], g_ref[...], sh_ref[...], sc_ref[...])
    h_ref[...] = h
    logits = jnp.dot(h, wr_ref[...], preferred_element_type=F32, precision=lax.Precision.HIGHEST) + br_ref[...]
    lane = lax.broadcasted_iota(jnp.int32, logits.shape, 1).astype(F32)
    m1 = jnp.max(logits, axis=-1, keepdims=True)
    i1 = jnp.min(jnp.where(logits == m1, lane, float(LANES)), axis=-1, keepdims=True)
    rest = jnp.where(lane == i1, NEG, logits)
    m2 = jnp.max(rest, axis=-1, keepdims=True)
    i2 = jnp.min(jnp.where(rest == m2, lane, float(LANES)), axis=-1, keepdims=True)
    e = jnp.exp(m2 - m1)
    w1 = 1.0 / (1.0 + e)
    w2 = e / (1.0 + e)
    hit1, hit2 = lane == i1, lane == i2
    onehot = jnp.where(jnp.logical_or(hit1, hit2), 1.0, 0.0)
    before = _dot(tri_ref[...], onehot.astype(BF16)) + run_ref[0:1, :]
    r1 = jnp.sum(jnp.where(hit1, before, 0.0), axis=-1, keepdims=True)
    r2 = jnp.sum(jnp.where(hit2, before, 0.0), axis=-1, keepdims=True)
    run_ref[...] = run_ref[...] + jnp.sum(onehot, axis=0, keepdims=True)
    meta = jnp.where(lane == 0.0, i1, jnp.where(lane == 1.0, i2, jnp.where(lane == 2.0, r1, jnp.where(lane == 3.0, r2, 0.0))))
    mi_ref[...] = meta.astype(jnp.int32)
    mf_ref[...] = jnp.where(lane == 0.0, w1, jnp.where(lane == 1.0, w2, 0.0))
    cnt_ref[...] = run_ref[...]


def _router(x, g, sh, sc, wr, br, tri, n_rows, tiles_per_batch, n_batch):
    d = x.shape[1]
    tm = ROW_TILE
    mod = _mod_spec(d, tiles_per_batch, n_batch)
    row = lambda i: (i, 0)
    return pl.pallas_call(
        _router_kernel,
        grid=(n_rows // tm,),
        in_specs=[pl.BlockSpec((tm, d), row), _full((1, d)), mod, mod, _full(wr.shape), _full(br.shape),
                  _full(tri.shape)],
        out_specs=[pl.BlockSpec((tm, d), row), pl.BlockSpec((tm, LANES), row), pl.BlockSpec((tm, LANES), row),
                   _full((8, LANES))],
        out_shape=[jax.ShapeDtypeStruct((n_rows, d), F32), jax.ShapeDtypeStruct((n_rows, LANES), jnp.int32),
                   jax.ShapeDtypeStruct((n_rows, LANES), F32), jax.ShapeDtypeStruct((8, LANES), F32)],
        scratch_shapes=[pltpu.VMEM((8, LANES), F32)],
        compiler_params=_params(("arbitrary",)),
        name="moe_router",
    )(x, g, sh, sc, wr, br, tri)


def _row_gather(src_hbm, idx_ref, base, n, buf, sem):
    def issue(j, carry):
        pltpu.make_async_copy(src_hbm.at[pl.ds(idx_ref[base + j], 1), :], buf.at[pl.ds(j, 1), :], sem).start()
        return carry

    lax.fori_loop(0, n, issue, 0)
    pltpu.make_async_copy(src_hbm.at[pl.ds(0, n), :], buf, sem).wait()


def _moe_kernel(te_ref, src_ref, nv_ref, h_hbm, wg_ref, wu_ref, wd_ref, y_ref, hbuf, sem):
    i, f = pl.program_id(0), pl.program_id(1)
    tm = hbuf.shape[0]
    valid = i < nv_ref[0]

    @pl.when(jnp.logical_and(valid, f == 0))
    def _gather():
        _row_gather(h_hbm, src_ref, i * tm, tm, hbuf, sem)

    @pl.when(valid)
    def _compute():
        hb = hbuf[...].astype(BF16)
        act = (_silu(_dot(hb, wg_ref[...])) * _dot(hb, wu_ref[...])).astype(BF16)
        part = _dot(act, wd_ref[...])

        @pl.when(f == 0)
        def _():
            y_ref[...] = part

        @pl.when(f > 0)
        def _():
            y_ref[...] = y_ref[...] + part

    @pl.when(jnp.logical_and(jnp.logical_not(valid), f == 0))
    def _unused_tile():
        y_ref[...] = jnp.zeros_like(y_ref)


def _moe_experts(tile_expert, src, n_valid, h, wg, wu, wd, n_tiles):
    d = h.shape[1]
    tm = MOE_TILE
    nf = MOE_FCHUNKS
    fc = wg.shape[2] // nf

    def f_idx(i, f, nv):
        return jnp.where(i < nv[0], f, nf - 1)

    grid_spec = pltpu.PrefetchScalarGridSpec(
        num_scalar_prefetch=3,
        grid=(n_tiles, nf),
        in_specs=[pl.BlockSpec(memory_space=pl.ANY),
                  pl.BlockSpec((None, d, fc), lambda i, f, te, src, nv: (te[i], 0, f_idx(i, f, nv))),
                  pl.BlockSpec((None, d, fc), lambda i, f, te, src, nv: (te[i], 0, f_idx(i, f, nv))),
                  pl.BlockSpec((None, fc, d), lambda i, f, te, src, nv: (te[i], f_idx(i, f, nv), 0))],
        out_specs=pl.BlockSpec((tm, d), lambda i, f, te, src, nv: (i, 0)),
        scratch_shapes=[pltpu.VMEM((tm, d), F32), pltpu.SemaphoreType.DMA(())],
    )
    return pl.pallas_call(
        _moe_kernel,
        grid_spec=grid_spec,
        out_shape=jax.ShapeDtypeStruct((n_tiles * tm, d), F32),
        compiler_params=_params(("arbitrary", "arbitrary")),
        name="moe_experts",
    )(tile_expert, src, n_valid, h, wg, wu, wd)


def _combine_kernel(d1_ref, d2_ref, x_ref, gate_ref, mf_ref, fg_ref, y_hbm, o_ref, buf1, buf2, sem1, sem2):
    i = pl.program_id(0)
    tm = buf1.shape[0]

    def issue(j, carry):
        pltpu.make_async_copy(y_hbm.at[pl.ds(d1_ref[i * tm + j], 1), :], buf1.at[pl.ds(j, 1), :], sem1).start()
        pltpu.make_async_copy(y_hbm.at[pl.ds(d2_ref[i * tm + j], 1), :], buf2.at[pl.ds(j, 1), :], sem2).start()
        return carry

    lax.fori_loop(0, tm, issue, 0)
    pltpu.make_async_copy(y_hbm.at[pl.ds(0, tm), :], buf1, sem1).wait()
    pltpu.make_async_copy(y_hbm.at[pl.ds(0, tm), :], buf2, sem2).wait()
    w = mf_ref[...]
    y = w[:, 0:1] * buf1[...] + w[:, 1:2] * buf2[...]
    o_ref[...] = _rms(x_ref[...] + gate_ref[...] * y, fg_ref[...])


def _moe_combine(d1, d2, x, gate, mf, fg, y, tiles_per_batch_512, n_batch):
    t, d = x.shape
    tm = COMBINE_TILE
    per = tiles_per_batch_512 * (ROW_TILE // tm)
    grid_spec = pltpu.PrefetchScalarGridSpec(
        num_scalar_prefetch=2,
        grid=(t // tm,),
        in_specs=[pl.BlockSpec((tm, d), lambda i, a, b: (i, 0)),
                  pl.BlockSpec((None, 1, d), lambda i, a, b: (jnp.minimum(i // per, n_batch), 0, 0)),
                  pl.BlockSpec((tm, LANES), lambda i, a, b: (i, 0)),
                  pl.BlockSpec((1, d), lambda i, a, b: (0, 0)),
                  pl.BlockSpec(memory_space=pl.ANY)],
        out_specs=pl.BlockSpec((tm, d), lambda i, a, b: (i, 0)),
        scratch_shapes=[pltpu.VMEM((tm, d), F32), pltpu.VMEM((tm, d), F32),
                        pltpu.SemaphoreType.DMA(()), pltpu.SemaphoreType.DMA(())],
    )
    return pl.pallas_call(
        _combine_kernel,
        grid_spec=grid_spec,
        out_shape=jax.ShapeDtypeStruct((t, d), F32),
        compiler_params=_params(("arbitrary",)),
        name="moe_combine",
    )(d1, d2, x, gate, mf, fg, y)


def kernel(x, c, ctx, c_ctx, ada_w, ada_b, norm_g, final_g, na_w_in, na_rpb, sg_w, sg_b, sg_norm_g, even_w_out,
           ffn_w_gate, ffn_w_up, ffn_w_down, mla_w_in, mla_q_norm_g, mla_kv_norm_g, mla_w_uq, mla_w_ukv, mla_w_out,
           moe_w_router, moe_b_router, moe_w_gate, moe_w_up, moe_w_down):
    n_batch, seq, d = x.shape
    n_ctx = ctx.shape[1]
    n_lat = n_batch * seq
    assert ada_w.shape[0] == 2 and seq % ROW_TILE == 0 and (n_batch * n_ctx) % ROW_TILE == 0
    assert n_batch + 1 <= 8 and seq % GRID_W == 0
    tpb = seq // ROW_TILE

    cond8 = jnp.concatenate([c, c_ctx[None, :], jnp.zeros((8 - n_batch - 1, d), F32)], axis=0)
    mods = _ada_mod(cond8, ada_w, ada_b)[:, :n_batch + 1]
    mod = lambda layer, k: mods[layer, :, k * d:(k + 1) * d].reshape(n_batch + 1, 1, d)
    xs = jnp.concatenate([x.reshape(n_lat, d), ctx.reshape(n_batch * n_ctx, d)], axis=0)

    w_in = na_w_in[0].astype(BF16)
    qkv, ug = _even_in(xs, norm_g[0, 0][None], mod(0, 0), mod(0, 1), w_in[:, :3 * A_WIDTH], w_in[:, 3 * A_WIDTH:],
                       tpb, n_batch)
    bias = _na_bias_table(na_rpb[0], seq // GRID_W)
    attn = _na_attention(qkv, bias, n_batch, seq, n_ctx)
    bmat = jnp.repeat(sg_b[0].T, SG_GROUP_DIM, axis=1)
    avg = jnp.asarray(np.kron(np.eye(SG_GROUPS), np.full((SG_GROUP_DIM, SG_GROUP_DIM), 1.0 / SG_GROUP_DIM)), BF16)
    gated = _spatial_gating(ug, sg_w[0].astype(BF16), bmat, sg_norm_g[0][None], avg)
    w_out = even_w_out[0].astype(BF16)
    xs = _proj_residual(xs, mod(0, 2), [attn, gated], [w_out[:A_WIDTH], w_out[A_WIDTH:]], xs.shape[0], tpb, n_batch)
    xs = _ffn(xs, norm_g[0, 1][None], mod(0, 3), mod(0, 4), mod(0, 5), ffn_w_gate[0].astype(BF16),
              ffn_w_up[0].astype(BF16), ffn_w_down[0].astype(BF16), tpb, n_batch)

    win, wuq, wukv, wo = _mla_weights(mla_w_in[0], mla_w_uq[0], mla_w_ukv[0], mla_w_out[0])
    cos, sin = _rope_tables(n_batch, seq, n_ctx)
    q, k, v = _mla_pre(xs, norm_g[1, 0][None], mod(1, 0), mod(1, 1), win, mla_q_norm_g[0][None],
                       mla_kv_norm_g[0][None], wuq, wukv, cos, sin, tpb, n_batch)
    o = _mla_attention(q, k, v, n_batch, seq, n_ctx)
    x1 = _proj_residual(xs, mod(1, 2), [o], [wo], n_lat, tpb, n_batch)

    wr = jnp.concatenate([moe_w_router[0], jnp.zeros((d, LANES - N_EXPERTS), F32)], axis=1)
    br = jnp.concatenate([moe_b_router[0], jnp.full((LANES - N_EXPERTS,), NEG, F32)])[None]
    tri = jnp.asarray(np.tril(np.ones((ROW_TILE, ROW_TILE), np.float32), -1), BF16)
    h, mi, mf, cnt = _router(x1, norm_g[1, 1][None], mod(1, 3), mod(1, 4), wr, br, tri, n_lat, tpb, n_batch)

    counts = cnt[0, :N_EXPERTS].astype(jnp.int32)
    tiles_e = (counts + MOE_TILE - 1) // MOE_TILE
    tile_end = jnp.cumsum(tiles_e)
    start = (tile_end - tiles_e) * MOE_TILE
    d1 = start[mi[:, 0]] + mi[:, 2]
    d2 = start[mi[:, 1]] + mi[:, 3]
    n_tiles = 2 * n_lat // MOE_TILE + N_EXPERTS
    tok = jnp.arange(n_lat, dtype=jnp.int32)
    src = jnp.zeros((n_tiles * MOE_TILE,), jnp.int32).at[d1].set(tok).at[d2].set(tok)
    n_valid = tile_end[-1:]
    last_tile = jnp.maximum(n_valid[0] - 1, 0)
    tile_ids = jnp.minimum(jnp.arange(n_tiles, dtype=jnp.int32), last_tile)
    tile_expert = jnp.minimum(jnp.sum((tile_ids[:, None] >= tile_end[None, :]).astype(jnp.int32), axis=1),
                              N_EXPERTS - 1)
    y = _moe_experts(tile_expert, src, n_valid.astype(jnp.int32), h, moe_w_gate[0].astype(BF16),
                     moe_w_up[0].astype(BF16), moe_w_down[0].astype(BF16), n_tiles)
    out = _moe_combine(d1.astype(jnp.int32), d2.astype(jnp.int32), x1, mod(1, 5), mf, final_g[None], y, tpb, n_batch)
    return out.reshape(n_batch, seq, d)
```

```python
import functools

import numpy as np
import jax
import jax.numpy as jnp
from jax import lax
from jax.experimental import pallas as pl
from jax.experimental.pallas import tpu as pltpu

F32 = jnp.float32
BF16 = jnp.bfloat16
EPS = 1e-6
NEG = -1e30

LANES = 128
VMEM_LIMIT_BYTES = 56 * 1024 * 1024

GRID_W = 64
NA_HEADS = 8
NA_HEAD_DIM = 64
NA_WIN_R = 8
NA_WIN_C = 16
SG_GROUPS = 8
SG_GROUP_DIM = 64
SG_CHUNK = 128
A_WIDTH = NA_HEADS * NA_HEAD_DIM
B_WIDTH = SG_GROUPS * SG_GROUP_DIM
MLA_HEADS = 16
MLA_NOPE = 64
MLA_ROPE = 32
MLA_V = 64
MLA_Q_RANK = 384
MLA_KV_RANK = 256
MLA_SCALE = (MLA_NOPE + MLA_ROPE) ** -0.5
ROPE_BASE = 10000.0
N_EXPERTS = 8

ROW_TILE = 512
NA_QROWS = 4
NA_QB = NA_QROWS * GRID_W
NA_KB = 3 * NA_QB
MLA_VROWS = 80
LOG2E = 1.4426950408889634
MOE_TILE = 512
MOE_FCHUNKS = 2
COMBINE_TILE = 256


def _params(sem):
    return pltpu.CompilerParams(dimension_semantics=sem, vmem_limit_bytes=VMEM_LIMIT_BYTES)


def _dot(a, b):
    return jnp.dot(a, b, preferred_element_type=F32)


def _dot_nt(a, b):
    return lax.dot_general(a, b, (((1,), (1,)), ((), ())), preferred_element_type=F32)


def _silu(x):
    return x / (1.0 + jnp.exp(-x))


def _gelu_tanh(x):
    return 0.5 * x * (1.0 + jnp.tanh(0.7978845608028654 * (x + 0.044715 * (x * x * x))))


def _rms(x, g):
    return x * lax.rsqrt(jnp.mean(x * x, axis=-1, keepdims=True) + EPS) * g


def _norm_mod(x, g, sh, sc):
    return _rms(x, g) * (1.0 + sc) + sh


def _full(shape):
    n = len(shape)
    return pl.BlockSpec(shape, lambda *_: (0,) * n)


def _ada_kernel(cond_ref, w_ref, b_ref, o_ref):
    c = cond_ref[...]
    o_ref[...] = jnp.dot(_silu(c), w_ref[...], preferred_element_type=F32,
                         precision=lax.Precision.HIGHEST) + b_ref[...]


def _ada_mod(cond8, ada_w, ada_b):
    depth, d, n = ada_w.shape
    tn = n // 4
    return pl.pallas_call(
        _ada_kernel,
        grid=(depth, n // tn),
        in_specs=[_full((8, d)),
                  pl.BlockSpec((None, d, tn), lambda l, j: (l, 0, j)),
                  pl.BlockSpec((None, 1, tn), lambda l, j: (l, 0, j))],
        out_specs=pl.BlockSpec((None, 8, tn), lambda l, j: (l, 0, j)),
        out_shape=jax.ShapeDtypeStruct((depth, 8, n), F32),
        compiler_params=_params(("parallel", "parallel")),
        name="ada_mod",
    )(cond8, ada_w, ada_b.reshape(depth, 1, n))


def _group_map(tiles_per_batch, n_batch):
    return lambda t: (jnp.minimum(t // tiles_per_batch, n_batch), 0, 0)


def _mod_spec(d, tiles_per_batch, n_batch):
    return pl.BlockSpec((None, 1, d), _group_map(tiles_per_batch, n_batch))


def _even_in_kernel(x_ref, g_ref, sh_ref, sc_ref, wqkv_ref, wug_ref, qkv_ref, ug_ref):
    h = _norm_mod(x_ref[...], g_ref[...], sh_ref[...], sc_ref[...]).astype(BF16)
    qkv_ref[...] = _dot(h, wqkv_ref[...]).astype(BF16)
    ug_ref[...] = _dot(h, wug_ref[...])


def _even_in(x, g, sh, sc, wqkv, wug, tiles_per_batch, n_batch):
    t, d = x.shape
    tm = ROW_TILE
    mod = _mod_spec(d, tiles_per_batch, n_batch)
    return pl.pallas_call(
        _even_in_kernel,
        grid=(t // tm,),
        in_specs=[pl.BlockSpec((tm, d), lambda i: (i, 0)), _full((1, d)), mod, mod,
                  _full(wqkv.shape), _full(wug.shape)],
        out_specs=[pl.BlockSpec((tm, wqkv.shape[1]), lambda i: (i, 0)),
                   pl.BlockSpec((tm, wug.shape[1]), lambda i: (i, 0))],
        out_shape=[jax.ShapeDtypeStruct((t, wqkv.shape[1]), BF16),
                   jax.ShapeDtypeStruct((t, wug.shape[1]), F32)],
        compiler_params=_params(("parallel",)),
        name="even_in",
    )(x, g, sh, sc, wqkv, wug)


def _na_bias_table(rpb, rows):
    nblk = rows // NA_QROWS
    n_heads = rpb.shape[0]
    win_rows = NA_KB // GRID_W
    qc = np.arange(GRID_W)
    col_start = np.clip(qc - NA_WIN_C // 2, 0, GRID_W - NA_WIN_C)
    col_ok = (qc[None, :] >= col_start[:, None]) & (qc[None, :] < col_start[:, None] + NA_WIN_C)
    col_j = qc[None, :] - qc[:, None] + NA_WIN_C - 1
    sel_c = ((col_j[None] == np.arange(2 * NA_WIN_C - 1)[:, None, None]) & col_ok[None]).astype(np.float32)
    toeplitz = jnp.einsum("hij,jqk->hiqk", rpb, sel_c, precision=lax.Precision.HIGHEST)
    sel_r = np.zeros((3, NA_QROWS, win_rows, 2 * NA_WIN_R - 1), np.float32)
    for kind, j in enumerate((0, 1, nblk - 1)):
        first_row = NA_QROWS * int(np.clip(j - 1, 0, nblk - 3))
        for a in range(NA_QROWS):
            r = NA_QROWS * j + a
            row_start = int(np.clip(r - NA_WIN_R // 2, 0, rows - NA_WIN_R))
            for b in range(win_rows):
                kr = first_row + b
                if row_start <= kr < row_start + NA_WIN_R:
                    sel_r[kind, a, b, kr - r + NA_WIN_R - 1] = 1.0
    tab = jnp.einsum("cabi,hiqk->chaqbk", sel_r, toeplitz, precision=lax.Precision.HIGHEST)
    row_ok = sel_r.sum(-1) > 0
    ok = jnp.asarray(row_ok)[:, None, :, None, :, None] & jnp.asarray(col_ok)[None, None, None, :, None, :]
    tab = jnp.where(ok, tab, NEG)
    return tab.reshape(3, n_heads, NA_QB, NA_KB)


def _na_kernel(q_ref, k0_ref, k1_ref, k2_ref, v0_ref, v1_ref, v2_ref, kc_ref, vc_ref, bias_ref, o_ref):
    jj = pl.program_id(2)
    lane = lax.broadcasted_iota(jnp.int32, (1, LANES), 1)
    first = lane < NA_HEAD_DIM
    q = q_ref[...]
    zero = jnp.zeros_like(q)
    scale = NA_HEAD_DIM ** -0.5

    def head_q(hh):
        return jnp.where(first if hh == 0 else jnp.logical_not(first), q, zero)

    @pl.when(jj == 0)
    def _ctx_queries():
        outs = []
        for hh in range(2):
            s = _dot_nt(head_q(hh), kc_ref[...]) * scale
            p = jnp.exp(s - jnp.max(s, axis=-1, keepdims=True))
            l = jnp.sum(p, axis=-1, keepdims=True)
            outs.append(_dot(p.astype(BF16), vc_ref[...]) / l)
        o_ref[...] = jnp.where(first, outs[0], outs[1]).astype(o_ref.dtype)

    @pl.when(jj > 0)
    def _latent_queries():
        k_refs = (k0_ref, k1_ref, k2_ref)
        v_refs = (v0_ref, v1_ref, v2_ref)
        outs = []
        for hh in range(2):
            qh = head_q(hh)
            s = [_dot_nt(qh, k_refs[i][...]) * scale + bias_ref[hh, :, NA_QB * i:NA_QB * (i + 1)]
                 for i in range(3)]
            s.append(_dot_nt(qh, kc_ref[...]) * scale)
            m = functools.reduce(jnp.maximum, [jnp.max(x, axis=-1, keepdims=True) for x in s])
            p = [jnp.exp(x - m) for x in s]
            l = functools.reduce(jnp.add, [jnp.sum(x, axis=-1, keepdims=True) for x in p])
            vals = [v_refs[i][...] for i in range(3)] + [vc_ref[...]]
            o = functools.reduce(jnp.add, [_dot(p[i].astype(BF16), vals[i]) for i in range(4)])
            outs.append(o / l)
        o_ref[...] = jnp.where(first, outs[0], outs[1]).astype(o_ref.dtype)


def _na_attention(qkv, bias, n_batch, seq, n_ctx):
    t = qkv.shape[0]
    assert n_ctx == NA_QB and seq % NA_QB == 0
    nblk = seq // NA_QB
    assert nblk >= 3
    hp = NA_HEADS // 2
    kcol, vcol = A_WIDTH // LANES, 2 * A_WIDTH // LANES
    ctx_blk = n_batch * nblk

    def q_map(b, h, j):
        return (jnp.where(j == 0, ctx_blk + b, b * nblk + j - 1), h)

    def win_map(i, col):
        return lambda b, h, j: (b * nblk + jnp.clip(j - 2, 0, nblk - 3) + i, col + h)

    def ctx_map(col):
        return lambda b, h, j: (ctx_blk + b, col + h)

    def bias_map(b, h, j):
        return (jnp.where(j <= 1, 0, jnp.where(j == nblk, 2, 1)), h, 0, 0)

    blk = (NA_QB, LANES)
    in_specs = ([pl.BlockSpec(blk, q_map)]
                + [pl.BlockSpec(blk, win_map(i, kcol)) for i in range(3)]
                + [pl.BlockSpec(blk, win_map(i, vcol)) for i in range(3)]
                + [pl.BlockSpec(blk, ctx_map(kcol)), pl.BlockSpec(blk, ctx_map(vcol)),
                   pl.BlockSpec((None, 2, NA_QB, NA_KB), bias_map)])
    return pl.pallas_call(
        _na_kernel,
        grid=(n_batch, hp, nblk + 1),
        in_specs=in_specs,
        out_specs=pl.BlockSpec(blk, q_map),
        out_shape=jax.ShapeDtypeStruct((t, A_WIDTH), BF16),
        compiler_params=_params(("parallel", "parallel", "arbitrary")),
        name="na_attention",
    )(*([qkv] * 9), bias)


def _seg_mean(x, avg):
    hi = x.astype(BF16)
    lo = (x - hi.astype(F32)).astype(BF16)
    return _dot(hi, avg) + _dot(lo, avg)


def _sg_kernel(ug_ref, ws_ref, bm_ref, ng_ref, avg_ref, o_ref):
    lane = lax.broadcasted_iota(jnp.int32, (1, LANES), 1)
    first = lane < SG_GROUP_DIM
    avg = avg_ref[...]
    for c in range(ROW_TILE // SG_CHUNK):
        rows = slice(SG_CHUNK * c, SG_CHUNK * (c + 1))
        u = _gelu_tanh(ug_ref[rows, 0:B_WIDTH])
        g = _gelu_tanh(ug_ref[rows, B_WIDTH:2 * B_WIDTH])
        d = g - _seg_mean(g, avg)
        var = _seg_mean(d * d, avg)
        gn = (d * lax.rsqrt(var + EPS) * ng_ref[...]).astype(BF16)
        parts = []
        for j in range(SG_GROUPS // 2):
            gj = gn[:, LANES * j:LANES * (j + 1)]
            parts.append(jnp.where(first, _dot(ws_ref[2 * j], gj), _dot(ws_ref[2 * j + 1], gj)))
        mixed = jnp.concatenate(parts, axis=1) + bm_ref[...]
        o_ref[rows, :] = (u * mixed).astype(o_ref.dtype)


def _spatial_gating(ug, ws, bmat, ng, avg):
    t = ug.shape[0]
    tm = ROW_TILE
    return pl.pallas_call(
        _sg_kernel,
        grid=(t // tm,),
        in_specs=[pl.BlockSpec((tm, 2 * B_WIDTH), lambda i: (i, 0)), _full(ws.shape), _full(bmat.shape),
                  _full(ng.shape), _full(avg.shape)],
        out_specs=pl.BlockSpec((tm, B_WIDTH), lambda i: (i, 0)),
        out_shape=jax.ShapeDtypeStruct((t, B_WIDTH), BF16),
        compiler_params=_params(("parallel",)),
        name="spatial_gating",
    )(ug, ws, bmat, ng, avg)


def _proj_res_kernel(n_in, x_ref, gate_ref, *refs):
    a_refs, w_refs, o_ref = refs[:n_in], refs[n_in:2 * n_in], refs[2 * n_in]
    y = functools.reduce(jnp.add, [_dot(a[...], w[...]) for a, w in zip(a_refs, w_refs)])
    o_ref[...] = x_ref[...] + gate_ref[...] * y


def _proj_residual(x, gate, acts, weights, n_rows, tiles_per_batch, n_batch):
    d = x.shape[1]
    tm = ROW_TILE
    row = lambda i: (i, 0)
    return pl.pallas_call(
        functools.partial(_proj_res_kernel, len(acts)),
        grid=(n_rows // tm,),
        in_specs=([pl.BlockSpec((tm, d), row), _mod_spec(d, tiles_per_batch, n_batch)]
                  + [pl.BlockSpec((tm, a.shape[1]), row) for a in acts]
                  + [_full(w.shape) for w in weights]),
        out_specs=pl.BlockSpec((tm, d), row),
        out_shape=jax.ShapeDtypeStruct((n_rows, d), F32),
        compiler_params=_params(("parallel",)),
        name="proj_residual",
    )(x, gate, *acts, *weights)


def _ffn_kernel(n_chunks, x_ref, g_ref, sh_ref, sc_ref, gate_ref, wg_ref, wu_ref, wd_ref, o_ref):
    x = x_ref[...]
    h = _norm_mod(x, g_ref[...], sh_ref[...], sc_ref[...]).astype(BF16)
    fc = wg_ref.shape[1] // n_chunks
    acc = None
    for c in range(n_chunks):
        cs = slice(fc * c, fc * (c + 1))
        act = (_silu(_dot(h, wg_ref[:, cs])) * _dot(h, wu_ref[:, cs])).astype(BF16)
        part = _dot(act, wd_ref[cs, :])
        acc = part if acc is None else acc + part
    o_ref[...] = x + gate_ref[...] * acc


def _ffn(x, g, sh, sc, gate, wg, wu, wd, tiles_per_batch, n_batch):
    t, d = x.shape
    tm = ROW_TILE
    f = wg.shape[1]
    n_chunks = 2 if f % (2 * LANES) == 0 else 1
    mod = _mod_spec(d, tiles_per_batch, n_batch)
    resident = lambda w: pl.BlockSpec(w.shape, lambda i: (0, 0), pipeline_mode=pl.Buffered(1))
    return pl.pallas_call(
        functools.partial(_ffn_kernel, n_chunks),
        grid=(t // tm,),
        in_specs=[pl.BlockSpec((tm, d), lambda i: (i, 0)), _full((1, d)), mod, mod, mod,
                  resident(wg), resident(wu), resident(wd)],
        out_specs=pl.BlockSpec((tm, d), lambda i: (i, 0)),
        out_shape=jax.ShapeDtypeStruct((t, d), F32),
        compiler_params=_params(("parallel",)),
        name="ffn",
    )(x, g, sh, sc, gate, wg, wu, wd)


def _rope_tables(n_batch, seq, n_ctx):
    t = np.arange(seq)
    pos = np.stack([t // GRID_W, t % GRID_W], axis=-1).astype(np.float32)
    n_freq = MLA_ROPE // 4
    inv = jnp.power(ROPE_BASE, -jnp.arange(n_freq, dtype=F32) / n_freq)
    ang = jnp.asarray(pos)[:, :, None] * inv
    cos, sin = jnp.cos(ang), jnp.sin(ang)
    cos_r = jnp.concatenate([cos, cos], axis=-1).reshape(seq, MLA_ROPE)
    sin_r = jnp.concatenate([-sin, sin], axis=-1).reshape(seq, MLA_ROPE)
    ones = jnp.ones((seq, MLA_NOPE), F32)
    pad = LANES - MLA_NOPE - MLA_ROPE
    cos_l = jnp.concatenate([ones, cos_r, jnp.ones((seq, pad), F32)], axis=-1)
    sin_l = jnp.concatenate([0 * ones, sin_r, jnp.zeros((seq, pad), F32)], axis=-1)
    cos_all = jnp.concatenate([jnp.tile(cos_l, (n_batch, 1)), jnp.ones((n_batch * n_ctx, LANES), F32)])
    sin_all = jnp.concatenate([jnp.tile(sin_l, (n_batch, 1)), jnp.zeros((n_batch * n_ctx, LANES), F32)])
    return cos_all, sin_all


def _swap_rope_cols(w_rope):
    q = MLA_ROPE // 4
    return jnp.concatenate([w_rope[..., q:2 * q], w_rope[..., 0:q], w_rope[..., 3 * q:4 * q],
                            w_rope[..., 2 * q:3 * q]], axis=-1)


def _mla_weights(w_in, w_uq, w_ukv):
    d = w_in.shape[0]
    pad = LANES - MLA_NOPE - MLA_ROPE
    w_kr = w_in[:, MLA_Q_RANK + MLA_KV_RANK:]
    lanes_kr = lambda w: jnp.concatenate([jnp.zeros((d, MLA_NOPE), F32), w, jnp.zeros((d, pad), F32)], axis=1)
    win = jnp.concatenate([w_in[:, :MLA_Q_RANK + MLA_KV_RANK], lanes_kr(w_kr), lanes_kr(_swap_rope_cols(w_kr))],
                          axis=1)
    uq = w_uq.reshape(MLA_Q_RANK, MLA_HEADS, MLA_NOPE + MLA_ROPE)
    zq = jnp.zeros((MLA_Q_RANK, MLA_HEADS, pad), F32)
    uq_a = jnp.concatenate([uq, zq], axis=-1).reshape(MLA_Q_RANK, MLA_HEADS * LANES)
    uq_b = jnp.concatenate([jnp.zeros((MLA_Q_RANK, MLA_HEADS, MLA_NOPE), F32),
                            _swap_rope_cols(uq[..., MLA_NOPE:]), zq], axis=-1).reshape(MLA_Q_RANK, MLA_HEADS * LANES)
    wuq_t = jnp.concatenate([uq_a, uq_b], axis=1).T
    ukv = w_ukv.reshape(MLA_KV_RANK, MLA_HEADS, MLA_NOPE + MLA_V)
    zk = jnp.zeros((MLA_KV_RANK, MLA_HEADS, LANES - MLA_NOPE), F32)
    uk = jnp.concatenate([ukv[..., :MLA_NOPE], zk], axis=-1).reshape(MLA_KV_RANK, MLA_HEADS * LANES)
    uv_t = jnp.transpose(ukv[..., MLA_NOPE:], (1, 2, 0))
    uv_t = jnp.concatenate([uv_t, jnp.zeros((MLA_HEADS, MLA_VROWS - MLA_V, MLA_KV_RANK), F32)], axis=1)
    return (win.astype(BF16), wuq_t.astype(BF16), uk.astype(BF16),
            uv_t.reshape(MLA_HEADS * MLA_VROWS, MLA_KV_RANK).astype(BF16))


def _mla_pre_kernel(x_ref, g_ref, sh_ref, sc_ref, win_ref, qg_ref, kvg_ref, wuqt_ref, wuk_ref, wuvt_ref,
                    cos_ref, sin_ref, cost_ref, sint_ref, qt_ref, k_ref, vt_ref):
    h = _norm_mod(x_ref[...], g_ref[...], sh_ref[...], sc_ref[...]).astype(BF16)
    p = _dot(h, win_ref[...])
    cq = _rms(p[:, :MLA_Q_RANK], qg_ref[...])
    ckv = _rms(p[:, MLA_Q_RANK:MLA_Q_RANK + MLA_KV_RANK], kvg_ref[...])
    cq_t = cq.T.astype(BF16)
    ckv_t = ckv.T.astype(BF16)
    ckv_b = ckv.astype(BF16)
    o_kr = MLA_Q_RANK + MLA_KV_RANK
    kr_rot = p[:, o_kr:o_kr + LANES] * cos_ref[...] + p[:, o_kr + LANES:o_kr + 2 * LANES] * sin_ref[...]
    kr2 = jnp.concatenate([kr_rot, kr_rot], axis=1)
    cos_t2 = jnp.concatenate([cost_ref[...], cost_ref[...]], axis=0)
    sin_t2 = jnp.concatenate([sint_ref[...], sint_ref[...]], axis=0)
    half = MLA_HEADS * LANES
    for c in range(MLA_HEADS // 2):
        rs = slice(2 * LANES * c, 2 * LANES * (c + 1))
        rs_b = slice(half + 2 * LANES * c, half + 2 * LANES * (c + 1))
        q_t = _dot(wuqt_ref[rs, :], cq_t) * cos_t2 + _dot(wuqt_ref[rs_b, :], cq_t) * sin_t2
        qt_ref[rs, :] = (q_t * (MLA_SCALE * LOG2E)).astype(qt_ref.dtype)
        k_ref[:, rs] = (_dot(ckv_b, wuk_ref[:, rs]) + kr2).astype(k_ref.dtype)
    group = 4 * MLA_VROWS
    row = lax.broadcasted_iota(jnp.int32, (group, 1), 0)
    ones_rows = functools.reduce(jnp.add, [jnp.where(row == MLA_VROWS * j + MLA_V, 1.0, 0.0) for j in range(4)])
    for c in range(MLA_HEADS // 4):
        rv = slice(group * c, group * (c + 1))
        vt_ref[rv, :] = (_dot(wuvt_ref[rv, :], ckv_t) + ones_rows).astype(vt_ref.dtype)


def _mla_pre(x, g, sh, sc, win, qg, kvg, wuq_t, wuk, wuv_t, cos, sin, tiles_per_batch, n_batch):
    t, d = x.shape
    tm = ROW_TILE
    n = MLA_HEADS * LANES
    nv = MLA_HEADS * MLA_VROWS
    mod = _mod_spec(d, tiles_per_batch, n_batch)
    row = lambda i: (i, 0)
    col = lambda i: (0, i)
    return pl.pallas_call(
        _mla_pre_kernel,
        grid=(t // tm,),
        in_specs=[pl.BlockSpec((tm, d), row), _full((1, d)), mod, mod, _full(win.shape), _full(qg.shape),
                  _full(kvg.shape), _full(wuq_t.shape), _full(wuk.shape), _full(wuv_t.shape),
                  pl.BlockSpec((tm, LANES), row), pl.BlockSpec((tm, LANES), row),
                  pl.BlockSpec((LANES, tm), col), pl.BlockSpec((LANES, tm), col)],
        out_specs=[pl.BlockSpec((None, n, tm), lambda i: (i, 0, 0)), pl.BlockSpec((tm, n), row),
                   pl.BlockSpec((None, nv, tm), lambda i: (i, 0, 0))],
        out_shape=[jax.ShapeDtypeStruct((t // tm, n, tm), BF16), jax.ShapeDtypeStruct((t, n), BF16),
                   jax.ShapeDtypeStruct((t // tm, nv, tm), BF16)],
        compiler_params=_params(("parallel",)),
        name="mla_pre",
    )(x, g, sh, sc, win, qg, kvg, wuq_t, wuk, wuv_t, cos, sin, cos.T, sin.T)


def _col_max(s_t):
    parts = [s_t[r:r + 64] for r in range(0, s_t.shape[0], 64)]
    while len(parts) > 1:
        parts = [jnp.maximum(parts[j], parts[j + 1]) for j in range(0, len(parts) - 1, 2)] + parts[len(parts) & ~1:]
    return jnp.max(parts[0], axis=0, keepdims=True)


def _flash_kernel(qt_ref, k_ref, vt_ref, kc_ref, vtc_ref, o_ref, sa0_ref, sa1_ref, sb0_ref, sb1_ref):
    tq = qt_ref.shape[1]
    n_blocks, _, tk = vt_ref.shape
    n_ctx = kc_ref.shape[0]
    heads = range(2)
    s_a, s_b = (sa0_ref, sa1_ref), (sb0_ref, sb1_ref)
    lanes = lambda hh: slice(LANES * hh, LANES * (hh + 1))
    vrows = lambda hh: slice(MLA_VROWS * hh, MLA_VROWS * (hh + 1))

    def scores(kb_of, bufs, rows):
        cmax = []
        for hh in heads:
            s_t = _dot(kb_of(hh), qt_ref[lanes(hh), :])
            bufs[hh][0:rows, :] = s_t
            cmax.append(_col_max(s_t))
        return tuple(cmax)

    def absorb(bufs, rows, cmax, vt_of, state):
        new = []
        for hh in heads:
            m, acc = state[hh]
            m_new = jnp.maximum(m, cmax[hh])
            p_t = jnp.exp2(bufs[hh][0:rows, :] - m_new).astype(BF16)
            new.append((m_new, acc * jnp.exp2(m - m_new) + _dot(vt_of(hh), p_t)))
        return tuple(new)

    def lat_k(blk):
        off = blk * tk if isinstance(blk, int) else pl.multiple_of(blk * tk, tk)
        return lambda hh: k_ref[pl.ds(off, tk), lanes(hh)]

    lat_v = lambda blk: (lambda hh: vt_ref[blk, vrows(hh), :])
    ctx_k = lambda hh: kc_ref[:, lanes(hh)]
    ctx_v = lambda hh: vtc_ref[vrows(hh), :]

    def pair(blk, cmax_a, state, last):
        cmax_b = scores(lat_k(blk + 1), s_b, tk)
        state = absorb(s_a, tk, cmax_a, lat_v(blk), state)
        cmax_a = scores(ctx_k, s_a, n_ctx) if last else scores(lat_k(blk + 2), s_a, tk)
        return cmax_a, absorb(s_b, tk, cmax_b, lat_v(blk + 1), state)

    init = tuple((jnp.full((1, tq), NEG, F32), jnp.zeros((MLA_VROWS, tq), F32)) for _ in heads)
    carry = (scores(lat_k(0), s_a, tk), init)
    carry = lax.fori_loop(0, n_blocks // 2 - 1, lambda j, c: pair(2 * j, *c, last=False), carry)
    cmax_a, state = pair(n_blocks - 2, *carry, last=True)
    state = absorb(s_a, n_ctx, cmax_a, ctx_v, state)
    outs = [acc[:MLA_V] / acc[MLA_V:MLA_V + 1] for _, acc in state]
    o_ref[...] = jnp.concatenate(outs, axis=0).T.astype(o_ref.dtype)


def _mla_attention(qt, k, vt, vtc, n_batch, seq, n_ctx):
    tq = ROW_TILE
    nq = seq // tq
    ctx0 = n_batch * seq // n_ctx
    return pl.pallas_call(
        _flash_kernel,
        grid=(n_batch, MLA_HEADS // 2, nq),
        in_specs=[pl.BlockSpec((None, 2 * LANES, tq), lambda b, h, i: (b * nq + i, h, 0)),
                  pl.BlockSpec((seq, 2 * LANES), lambda b, h, i: (b, h)),
                  pl.BlockSpec((nq, 2 * MLA_VROWS, tq), lambda b, h, i: (b, h, 0)),
                  pl.BlockSpec((n_ctx, 2 * LANES), lambda b, h, i: (ctx0 + b, h)),
                  pl.BlockSpec((None, 2 * MLA_VROWS, n_ctx), lambda b, h, i: (b, h, 0))],
        out_specs=pl.BlockSpec((tq, LANES), lambda b, h, i: (b * nq + i, h)),
        out_shape=jax.ShapeDtypeStruct((n_batch * seq, MLA_HEADS * MLA_V), BF16),
        scratch_shapes=[pltpu.VMEM((tq, tq), F32)] * 4,
        compiler_params=_params(("parallel", "parallel", "arbitrary")),
        name="mla_flash",
    )(qt, k, vt, k, vtc)


def _router_kernel(x_ref, g_ref, sh_ref, sc_ref, wr_ref, br_ref, tri_ref, h_ref, mi_ref, mf_ref, cnt_ref, run_ref):
    @pl.when(pl.program_id(0) == 0)
    def _init():
        run_ref[...] = jnp.zeros_like(run_ref)

    h = _norm_mod(x_ref[...], g_ref[...], sh_ref[...], sc_ref[...])
    h_ref[...] = h
    logits = jnp.dot(h, wr_ref[...], preferred_element_type=F32, precision=lax.Precision.HIGHEST) + br_ref[...]
    lane = lax.broadcasted_iota(jnp.int32, logits.shape, 1).astype(F32)
    m1 = jnp.max(logits, axis=-1, keepdims=True)
    i1 = jnp.min(jnp.where(logits == m1, lane, float(LANES)), axis=-1, keepdims=True)
    rest = jnp.where(lane == i1, NEG, logits)
    m2 = jnp.max(rest, axis=-1, keepdims=True)
    i2 = jnp.min(jnp.where(rest == m2, lane, float(LANES)), axis=-1, keepdims=True)
    e = jnp.exp(m2 - m1)
    w1 = 1.0 / (1.0 + e)
    w2 = e / (1.0 + e)
    hit1, hit2 = lane == i1, lane == i2
    onehot = jnp.where(jnp.logical_or(hit1, hit2), 1.0, 0.0)
    before = _dot(tri_ref[...], onehot.astype(BF16)) + run_ref[0:1, :]
    r1 = jnp.sum(jnp.where(hit1, before, 0.0), axis=-1, keepdims=True)
    r2 = jnp.sum(jnp.where(hit2, before, 0.0), axis=-1, keepdims=True)
    run_ref[...] = run_ref[...] + jnp.sum(onehot, axis=0, keepdims=True)
    meta = jnp.where(lane == 0.0, i1, jnp.where(lane == 1.0, i2, jnp.where(lane == 2.0, r1, jnp.where(lane == 3.0, r2, 0.0))))
    mi_ref[...] = meta.astype(jnp.int32)
    mf_ref[...] = jnp.where(lane == 0.0, w1, jnp.where(lane == 1.0, w2, 0.0))
    cnt_ref[...] = run_ref[...]


def _router(x, g, sh, sc, wr, br, tri, n_rows, tiles_per_batch, n_batch):
    d = x.shape[1]
    tm = ROW_TILE
    mod = _mod_spec(d, tiles_per_batch, n_batch)
    row = lambda i: (i, 0)
    return pl.pallas_call(
        _router_kernel,
        grid=(n_rows // tm,),
        in_specs=[pl.BlockSpec((tm, d), row), _full((1, d)), mod, mod, _full(wr.shape), _full(br.shape),
                  _full(tri.shape)],
        out_specs=[pl.BlockSpec((tm, d), row), pl.BlockSpec((tm, LANES), row), pl.BlockSpec((tm, LANES), row),
                   _full((8, LANES))],
        out_shape=[jax.ShapeDtypeStruct((n_rows, d), F32), jax.ShapeDtypeStruct((n_rows, LANES), jnp.int32),
                   jax.ShapeDtypeStruct((n_rows, LANES), F32), jax.ShapeDtypeStruct((8, LANES), F32)],
        scratch_shapes=[pltpu.VMEM((8, LANES), F32)],
        compiler_params=_params(("arbitrary",)),
        name="moe_router",
    )(x, g, sh, sc, wr, br, tri)


def _moe_kernel(te_ref, src_ref, dst_ref, rows_ref, nv_ref, h_hbm, wg_ref, wu_ref, wd_ref, y_hbm, hbuf, ybuf,
                sem_g, sem_s):
    i, f = pl.program_id(0), pl.program_id(1)
    _, tm, _ = hbuf.shape
    half = tm // MOE_FCHUNKS
    nv = nv_ref[0]
    valid = i < nv
    slot = lax.rem(i, 2)
    other = 1 - slot

    def gather_row(idx, s, r):
        return pltpu.make_async_copy(h_hbm.at[pl.ds(idx, 1), :], hbuf.at[s, pl.ds(r, 1), :], sem_g.at[s])

    def scatter_row(idx, s, r):
        return pltpu.make_async_copy(ybuf.at[s, pl.ds(r, 1), :], y_hbm.at[pl.ds(idx, 1), :], sem_s.at[s])

    def wait_gather(s):
        pltpu.make_async_copy(h_hbm.at[pl.ds(0, tm), :], hbuf.at[s], sem_g.at[s]).wait()

    def wait_scatter(s, n):
        n8 = pl.multiple_of((n // 8) * 8, 8)

        @pl.when(n8 > 0)
        def _():
            pltpu.make_async_copy(ybuf.at[s, pl.ds(0, n8), :], y_hbm.at[pl.ds(0, n8), :], sem_s.at[s]).wait()

        def one_row(j, carry):
            scatter_row(0, s, 0).wait()
            return carry

        lax.fori_loop(0, n - n8, one_row, 0)

    @pl.when(jnp.logical_and(i == 0, f == 0))
    def _prologue():
        ybuf[...] = jnp.zeros_like(ybuf)

        def issue(j, carry):
            gather_row(src_ref[j], 0, j).start()
            return carry

        lax.fori_loop(0, tm, issue, 0)

    @pl.when(jnp.logical_and(f == 0, i <= nv))
    def _rows_ready():
        wait_gather(slot)

    @pl.when(jnp.logical_and(f == 0, jnp.logical_and(i >= 1, i <= nv)))
    def _slot_free():
        wait_scatter(slot, rows_ref[jnp.maximum(i - 1, 0)])

    @pl.when(valid)
    def _compute():
        hb = hbuf[slot].astype(BF16)
        r0 = f * half
        n_prev = rows_ref[i]
        for j in range(half):
            gather_row(src_ref[(i + 1) * tm + r0 + j], other, r0 + j).start()

            @pl.when(r0 + j < n_prev)
            def _():
                scatter_row(dst_ref[i * tm + r0 + j], other, r0 + j).start()
        act = (_silu(_dot(hb, wg_ref[...])) * _dot(hb, wu_ref[...])).astype(BF16)
        part = _dot(act, wd_ref[...])
        ybuf[slot] = jnp.where(f == 0, part, ybuf[slot] + part)

    @pl.when(jnp.logical_and(i == nv, f == 0))
    def _flush():
        n_last = rows_ref[i]

        def issue(j, carry):
            scatter_row(dst_ref[i * tm + j], other, j).start()
            return carry

        lax.fori_loop(0, n_last, issue, 0)
        wait_scatter(other, n_last)


def _moe_experts(tile_expert, src, dst, rows, n_valid, h, wg, wu, wd, n_tiles, y_rows):
    d = h.shape[1]
    tm = MOE_TILE
    nf = MOE_FCHUNKS
    fc = wg.shape[2] // nf

    def f_idx(i, f, nv):
        return jnp.where(i < nv[0], f, nf - 1)

    grid_spec = pltpu.PrefetchScalarGridSpec(
        num_scalar_prefetch=5,
        grid=(n_tiles + 1, nf),
        in_specs=[pl.BlockSpec(memory_space=pl.ANY),
                  pl.BlockSpec((None, d, fc), lambda i, f, te, src, dst, rows, nv: (te[i], 0, f_idx(i, f, nv))),
                  pl.BlockSpec((None, d, fc), lambda i, f, te, src, dst, rows, nv: (te[i], 0, f_idx(i, f, nv))),
                  pl.BlockSpec((None, fc, d), lambda i, f, te, src, dst, rows, nv: (te[i], f_idx(i, f, nv), 0))],
        out_specs=pl.BlockSpec(memory_space=pl.ANY),
        scratch_shapes=[pltpu.VMEM((2, tm, d), F32), pltpu.VMEM((2, tm, d), F32),
                        pltpu.SemaphoreType.DMA((2,)), pltpu.SemaphoreType.DMA((2,))],
    )
    return pl.pallas_call(
        _moe_kernel,
        grid_spec=grid_spec,
        out_shape=jax.ShapeDtypeStruct((y_rows, d), F32),
        compiler_params=_params(("arbitrary", "arbitrary")),
        name="moe_experts",
    )(tile_expert, src, dst, rows, n_valid, h, wg, wu, wd)


def _combine_kernel(x_ref, gate_ref, mf_ref, fg_ref, y1_ref, y2_ref, o_ref):
    w = mf_ref[...]
    y = w[:, 0:1] * y1_ref[...] + w[:, 1:2] * y2_ref[...]
    o_ref[...] = _rms(x_ref[...] + gate_ref[...] * y, fg_ref[...])


def _moe_combine(x, gate, mf, fg, y, tiles_per_batch, n_batch):
    t, d = x.shape
    tm = ROW_TILE
    row = lambda i: (i, 0)
    return pl.pallas_call(
        _combine_kernel,
        grid=(t // tm,),
        in_specs=[pl.BlockSpec((tm, d), row), _mod_spec(d, tiles_per_batch, n_batch), pl.BlockSpec((tm, LANES), row),
                  _full((1, d)), pl.BlockSpec((tm, d), row), pl.BlockSpec((tm, d), lambda i: (t // tm + i, 0))],
        out_specs=pl.BlockSpec((tm, d), row),
        out_shape=jax.ShapeDtypeStruct((t, d), F32),
        compiler_params=_params(("parallel",)),
        name="moe_combine",
    )(x, gate, mf, fg, y, y)


def kernel(x, c, ctx, c_ctx, ada_w, ada_b, norm_g, final_g, na_w_in, na_rpb, sg_w, sg_b, sg_norm_g, even_w_out,
           ffn_w_gate, ffn_w_up, ffn_w_down, mla_w_in, mla_q_norm_g, mla_kv_norm_g, mla_w_uq, mla_w_ukv, mla_w_out,
           moe_w_router, moe_b_router, moe_w_gate, moe_w_up, moe_w_down):
    n_batch, seq, d = x.shape
    n_ctx = ctx.shape[1]
    n_lat = n_batch * seq
    assert ada_w.shape[0] == 2 and seq % ROW_TILE == 0 and n_batch * n_ctx == ROW_TILE
    assert n_batch + 1 <= 8 and seq % GRID_W == 0
    tpb = seq // ROW_TILE

    cond8 = jnp.concatenate([c, c_ctx[None, :], jnp.zeros((8 - n_batch - 1, d), F32)], axis=0)
    mods = _ada_mod(cond8, ada_w, ada_b)[:, :n_batch + 1]
    mod = lambda layer, k: mods[layer, :, k * d:(k + 1) * d].reshape(n_batch + 1, 1, d)
    xs = jnp.concatenate([x.reshape(n_lat, d), ctx.reshape(n_batch * n_ctx, d)], axis=0)

    w_in = na_w_in[0].astype(BF16)
    qkv, ug = _even_in(xs, norm_g[0, 0][None], mod(0, 0), mod(0, 1), w_in[:, :3 * A_WIDTH], w_in[:, 3 * A_WIDTH:],
                       tpb, n_batch)
    bias = _na_bias_table(na_rpb[0], seq // GRID_W)
    attn = _na_attention(qkv, bias, n_batch, seq, n_ctx)
    bmat = jnp.repeat(sg_b[0].T, SG_GROUP_DIM, axis=1)
    avg = jnp.asarray(np.kron(np.eye(SG_GROUPS), np.full((SG_GROUP_DIM, SG_GROUP_DIM), 1.0 / SG_GROUP_DIM)), BF16)
    gated = _spatial_gating(ug, sg_w[0].astype(BF16), bmat, sg_norm_g[0][None], avg)
    w_out = even_w_out[0].astype(BF16)
    xs = _proj_residual(xs, mod(0, 2), [attn, gated], [w_out[:A_WIDTH], w_out[A_WIDTH:]], xs.shape[0], tpb, n_batch)
    xs = _ffn(xs, norm_g[0, 1][None], mod(0, 3), mod(0, 4), mod(0, 5), ffn_w_gate[0].astype(BF16),
              ffn_w_up[0].astype(BF16), ffn_w_down[0].astype(BF16), tpb, n_batch)

    win, wuq_t, wuk, wuv_t = _mla_weights(mla_w_in[0], mla_w_uq[0], mla_w_ukv[0])
    cos, sin = _rope_tables(n_batch, seq, n_ctx)
    qt, k, vt = _mla_pre(xs, norm_g[1, 0][None], mod(1, 0), mod(1, 1), win, mla_q_norm_g[0][None],
                         mla_kv_norm_g[0][None], wuq_t, wuk, wuv_t, cos, sin, tpb, n_batch)
    vtc = jnp.transpose(vt[n_batch * tpb].reshape(-1, n_batch, n_ctx), (1, 0, 2))
    o = _mla_attention(qt, k, vt, vtc, n_batch, seq, n_ctx)
    x1 = _proj_residual(xs, mod(1, 2), [o], [mla_w_out[0].astype(BF16)], n_lat, tpb, n_batch)

    wr = jnp.concatenate([moe_w_router[0], jnp.zeros((d, LANES - N_EXPERTS), F32)], axis=1)
    br = jnp.concatenate([moe_b_router[0], jnp.full((LANES - N_EXPERTS,), NEG, F32)])[None]
    tri = jnp.asarray(np.tril(np.ones((ROW_TILE, ROW_TILE), np.float32), -1), BF16)
    h, mi, mf, cnt = _router(x1, norm_g[1, 1][None], mod(1, 3), mod(1, 4), wr, br, tri, n_lat, tpb, n_batch)

    counts = cnt[0, :N_EXPERTS].astype(jnp.int32)
    tiles_e = (counts + MOE_TILE - 1) // MOE_TILE
    tile_end = jnp.cumsum(tiles_e)
    start = (tile_end - tiles_e) * MOE_TILE
    slot12 = jnp.concatenate([start[mi[:, 0]] + mi[:, 2], start[mi[:, 1]] + mi[:, 3]])
    n_tiles = 2 * n_lat // MOE_TILE + N_EXPERTS
    n_slots = n_tiles * MOE_TILE
    dst = jnp.zeros((n_slots,), jnp.int32).at[slot12].set(jnp.arange(2 * n_lat, dtype=jnp.int32))
    spare_tile = jnp.zeros((MOE_TILE,), jnp.int32)
    src = jnp.concatenate([dst % n_lat, spare_tile])
    dst = jnp.concatenate([spare_tile, dst])
    n_valid = tile_end[-1:]
    last_tile = jnp.maximum(n_valid[0] - 1, 0)
    all_tiles = jnp.arange(n_tiles + 1, dtype=jnp.int32)
    tile_ids = jnp.minimum(all_tiles, last_tile)
    tile_expert = jnp.minimum(jnp.sum((tile_ids[:, None] >= tile_end[None, :]).astype(jnp.int32), axis=1),
                              N_EXPERTS - 1)
    filled = counts[tile_expert] - (all_tiles - (tile_end - tiles_e)[tile_expert]) * MOE_TILE
    rows = jnp.where(all_tiles < n_valid[0], jnp.clip(filled, 0, MOE_TILE), 0)
    rows = jnp.concatenate([jnp.zeros((1,), jnp.int32), rows]).astype(jnp.int32)
    y = _moe_experts(tile_expert, src, dst, rows, n_valid.astype(jnp.int32), h, moe_w_gate[0].astype(BF16),
                     moe_w_up[0].astype(BF16), moe_w_down[0].astype(BF16), n_tiles, 2 * n_lat)
    out = _moe_combine(x1, mod(1, 5), mf, final_g[None], y, tpb, n_batch)
    return out.reshape(n_batch, seq, d)
```

```python
import functools

import numpy as np
import jax
import jax.numpy as jnp
from jax import lax
from jax.experimental import pallas as pl
from jax.experimental.pallas import tpu as pltpu

F32 = jnp.float32
BF16 = jnp.bfloat16
EPS = 1e-6
NEG = -1e30

LANES = 128
VMEM_LIMIT_BYTES = 56 * 1024 * 1024

GRID_W = 64
NA_HEADS = 8
NA_HEAD_DIM = 64
NA_WIN_R = 8
NA_WIN_C = 16
SG_GROUPS = 8
SG_GROUP_DIM = 64
SG_CHUNK = 128
A_WIDTH = NA_HEADS * NA_HEAD_DIM
B_WIDTH = SG_GROUPS * SG_GROUP_DIM
MLA_HEADS = 16
MLA_NOPE = 64
MLA_ROPE = 32
MLA_V = 64
MLA_Q_RANK = 384
MLA_KV_RANK = 256
MLA_SCALE = (MLA_NOPE + MLA_ROPE) ** -0.5
ROPE_BASE = 10000.0
N_EXPERTS = 8

ROW_TILE = 512
NA_QROWS = 4
NA_QB = NA_QROWS * GRID_W
NA_KB = 3 * NA_QB
MLA_VROWS = 80
LOG2E = 1.4426950408889634
MOE_TILE = 512
MOE_FCHUNKS = 2


def _params(sem):
    return pltpu.CompilerParams(dimension_semantics=sem, vmem_limit_bytes=VMEM_LIMIT_BYTES)


def _dot(a, b):
    return jnp.dot(a, b, preferred_element_type=F32)


def _dot_nt(a, b):
    return lax.dot_general(a, b, (((1,), (1,)), ((), ())), preferred_element_type=F32)


def _silu(x):
    return x / (1.0 + jnp.exp(-x))


def _gelu_tanh(x):
    return 0.5 * x * (1.0 + jnp.tanh(0.7978845608028654 * (x + 0.044715 * (x * x * x))))


def _rms(x, g):
    return x * lax.rsqrt(jnp.mean(x * x, axis=-1, keepdims=True) + EPS) * g


def _norm_mod(x, g, sh, sc):
    return _rms(x, g) * (1.0 + sc) + sh


def _full(shape):
    n = len(shape)
    return pl.BlockSpec(shape, lambda *_: (0,) * n)


def _ada_kernel(cond_ref, w_ref, b_ref, o_ref):
    c = cond_ref[...]
    o_ref[...] = jnp.dot(_silu(c), w_ref[...], preferred_element_type=F32,
                         precision=lax.Precision.HIGHEST) + b_ref[...]


def _ada_mod(cond8, ada_w, ada_b):
    depth, d, n = ada_w.shape
    tn = n // 4
    return pl.pallas_call(
        _ada_kernel,
        grid=(depth, n // tn),
        in_specs=[_full((8, d)),
                  pl.BlockSpec((None, d, tn), lambda l, j: (l, 0, j)),
                  pl.BlockSpec((None, 1, tn), lambda l, j: (l, 0, j))],
        out_specs=pl.BlockSpec((None, 8, tn), lambda l, j: (l, 0, j)),
        out_shape=jax.ShapeDtypeStruct((depth, 8, n), F32),
        compiler_params=_params(("parallel", "parallel")),
        name="ada_mod",
    )(cond8, ada_w, ada_b.reshape(depth, 1, n))


def _group_map(tiles_per_batch, n_batch):
    return lambda t: (jnp.minimum(t // tiles_per_batch, n_batch), 0, 0)


def _mod_spec(d, tiles_per_batch, n_batch):
    return pl.BlockSpec((None, 1, d), _group_map(tiles_per_batch, n_batch))


def _stream_specs(x_lat, x_ctx):
    tm, d = ROW_TILE, x_lat.shape[1]
    assert x_ctx.shape[0] == tm
    last = x_lat.shape[0] // tm - 1
    return [pl.BlockSpec((tm, d), lambda i: (jnp.minimum(i, last), 0)), pl.BlockSpec((tm, d), lambda i: (0, 0))]


def _stream_tile(xl_ref, xc_ref, n_lat_tiles):
    return jnp.where(pl.program_id(0) < n_lat_tiles, xl_ref[...], xc_ref[...])


def _even_in_kernel(n_lat_tiles, xl_ref, xc_ref, g_ref, sh_ref, sc_ref, wqkv_ref, wug_ref, qkv_ref, ug_ref):
    x = _stream_tile(xl_ref, xc_ref, n_lat_tiles)
    h = _norm_mod(x, g_ref[...], sh_ref[...], sc_ref[...]).astype(BF16)
    qkv_ref[...] = _dot(h, wqkv_ref[...]).astype(BF16)
    ug_ref[...] = _dot(h, wug_ref[...])


def _even_in(x_lat, x_ctx, g, sh, sc, wqkv, wug, tiles_per_batch, n_batch):
    d = x_lat.shape[1]
    t = x_lat.shape[0] + x_ctx.shape[0]
    tm = ROW_TILE
    mod = _mod_spec(d, tiles_per_batch, n_batch)
    return pl.pallas_call(
        functools.partial(_even_in_kernel, x_lat.shape[0] // tm),
        grid=(t // tm,),
        in_specs=_stream_specs(x_lat, x_ctx) + [_full((1, d)), mod, mod,
                  _full(wqkv.shape), _full(wug.shape)],
        out_specs=[pl.BlockSpec((tm, wqkv.shape[1]), lambda i: (i, 0)),
                   pl.BlockSpec((tm, wug.shape[1]), lambda i: (i, 0))],
        out_shape=[jax.ShapeDtypeStruct((t, wqkv.shape[1]), BF16),
                   jax.ShapeDtypeStruct((t, wug.shape[1]), F32)],
        compiler_params=_params(("parallel",)),
        name="even_in",
    )(x_lat, x_ctx, g, sh, sc, wqkv, wug)


def _na_bias_table(rpb, rows):
    nblk = rows // NA_QROWS
    n_heads = rpb.shape[0]
    win_rows = NA_KB // GRID_W
    qc = np.arange(GRID_W)
    col_start = np.clip(qc - NA_WIN_C // 2, 0, GRID_W - NA_WIN_C)
    col_ok = (qc[None, :] >= col_start[:, None]) & (qc[None, :] < col_start[:, None] + NA_WIN_C)
    col_j = qc[None, :] - qc[:, None] + NA_WIN_C - 1
    sel_c = ((col_j[None] == np.arange(2 * NA_WIN_C - 1)[:, None, None]) & col_ok[None]).astype(np.float32)
    toeplitz = jnp.einsum("hij,jqk->hiqk", rpb, sel_c, precision=lax.Precision.HIGHEST)
    toeplitz = jnp.where(col_ok[None, None], toeplitz, NEG)
    n_rel = 2 * NA_WIN_R - 1
    toeplitz = jnp.concatenate([toeplitz, jnp.full((n_heads, 1, GRID_W, GRID_W), NEG, F32)], axis=1)
    rel = np.full((3, NA_QROWS, win_rows), n_rel, np.int32)
    for kind, j in enumerate((0, 1, nblk - 1)):
        first_row = NA_QROWS * int(np.clip(j - 1, 0, nblk - 3))
        for a in range(NA_QROWS):
            r = NA_QROWS * j + a
            row_start = int(np.clip(r - NA_WIN_R // 2, 0, rows - NA_WIN_R))
            for b in range(win_rows):
                kr = first_row + b
                if row_start <= kr < row_start + NA_WIN_R:
                    rel[kind, a, b] = kr - r + NA_WIN_R - 1
    grid_spec = pltpu.PrefetchScalarGridSpec(
        num_scalar_prefetch=1,
        grid=(3, n_heads),
        in_specs=[pl.BlockSpec((None, n_rel + 1, GRID_W, GRID_W), lambda c, h, rel: (h, 0, 0, 0))],
        out_specs=pl.BlockSpec((None, None, NA_QB, NA_KB), lambda c, h, rel: (c, h, 0, 0)),
    )
    return pl.pallas_call(
        _na_bias_kernel,
        grid_spec=grid_spec,
        out_shape=jax.ShapeDtypeStruct((3, n_heads, NA_QB, NA_KB), F32),
        compiler_params=_params(("parallel", "parallel")),
        name="na_bias",
    )(jnp.asarray(rel.reshape(-1)), toeplitz)


def _na_bias_kernel(rel_ref, t_ref, o_ref):
    kind = pl.program_id(0)
    win_rows = NA_KB // GRID_W
    for a in range(NA_QROWS):
        for bp in range(win_rows // 2):
            base = (kind * NA_QROWS + a) * win_rows + 2 * bp
            pair = jnp.concatenate([t_ref[rel_ref[base]], t_ref[rel_ref[base + 1]]], axis=1)
            o_ref[GRID_W * a:GRID_W * (a + 1), 2 * GRID_W * bp:2 * GRID_W * (bp + 1)] = pair


def _na_kernel(q_ref, k0_ref, k1_ref, k2_ref, v0_ref, v1_ref, v2_ref, kc_ref, vc_ref, bias_ref, o_ref):
    jj = pl.program_id(2)
    lane = lax.broadcasted_iota(jnp.int32, (1, LANES), 1)
    first = lane < NA_HEAD_DIM
    q = q_ref[...]
    zero = jnp.zeros_like(q)
    scale = NA_HEAD_DIM ** -0.5

    def head_q(hh):
        return jnp.where(first if hh == 0 else jnp.logical_not(first), q, zero)

    @pl.when(jj == 0)
    def _ctx_queries():
        outs = []
        for hh in range(2):
            s = _dot_nt(head_q(hh), kc_ref[...]) * scale
            p = jnp.exp(s - jnp.max(s, axis=-1, keepdims=True))
            l = jnp.sum(p, axis=-1, keepdims=True)
            outs.append(_dot(p.astype(BF16), vc_ref[...]) / l)
        o_ref[...] = jnp.where(first, outs[0], outs[1]).astype(o_ref.dtype)

    @pl.when(jj > 0)
    def _latent_queries():
        k_refs = (k0_ref, k1_ref, k2_ref)
        v_refs = (v0_ref, v1_ref, v2_ref)
        outs = []
        for hh in range(2):
            qh = head_q(hh)
            s = [_dot_nt(qh, k_refs[i][...]) * scale + bias_ref[hh, :, NA_QB * i:NA_QB * (i + 1)]
                 for i in range(3)]
            s.append(_dot_nt(qh, kc_ref[...]) * scale)
            m = functools.reduce(jnp.maximum, [jnp.max(x, axis=-1, keepdims=True) for x in s])
            p = [jnp.exp(x - m) for x in s]
            l = functools.reduce(jnp.add, [jnp.sum(x, axis=-1, keepdims=True) for x in p])
            vals = [v_refs[i][...] for i in range(3)] + [vc_ref[...]]
            o = functools.reduce(jnp.add, [_dot(p[i].astype(BF16), vals[i]) for i in range(4)])
            outs.append(o / l)
        o_ref[...] = jnp.where(first, outs[0], outs[1]).astype(o_ref.dtype)


def _na_attention(qkv, bias, n_batch, seq, n_ctx):
    t = qkv.shape[0]
    assert n_ctx == NA_QB and seq % NA_QB == 0
    nblk = seq // NA_QB
    assert nblk >= 3
    hp = NA_HEADS // 2
    kcol, vcol = A_WIDTH // LANES, 2 * A_WIDTH // LANES
    ctx_blk = n_batch * nblk

    def q_map(b, h, j):
        return (jnp.where(j == 0, ctx_blk + b, b * nblk + j - 1), h)

    def win_map(i, col):
        return lambda b, h, j: (b * nblk + jnp.clip(j - 2, 0, nblk - 3) + i, col + h)

    def ctx_map(col):
        return lambda b, h, j: (ctx_blk + b, col + h)

    def bias_map(b, h, j):
        return (jnp.where(j <= 1, 0, jnp.where(j == nblk, 2, 1)), h, 0, 0)

    blk = (NA_QB, LANES)
    in_specs = ([pl.BlockSpec(blk, q_map)]
                + [pl.BlockSpec(blk, win_map(i, kcol)) for i in range(3)]
                + [pl.BlockSpec(blk, win_map(i, vcol)) for i in range(3)]
                + [pl.BlockSpec(blk, ctx_map(kcol)), pl.BlockSpec(blk, ctx_map(vcol)),
                   pl.BlockSpec((None, 2, NA_QB, NA_KB), bias_map)])
    return pl.pallas_call(
        _na_kernel,
        grid=(n_batch, hp, nblk + 1),
        in_specs=in_specs,
        out_specs=pl.BlockSpec(blk, q_map),
        out_shape=jax.ShapeDtypeStruct((t, A_WIDTH), BF16),
        compiler_params=_params(("parallel", "parallel", "arbitrary")),
        name="na_attention",
    )(*([qkv] * 9), bias)


def _seg_mean(x, avg):
    hi = x.astype(BF16)
    lo = (x - hi.astype(F32)).astype(BF16)
    return _dot(hi, avg) + _dot(lo, avg)


def _sg_kernel(ug_ref, ws_ref, bm_ref, ng_ref, avg_ref, o_ref):
    lane = lax.broadcasted_iota(jnp.int32, (1, LANES), 1)
    first = lane < SG_GROUP_DIM
    avg = avg_ref[...]
    for c in range(ROW_TILE // SG_CHUNK):
        rows = slice(SG_CHUNK * c, SG_CHUNK * (c + 1))
        u = _gelu_tanh(ug_ref[rows, 0:B_WIDTH])
        g = _gelu_tanh(ug_ref[rows, B_WIDTH:2 * B_WIDTH])
        d = g - _seg_mean(g, avg)
        var = _seg_mean(d * d, avg)
        gn = (d * lax.rsqrt(var + EPS) * ng_ref[...]).astype(BF16)
        parts = []
        for j in range(SG_GROUPS // 2):
            gj = gn[:, LANES * j:LANES * (j + 1)]
            parts.append(jnp.where(first, _dot(ws_ref[2 * j], gj), _dot(ws_ref[2 * j + 1], gj)))
        mixed = jnp.concatenate(parts, axis=1) + bm_ref[...]
        o_ref[rows, :] = (u * mixed).astype(o_ref.dtype)


def _spatial_gating(ug, ws, bmat, ng, avg):
    t = ug.shape[0]
    tm = ROW_TILE
    return pl.pallas_call(
        _sg_kernel,
        grid=(t // tm,),
        in_specs=[pl.BlockSpec((tm, 2 * B_WIDTH), lambda i: (i, 0)), _full(ws.shape), _full(bmat.shape),
                  _full(ng.shape), _full(avg.shape)],
        out_specs=pl.BlockSpec((tm, B_WIDTH), lambda i: (i, 0)),
        out_shape=jax.ShapeDtypeStruct((t, B_WIDTH), BF16),
        compiler_params=_params(("parallel",)),
        name="spatial_gating",
    )(ug, ws, bmat, ng, avg)


def _mixer_residual(x, gate_ref, a_refs, w_refs):
    y = functools.reduce(jnp.add, [_dot(a[...], w[...]) for a, w in zip(a_refs, w_refs)])
    return x + gate_ref[...] * y


def _ffn_kernel(n_chunks, n_in, n_lat_tiles, xl_ref, xc_ref, gate_m_ref, g_ref, sh_ref, sc_ref, gate_ref, *refs):
    a_refs, w_refs = refs[:n_in], refs[n_in:2 * n_in]
    wg_ref, wu_ref, wd_ref, o_ref = refs[2 * n_in:]
    x = _mixer_residual(_stream_tile(xl_ref, xc_ref, n_lat_tiles), gate_m_ref, a_refs, w_refs)
    h = _norm_mod(x, g_ref[...], sh_ref[...], sc_ref[...]).astype(BF16)
    fc = wg_ref.shape[1] // n_chunks
    acc = None
    for c in range(n_chunks):
        cs = slice(fc * c, fc * (c + 1))
        act = (_silu(_dot(h, wg_ref[:, cs])) * _dot(h, wu_ref[:, cs])).astype(BF16)
        part = _dot(act, wd_ref[cs, :])
        acc = part if acc is None else acc + part
    o_ref[...] = x + gate_ref[...] * acc


def _ffn(x_lat, x_ctx, gate_m, acts, weights, g, sh, sc, gate, wg, wu, wd, tiles_per_batch, n_batch):
    d = x_lat.shape[1]
    t = x_lat.shape[0] + x_ctx.shape[0]
    tm = ROW_TILE
    f = wg.shape[1]
    n_chunks = 2 if f % (2 * LANES) == 0 else 1
    mod = _mod_spec(d, tiles_per_batch, n_batch)
    row = lambda i: (i, 0)
    resident = lambda w: pl.BlockSpec(w.shape, lambda i: (0, 0), pipeline_mode=pl.Buffered(1))
    return pl.pallas_call(
        functools.partial(_ffn_kernel, n_chunks, len(acts), x_lat.shape[0] // tm),
        grid=(t // tm,),
        in_specs=(_stream_specs(x_lat, x_ctx) + [mod, _full((1, d)), mod, mod, mod]
                  + [pl.BlockSpec((tm, a.shape[1]), row) for a in acts] + [resident(w) for w in weights]
                  + [resident(wg), resident(wu), resident(wd)]),
        out_specs=pl.BlockSpec((tm, d), row),
        out_shape=jax.ShapeDtypeStruct((t, d), F32),
        compiler_params=_params(("parallel",)),
        name="ffn",
    )(x_lat, x_ctx, gate_m, g, sh, sc, gate, *acts, *weights, wg, wu, wd)


def _rope_tables(n_batch, seq, n_ctx):
    t = np.arange(seq)
    pos = np.stack([t // GRID_W, t % GRID_W], axis=-1).astype(np.float32)
    n_freq = MLA_ROPE // 4
    inv = jnp.power(ROPE_BASE, -jnp.arange(n_freq, dtype=F32) / n_freq)
    ang = jnp.asarray(pos)[:, :, None] * inv
    cos, sin = jnp.cos(ang), jnp.sin(ang)
    cos_r = jnp.concatenate([cos, cos], axis=-1).reshape(seq, MLA_ROPE)
    sin_r = jnp.concatenate([-sin, sin], axis=-1).reshape(seq, MLA_ROPE)
    ones = jnp.ones((seq, MLA_NOPE), F32)
    pad = LANES - MLA_NOPE - MLA_ROPE
    cos_l = jnp.concatenate([ones, cos_r, jnp.ones((seq, pad), F32)], axis=-1)
    sin_l = jnp.concatenate([0 * ones, sin_r, jnp.zeros((seq, pad), F32)], axis=-1)
    cos_all = jnp.concatenate([jnp.tile(cos_l, (n_batch, 1)), jnp.ones((n_batch * n_ctx, LANES), F32)])
    sin_all = jnp.concatenate([jnp.tile(sin_l, (n_batch, 1)), jnp.zeros((n_batch * n_ctx, LANES), F32)])
    return cos_all, sin_all


def _swap_rope_cols(w_rope):
    q = MLA_ROPE // 4
    return jnp.concatenate([w_rope[..., q:2 * q], w_rope[..., 0:q], w_rope[..., 3 * q:4 * q],
                            w_rope[..., 2 * q:3 * q]], axis=-1)


def _mla_weights(w_in, w_uq, w_ukv):
    d = w_in.shape[0]
    pad = LANES - MLA_NOPE - MLA_ROPE
    w_kr = w_in[:, MLA_Q_RANK + MLA_KV_RANK:]
    lanes_kr = lambda w: jnp.concatenate([jnp.zeros((d, MLA_NOPE), F32), w, jnp.zeros((d, pad), F32)], axis=1)
    win = jnp.concatenate([w_in[:, :MLA_Q_RANK + MLA_KV_RANK], lanes_kr(w_kr), lanes_kr(_swap_rope_cols(w_kr))],
                          axis=1)
    uq = w_uq.reshape(MLA_Q_RANK, MLA_HEADS, MLA_NOPE + MLA_ROPE)
    zq = jnp.zeros((MLA_Q_RANK, MLA_HEADS, pad), F32)
    uq_a = jnp.concatenate([uq, zq], axis=-1).reshape(MLA_Q_RANK, MLA_HEADS * LANES)
    uq_b = jnp.concatenate([jnp.zeros((MLA_Q_RANK, MLA_HEADS, MLA_NOPE), F32),
                            _swap_rope_cols(uq[..., MLA_NOPE:]), zq], axis=-1).reshape(MLA_Q_RANK, MLA_HEADS * LANES)
    wuq_t = jnp.concatenate([uq_a, uq_b], axis=1).T
    ukv = w_ukv.reshape(MLA_KV_RANK, MLA_HEADS, MLA_NOPE + MLA_V)
    zk = jnp.zeros((MLA_KV_RANK, MLA_HEADS, LANES - MLA_NOPE), F32)
    uk = jnp.concatenate([ukv[..., :MLA_NOPE], zk], axis=-1).reshape(MLA_KV_RANK, MLA_HEADS * LANES)
    uv_t = jnp.transpose(ukv[..., MLA_NOPE:], (1, 2, 0))
    uv_t = jnp.concatenate([uv_t, jnp.zeros((MLA_HEADS, MLA_VROWS - MLA_V, MLA_KV_RANK), F32)], axis=1)
    return (win.astype(BF16), wuq_t.astype(BF16), uk.astype(BF16),
            uv_t.reshape(MLA_HEADS * MLA_VROWS, MLA_KV_RANK).astype(BF16))


def _mla_pre_kernel(x_ref, g_ref, sh_ref, sc_ref, win_ref, qg_ref, kvg_ref, wuqt_ref, wuk_ref, wuvt_ref,
                    cos_ref, sin_ref, cost_ref, sint_ref, qt_ref, k_ref, vt_ref):
    h = _norm_mod(x_ref[...], g_ref[...], sh_ref[...], sc_ref[...]).astype(BF16)
    p = _dot(h, win_ref[...])
    cq = _rms(p[:, :MLA_Q_RANK], qg_ref[...])
    ckv = _rms(p[:, MLA_Q_RANK:MLA_Q_RANK + MLA_KV_RANK], kvg_ref[...])
    cq_t = cq.T.astype(BF16)
    ckv_t = ckv.T.astype(BF16)
    ckv_b = ckv.astype(BF16)
    o_kr = MLA_Q_RANK + MLA_KV_RANK
    kr_rot = p[:, o_kr:o_kr + LANES] * cos_ref[...] + p[:, o_kr + LANES:o_kr + 2 * LANES] * sin_ref[...]
    kr2 = jnp.concatenate([kr_rot, kr_rot], axis=1)
    cos_t2 = jnp.concatenate([cost_ref[...], cost_ref[...]], axis=0)
    sin_t2 = jnp.concatenate([sint_ref[...], sint_ref[...]], axis=0)
    half = MLA_HEADS * LANES
    for c in range(MLA_HEADS // 2):
        rs = slice(2 * LANES * c, 2 * LANES * (c + 1))
        rs_b = slice(half + 2 * LANES * c, half + 2 * LANES * (c + 1))
        q_t = _dot(wuqt_ref[rs, :], cq_t) * cos_t2 + _dot(wuqt_ref[rs_b, :], cq_t) * sin_t2
        qt_ref[rs, :] = (q_t * (MLA_SCALE * LOG2E)).astype(qt_ref.dtype)
        k_ref[:, rs] = (_dot(ckv_b, wuk_ref[:, rs]) + kr2).astype(k_ref.dtype)
    group = 4 * MLA_VROWS
    row = lax.broadcasted_iota(jnp.int32, (group, 1), 0)
    ones_rows = functools.reduce(jnp.add, [jnp.where(row == MLA_VROWS * j + MLA_V, 1.0, 0.0) for j in range(4)])
    for c in range(MLA_HEADS // 4):
        rv = slice(group * c, group * (c + 1))
        vt_ref[rv, :] = (_dot(wuvt_ref[rv, :], ckv_t) + ones_rows).astype(vt_ref.dtype)


def _mla_pre(x, g, sh, sc, win, qg, kvg, wuq_t, wuk, wuv_t, cos, sin, tiles_per_batch, n_batch):
    t, d = x.shape
    tm = ROW_TILE
    n = MLA_HEADS * LANES
    nv = MLA_HEADS * MLA_VROWS
    mod = _mod_spec(d, tiles_per_batch, n_batch)
    row = lambda i: (i, 0)
    col = lambda i: (0, i)
    return pl.pallas_call(
        _mla_pre_kernel,
        grid=(t // tm,),
        in_specs=[pl.BlockSpec((tm, d), row), _full((1, d)), mod, mod, _full(win.shape), _full(qg.shape),
                  _full(kvg.shape), _full(wuq_t.shape), _full(wuk.shape), _full(wuv_t.shape),
                  pl.BlockSpec((tm, LANES), row), pl.BlockSpec((tm, LANES), row),
                  pl.BlockSpec((LANES, tm), col), pl.BlockSpec((LANES, tm), col)],
        out_specs=[pl.BlockSpec((None, n, tm), lambda i: (i, 0, 0)), pl.BlockSpec((tm, n), row),
                   pl.BlockSpec((None, nv, tm), lambda i: (i, 0, 0))],
        out_shape=[jax.ShapeDtypeStruct((t // tm, n, tm), BF16), jax.ShapeDtypeStruct((t, n), BF16),
                   jax.ShapeDtypeStruct((t // tm, nv, tm), BF16)],
        compiler_params=_params(("parallel",)),
        name="mla_pre",
    )(x, g, sh, sc, win, qg, kvg, wuq_t, wuk, wuv_t, cos, sin, cos.T, sin.T)


def _col_max(s_t):
    parts = [s_t[r:r + 64] for r in range(0, s_t.shape[0], 64)]
    while len(parts) > 1:
        parts = [jnp.maximum(parts[j], parts[j + 1]) for j in range(0, len(parts) - 1, 2)] + parts[len(parts) & ~1:]
    return jnp.max(parts[0], axis=0, keepdims=True)


def _flash_kernel(qt_ref, k_ref, vt_ref, kc_ref, vtc_ref, o_ref, sa0_ref, sa1_ref, sb0_ref, sb1_ref):
    tq = qt_ref.shape[1]
    n_blocks, _, tk = vt_ref.shape
    n_ctx = kc_ref.shape[0]
    heads = range(2)
    s_a, s_b = (sa0_ref, sa1_ref), (sb0_ref, sb1_ref)
    lanes = lambda hh: slice(LANES * hh, LANES * (hh + 1))
    vrows = lambda hh: slice(MLA_VROWS * hh, MLA_VROWS * (hh + 1))

    def scores(kb_of, bufs, rows):
        cmax = []
        for hh in heads:
            s_t = _dot(kb_of(hh), qt_ref[lanes(hh), :])
            bufs[hh][0:rows, :] = s_t
            cmax.append(_col_max(s_t))
        return tuple(cmax)

    def absorb(bufs, rows, cmax, vt_of, state):
        new = []
        for hh in heads:
            m, acc = state[hh]
            m_new = jnp.maximum(m, cmax[hh])
            p_t = jnp.exp2(bufs[hh][0:rows, :] - m_new).astype(BF16)
            new.append((m_new, acc * jnp.exp2(m - m_new) + _dot(vt_of(hh), p_t)))
        return tuple(new)

    def lat_k(blk):
        off = blk * tk if isinstance(blk, int) else pl.multiple_of(blk * tk, tk)
        return lambda hh: k_ref[pl.ds(off, tk), lanes(hh)]

    lat_v = lambda blk: (lambda hh: vt_ref[blk, vrows(hh), :])
    ctx_k = lambda hh: kc_ref[:, lanes(hh)]
    ctx_v = lambda hh: vtc_ref[vrows(hh), :]

    def pair(blk, cmax_a, state, last):
        cmax_b = scores(lat_k(blk + 1), s_b, tk)
        state = absorb(s_a, tk, cmax_a, lat_v(blk), state)
        cmax_a = scores(ctx_k, s_a, n_ctx) if last else scores(lat_k(blk + 2), s_a, tk)
        return cmax_a, absorb(s_b, tk, cmax_b, lat_v(blk + 1), state)

    def two_pairs(j, carry):
        return pair(4 * j + 2, *pair(4 * j, *carry, last=False), last=False)

    init = tuple((jnp.full((1, tq), NEG, F32), jnp.zeros((MLA_VROWS, tq), F32)) for _ in heads)
    carry = (scores(lat_k(0), s_a, tk), init)
    inner_pairs = n_blocks // 2 - 1
    carry = lax.fori_loop(0, inner_pairs // 2, two_pairs, carry)
    if inner_pairs % 2:
        carry = pair(n_blocks - 4, *carry, last=False)
    cmax_a, state = pair(n_blocks - 2, *carry, last=True)
    state = absorb(s_a, n_ctx, cmax_a, ctx_v, state)
    outs = [acc[:MLA_V] / acc[MLA_V:MLA_V + 1] for _, acc in state]
    o_ref[...] = jnp.concatenate(outs, axis=0).T.astype(o_ref.dtype)


def _mla_attention(qt, k, vt, vtc, n_batch, seq, n_ctx):
    tq = ROW_TILE
    nq = seq // tq
    ctx0 = n_batch * seq // n_ctx
    return pl.pallas_call(
        _flash_kernel,
        grid=(n_batch, MLA_HEADS // 2, nq),
        in_specs=[pl.BlockSpec((None, 2 * LANES, tq), lambda b, h, i: (b * nq + i, h, 0)),
                  pl.BlockSpec((seq, 2 * LANES), lambda b, h, i: (b, h)),
                  pl.BlockSpec((nq, 2 * MLA_VROWS, tq), lambda b, h, i: (b, h, 0)),
                  pl.BlockSpec((n_ctx, 2 * LANES), lambda b, h, i: (ctx0 + b, h)),
                  pl.BlockSpec((None, 2 * MLA_VROWS, n_ctx), lambda b, h, i: (b, h, 0))],
        out_specs=pl.BlockSpec((tq, LANES), lambda b, h, i: (b * nq + i, h)),
        out_shape=jax.ShapeDtypeStruct((n_batch * seq, MLA_HEADS * MLA_V), BF16),
        scratch_shapes=[pltpu.VMEM((tq, tq), F32)] * 4,
        compiler_params=_params(("parallel", "parallel", "arbitrary")),
        name="mla_flash",
    )(qt, k, vt, k, vtc)


def _router_kernel(x_ref, gate_m_ref, o_ref, wo_ref, g_ref, sh_ref, sc_ref, wr_ref, br_ref, tri_ref,
                   x1_ref, h_ref, mi_ref, mf_ref, cnt_ref, run_ref):
    @pl.when(pl.program_id(0) == 0)
    def _init():
        run_ref[...] = jnp.zeros_like(run_ref)

    x1 = _mixer_residual(x_ref[...], gate_m_ref, [o_ref], [wo_ref])
    x1_ref[...] = x1
    h = _norm_mod(x1, g_ref[...], sh_ref[...], sc_ref[...])
    h_ref[...] = h
    logits = jnp.dot(h, wr_ref[...], preferred_element_type=F32, precision=lax.Precision.HIGHEST) + br_ref[...]
    lane = lax.broadcasted_iota(jnp.int32, logits.shape, 1).astype(F32)
    m1 = jnp.max(logits, axis=-1, keepdims=True)
    i1 = jnp.min(jnp.where(logits == m1, lane, float(LANES)), axis=-1, keepdims=True)
    rest = jnp.where(lane == i1, NEG, logits)
    m2 = jnp.max(rest, axis=-1, keepdims=True)
    i2 = jnp.min(jnp.where(rest == m2, lane, float(LANES)), axis=-1, keepdims=True)
    e = jnp.exp(m2 - m1)
    w1 = 1.0 / (1.0 + e)
    w2 = e / (1.0 + e)
    hit1, hit2 = lane == i1, lane == i2
    onehot = jnp.where(jnp.logical_or(hit1, hit2), 1.0, 0.0)
    before = _dot(tri_ref[...], onehot.astype(BF16)) + run_ref[0:1, :]
    r1 = jnp.sum(jnp.where(hit1, before, 0.0), axis=-1, keepdims=True)
    r2 = jnp.sum(jnp.where(hit2, before, 0.0), axis=-1, keepdims=True)
    run_ref[...] = run_ref[...] + jnp.sum(onehot, axis=0, keepdims=True)
    meta = jnp.where(lane == 0.0, i1, jnp.where(lane == 1.0, i2, jnp.where(lane == 2.0, r1, jnp.where(lane == 3.0, r2, 0.0))))
    mi_ref[...] = meta.astype(jnp.int32)
    mf_ref[...] = jnp.where(lane == 0.0, w1, jnp.where(lane == 1.0, w2, 0.0))
    cnt_ref[...] = run_ref[...]


def _router(x, gate_m, o, wo, g, sh, sc, wr, br, tri, n_rows, tiles_per_batch, n_batch):
    d = x.shape[1]
    tm = ROW_TILE
    mod = _mod_spec(d, tiles_per_batch, n_batch)
    row = lambda i: (i, 0)
    return pl.pallas_call(
        _router_kernel,
        grid=(n_rows // tm,),
        in_specs=[pl.BlockSpec((tm, d), row), mod, pl.BlockSpec((tm, o.shape[1]), row), _full(wo.shape),
                  _full((1, d)), mod, mod, _full(wr.shape), _full(br.shape), _full(tri.shape)],
        out_specs=[pl.BlockSpec((tm, d), row), pl.BlockSpec((tm, d), row), pl.BlockSpec((tm, LANES), row),
                   pl.BlockSpec((tm, LANES), row), _full((8, LANES))],
        out_shape=[jax.ShapeDtypeStruct((n_rows, d), F32), jax.ShapeDtypeStruct((n_rows, d), F32),
                   jax.ShapeDtypeStruct((n_rows, LANES), jnp.int32), jax.ShapeDtypeStruct((n_rows, LANES), F32),
                   jax.ShapeDtypeStruct((8, LANES), F32)],
        scratch_shapes=[pltpu.VMEM((8, LANES), F32)],
        compiler_params=_params(("arbitrary",)),
        name="moe_router",
    )(x, gate_m, o, wo, g, sh, sc, wr, br, tri)


def _moe_kernel(te_ref, src_ref, dst_ref, rows_ref, nv_ref, h_hbm, wg_ref, wu_ref, wd_ref, y_hbm, hbuf, ybuf,
                sem_g, sem_s):
    i, f = pl.program_id(0), pl.program_id(1)
    _, tm, _ = hbuf.shape
    half = tm // MOE_FCHUNKS
    nv = nv_ref[0]
    valid = i < nv
    slot = lax.rem(i, 2)
    other = 1 - slot

    def gather_row(idx, s, r):
        return pltpu.make_async_copy(h_hbm.at[pl.ds(idx, 1), :], hbuf.at[s, pl.ds(r, 1), :], sem_g.at[s])

    def scatter_row(idx, s, r):
        return pltpu.make_async_copy(ybuf.at[s, pl.ds(r, 1), :], y_hbm.at[pl.ds(idx, 1), :], sem_s.at[s])

    def wait_gather(s):
        pltpu.make_async_copy(h_hbm.at[pl.ds(0, tm), :], hbuf.at[s], sem_g.at[s]).wait()

    def wait_scatter(s, n):
        n8 = pl.multiple_of((n // 8) * 8, 8)

        @pl.when(n8 > 0)
        def _():
            pltpu.make_async_copy(ybuf.at[s, pl.ds(0, n8), :], y_hbm.at[pl.ds(0, n8), :], sem_s.at[s]).wait()

        def one_row(j, carry):
            scatter_row(0, s, 0).wait()
            return carry

        lax.fori_loop(0, n - n8, one_row, 0)

    @pl.when(jnp.logical_and(i == 0, f == 0))
    def _prologue():
        ybuf[...] = jnp.zeros_like(ybuf)

        def issue(j, carry):
            gather_row(src_ref[j], 0, j).start()
            return carry

        lax.fori_loop(0, tm, issue, 0)

    @pl.when(jnp.logical_and(f == 0, i <= nv))
    def _rows_ready():
        wait_gather(slot)

    @pl.when(jnp.logical_and(f == 0, jnp.logical_and(i >= 1, i <= nv)))
    def _slot_free():
        wait_scatter(slot, rows_ref[jnp.maximum(i - 1, 0)])

    @pl.when(valid)
    def _compute():
        hb = hbuf[slot].astype(BF16)
        r0 = f * half
        n_prev = rows_ref[i]
        for j in range(half):
            gather_row(src_ref[(i + 1) * tm + r0 + j], other, r0 + j).start()

            @pl.when(r0 + j < n_prev)
            def _():
                scatter_row(dst_ref[i * tm + r0 + j], other, r0 + j).start()
        act = (_silu(_dot(hb, wg_ref[...])) * _dot(hb, wu_ref[...])).astype(BF16)
        part = _dot(act, wd_ref[...])
        ybuf[slot] = jnp.where(f == 0, part, ybuf[slot] + part)

    @pl.when(jnp.logical_and(i == nv, f == 0))
    def _flush():
        n_last = rows_ref[i]

        def issue(j, carry):
            scatter_row(dst_ref[i * tm + j], other, j).start()
            return carry

        lax.fori_loop(0, n_last, issue, 0)
        wait_scatter(other, n_last)


def _moe_experts(tile_expert, src, dst, rows, n_valid, h, wg, wu, wd, n_tiles, y_rows):
    d = h.shape[1]
    tm = MOE_TILE
    nf = MOE_FCHUNKS
    fc = wg.shape[2] // nf

    def f_idx(i, f, nv):
        return jnp.where(i < nv[0], f, nf - 1)

    grid_spec = pltpu.PrefetchScalarGridSpec(
        num_scalar_prefetch=5,
        grid=(n_tiles + 1, nf),
        in_specs=[pl.BlockSpec(memory_space=pl.ANY),
                  pl.BlockSpec((None, d, fc), lambda i, f, te, src, dst, rows, nv: (te[i], 0, f_idx(i, f, nv))),
                  pl.BlockSpec((None, d, fc), lambda i, f, te, src, dst, rows, nv: (te[i], 0, f_idx(i, f, nv))),
                  pl.BlockSpec((None, fc, d), lambda i, f, te, src, dst, rows, nv: (te[i], f_idx(i, f, nv), 0))],
        out_specs=pl.BlockSpec(memory_space=pl.ANY),
        scratch_shapes=[pltpu.VMEM((2, tm, d), F32), pltpu.VMEM((2, tm, d), F32),
                        pltpu.SemaphoreType.DMA((2,)), pltpu.SemaphoreType.DMA((2,))],
    )
    return pl.pallas_call(
        _moe_kernel,
        grid_spec=grid_spec,
        out_shape=jax.ShapeDtypeStruct((y_rows, d), F32),
        compiler_params=_params(("arbitrary", "arbitrary")),
        name="moe_experts",
    )(tile_expert, src, dst, rows, n_valid, h, wg, wu, wd)


def _combine_kernel(x_ref, gate_ref, mf_ref, fg_ref, y1_ref, y2_ref, o_ref):
    w = mf_ref[...]
    y = w[:, 0:1] * y1_ref[...] + w[:, 1:2] * y2_ref[...]
    o_ref[...] = _rms(x_ref[...] + gate_ref[...] * y, fg_ref[...])


def _moe_combine(x, gate, mf, fg, y, tiles_per_batch, n_batch):
    t, d = x.shape
    tm = ROW_TILE
    row = lambda i: (i, 0)
    return pl.pallas_call(
        _combine_kernel,
        grid=(t // tm,),
        in_specs=[pl.BlockSpec((tm, d), row), _mod_spec(d, tiles_per_batch, n_batch), pl.BlockSpec((tm, LANES), row),
                  _full((1, d)), pl.BlockSpec((tm, d), row), pl.BlockSpec((tm, d), lambda i: (t // tm + i, 0))],
        out_specs=pl.BlockSpec((tm, d), row),
        out_shape=jax.ShapeDtypeStruct((t, d), F32),
        compiler_params=_params(("parallel",)),
        name="moe_combine",
    )(x, gate, mf, fg, y, y)


def kernel(x, c, ctx, c_ctx, ada_w, ada_b, norm_g, final_g, na_w_in, na_rpb, sg_w, sg_b, sg_norm_g, even_w_out,
           ffn_w_gate, ffn_w_up, ffn_w_down, mla_w_in, mla_q_norm_g, mla_kv_norm_g, mla_w_uq, mla_w_ukv, mla_w_out,
           moe_w_router, moe_b_router, moe_w_gate, moe_w_up, moe_w_down):
    n_batch, seq, d = x.shape
    n_ctx = ctx.shape[1]
    n_lat = n_batch * seq
    assert ada_w.shape[0] == 2 and seq % ROW_TILE == 0 and n_batch * n_ctx == ROW_TILE
    assert n_batch + 1 <= 8 and seq % GRID_W == 0
    tpb = seq // ROW_TILE

    cond8 = jnp.concatenate([c, c_ctx[None, :], jnp.zeros((8 - n_batch - 1, d), F32)], axis=0)
    mods = _ada_mod(cond8, ada_w, ada_b)[:, :n_batch + 1]
    mod = lambda layer, k: mods[layer, :, k * d:(k + 1) * d].reshape(n_batch + 1, 1, d)
    x_lat, x_ctx = x.reshape(n_lat, d), ctx.reshape(n_batch * n_ctx, d)

    w_in = na_w_in[0].astype(BF16)
    qkv, ug = _even_in(x_lat, x_ctx, norm_g[0, 0][None], mod(0, 0), mod(0, 1), w_in[:, :3 * A_WIDTH],
                       w_in[:, 3 * A_WIDTH:], tpb, n_batch)
    bias = _na_bias_table(na_rpb[0], seq // GRID_W)
    attn = _na_attention(qkv, bias, n_batch, seq, n_ctx)
    bmat = jnp.repeat(sg_b[0].T, SG_GROUP_DIM, axis=1)
    avg = jnp.asarray(np.kron(np.eye(SG_GROUPS), np.full((SG_GROUP_DIM, SG_GROUP_DIM), 1.0 / SG_GROUP_DIM)), BF16)
    gated = _spatial_gating(ug, sg_w[0].astype(BF16), bmat, sg_norm_g[0][None], avg)
    w_out = even_w_out[0].astype(BF16)
    xs = _ffn(x_lat, x_ctx, mod(0, 2), [attn, gated], [w_out[:A_WIDTH], w_out[A_WIDTH:]], norm_g[0, 1][None], mod(0, 3),
              mod(0, 4), mod(0, 5), ffn_w_gate[0].astype(BF16), ffn_w_up[0].astype(BF16), ffn_w_down[0].astype(BF16),
              tpb, n_batch)

    win, wuq_t, wuk, wuv_t = _mla_weights(mla_w_in[0], mla_w_uq[0], mla_w_ukv[0])
    cos, sin = _rope_tables(n_batch, seq, n_ctx)
    qt, k, vt = _mla_pre(xs, norm_g[1, 0][None], mod(1, 0), mod(1, 1), win, mla_q_norm_g[0][None],
                         mla_kv_norm_g[0][None], wuq_t, wuk, wuv_t, cos, sin, tpb, n_batch)
    vtc = jnp.transpose(vt[n_batch * tpb].reshape(-1, n_batch, n_ctx), (1, 0, 2))
    o = _mla_attention(qt, k, vt, vtc, n_batch, seq, n_ctx)
    wr =jnp.concatenate([moe_w_router[0], jnp.zeros((d, LANES - N_EXPERTS), F32)], axis=1)
    br = jnp.concatenate([moe_b_router[0], jnp.full((LANES - N_EXPERTS,), NEG, F32)])[None]
    tri = jnp.asarray(np.tril(np.ones((ROW_TILE, ROW_TILE), np.float32), -1), BF16)
    x1, h, mi, mf, cnt = _router(xs, mod(1, 2), o, mla_w_out[0].astype(BF16), norm_g[1, 1][None], mod(1, 3),
                                 mod(1, 4), wr, br, tri, n_lat, tpb, n_batch)

    counts = cnt[0, :N_EXPERTS].astype(jnp.int32)
    tiles_e = (counts + MOE_TILE - 1) // MOE_TILE
    tile_end = jnp.cumsum(tiles_e)
    start = (tile_end - tiles_e) * MOE_TILE
    slot12 = jnp.concatenate([start[mi[:, 0]] + mi[:, 2], start[mi[:, 1]] + mi[:, 3]])
    n_tiles = 2 * n_lat // MOE_TILE + N_EXPERTS
    n_slots = n_tiles * MOE_TILE
    dst = jnp.zeros((n_slots,), jnp.int32).at[slot12].set(jnp.arange(2 * n_lat, dtype=jnp.int32))
    spare_tile = jnp.zeros((MOE_TILE,), jnp.int32)
    src = jnp.concatenate([dst % n_lat, spare_tile])
    dst = jnp.concatenate([spare_tile, dst])
    n_valid = tile_end[-1:]
    last_tile = jnp.maximum(n_valid[0] - 1, 0)
    all_tiles = jnp.arange(n_tiles + 1, dtype=jnp.int32)
    tile_ids = jnp.minimum(all_tiles, last_tile)
    tile_expert = jnp.minimum(jnp.sum((tile_ids[:, None] >= tile_end[None, :]).astype(jnp.int32), axis=1),
                              N_EXPERTS - 1)
    filled = counts[tile_expert] - (all_tiles - (tile_end - tiles_e)[tile_expert]) * MOE_TILE
    rows = jnp.where(all_tiles < n_valid[0], jnp.clip(filled, 0, MOE_TILE), 0)
    rows = jnp.concatenate([jnp.zeros((1,), jnp.int32), rows]).astype(jnp.int32)
    y = _moe_experts(tile_expert, src, dst, rows, n_valid.astype(jnp.int32), h, moe_w_gate[0].astype(BF16),
                     moe_w_up[0].astype(BF16), moe_w_down[0].astype(BF16), n_tiles, 2 * n_lat)
    out = _moe_combine(x1, mod(1, 5), mf, final_g[None], y, tpb, n_batch)
    return out.reshape(n_batch, seq, d)
```

```python
import functools

import numpy as np
import jax
import jax.numpy as jnp
from jax import lax
from jax.experimental import pallas as pl
from jax.experimental.pallas import tpu as pltpu

F32 = jnp.float32
BF16 = jnp.bfloat16
EPS = 1e-6
NEG = -1e30

LANES = 128
VMEM_LIMIT_BYTES = 56 * 1024 * 1024

GRID_W = 64
NA_HEADS = 8
NA_HEAD_DIM = 64
NA_WIN_R = 8
NA_WIN_C = 16
SG_GROUPS = 8
SG_GROUP_DIM = 64
SG_CHUNK = 128
A_WIDTH = NA_HEADS * NA_HEAD_DIM
B_WIDTH = SG_GROUPS * SG_GROUP_DIM
MLA_HEADS = 16
MLA_NOPE = 64
MLA_ROPE = 32
MLA_V = 64
MLA_Q_RANK = 384
MLA_KV_RANK = 256
MLA_SCALE = (MLA_NOPE + MLA_ROPE) ** -0.5
ROPE_BASE = 10000.0
N_EXPERTS = 8

ROW_TILE = 512
NA_QROWS = 4
NA_QB = NA_QROWS * GRID_W
NA_KB = 3 * NA_QB
NA_STEP_HEADS = 4
MLA_VROWS = 80
LOG2E = 1.4426950408889634
MOE_TILE = 512
MOE_FCHUNKS = 2


def _params(sem):
    return pltpu.CompilerParams(dimension_semantics=sem, vmem_limit_bytes=VMEM_LIMIT_BYTES)


def _dot(a, b):
    return jnp.dot(a, b, preferred_element_type=F32)


def _dot_nt(a, b):
    return lax.dot_general(a, b, (((1,), (1,)), ((), ())), preferred_element_type=F32)


def _silu(x):
    return x / (1.0 + jnp.exp(-x))


def _gelu_tanh(x):
    return 0.5 * x * (1.0 + jnp.tanh(0.7978845608028654 * (x + 0.044715 * (x * x * x))))


def _rms(x, g):
    return x * lax.rsqrt(jnp.mean(x * x, axis=-1, keepdims=True) + EPS) * g


def _norm_mod(x, g, sh, sc):
    return _rms(x, g) * (1.0 + sc) + sh


def _full(shape):
    n = len(shape)
    return pl.BlockSpec(shape, lambda *_: (0,) * n)


def _ada_kernel(cond_ref, w_ref, b_ref, o_ref):
    c = cond_ref[...]
    o_ref[...] = jnp.dot(_silu(c), w_ref[...], preferred_element_type=F32,
                         precision=lax.Precision.HIGHEST) + b_ref[...]


def _ada_mod(cond8, ada_w, ada_b):
    depth, d, n = ada_w.shape
    tn = n // 4
    return pl.pallas_call(
        _ada_kernel,
        grid=(depth, n // tn),
        in_specs=[_full((8, d)),
                  pl.BlockSpec((None, d, tn), lambda l, j: (l, 0, j)),
                  pl.BlockSpec((None, 1, tn), lambda l, j: (l, 0, j))],
        out_specs=pl.BlockSpec((None, 8, tn), lambda l, j: (l, 0, j)),
        out_shape=jax.ShapeDtypeStruct((depth, 8, n), F32),
        compiler_params=_params(("parallel", "parallel")),
        name="ada_mod",
    )(cond8, ada_w, ada_b.reshape(depth, 1, n))


def _group_map(tiles_per_batch, n_batch):
    return lambda t: (jnp.minimum(t // tiles_per_batch, n_batch), 0, 0)


def _mod_spec(d, tiles_per_batch, n_batch):
    return pl.BlockSpec((None, 1, d), _group_map(tiles_per_batch, n_batch))


def _stream_specs(x_lat, x_ctx):
    tm, d = ROW_TILE, x_lat.shape[1]
    assert x_ctx.shape[0] == tm
    last = x_lat.shape[0] // tm - 1
    return [pl.BlockSpec((tm, d), lambda i: (jnp.minimum(i, last), 0)), pl.BlockSpec((tm, d), lambda i: (0, 0))]


def _stream_tile(xl_ref, xc_ref, n_lat_tiles):
    return jnp.where(pl.program_id(0) < n_lat_tiles, xl_ref[...], xc_ref[...])


def _even_in_kernel(n_lat_tiles, xl_ref, xc_ref, g_ref, sh_ref, sc_ref, wqkv_ref, wug_ref, qt_ref, k_ref, vt_ref,
                    ug_ref):
    x = _stream_tile(xl_ref, xc_ref, n_lat_tiles)
    h = _norm_mod(x, g_ref[...], sh_ref[...], sc_ref[...]).astype(BF16)
    qkv = _dot(h, wqkv_ref[...])
    qt_ref[...] = (qkv[:, :A_WIDTH] * (NA_HEAD_DIM ** -0.5 * LOG2E)).T.astype(qt_ref.dtype)
    k_ref[...] = qkv[:, A_WIDTH:2 * A_WIDTH].astype(k_ref.dtype)
    vt_ref[...] = qkv[:, 2 * A_WIDTH:].T.astype(vt_ref.dtype)
    ug_ref[...] = _dot(h, wug_ref[...])


def _even_in(x_lat, x_ctx, g, sh, sc, wqkv, wug, tiles_per_batch, n_batch):
    d = x_lat.shape[1]
    t = x_lat.shape[0] + x_ctx.shape[0]
    tm = ROW_TILE
    mod = _mod_spec(d, tiles_per_batch, n_batch)
    return pl.pallas_call(
        functools.partial(_even_in_kernel, x_lat.shape[0] // tm),
        grid=(t // tm,),
        in_specs=_stream_specs(x_lat, x_ctx) + [_full((1, d)), mod, mod, _full(wqkv.shape), _full(wug.shape)],
        out_specs=[pl.BlockSpec((A_WIDTH, tm), lambda i: (0, i)), pl.BlockSpec((tm, A_WIDTH), lambda i: (i, 0)),
                   pl.BlockSpec((A_WIDTH, tm), lambda i: (0, i)), pl.BlockSpec((tm, wug.shape[1]), lambda i: (i, 0))],
        out_shape=[jax.ShapeDtypeStruct((A_WIDTH, t), BF16), jax.ShapeDtypeStruct((t, A_WIDTH), BF16),
                   jax.ShapeDtypeStruct((A_WIDTH, t), BF16), jax.ShapeDtypeStruct((t, wug.shape[1]), F32)],
        compiler_params=_params(("parallel",)),
        name="even_in",
    )(x_lat, x_ctx, g, sh, sc, wqkv, wug)


def _na_bias_table(rpb, rows):
    nblk = rows // NA_QROWS
    n_heads = rpb.shape[0]
    win_rows = NA_KB // GRID_W
    qc = np.arange(GRID_W)
    col_start = np.clip(qc - NA_WIN_C // 2, 0, GRID_W - NA_WIN_C)
    col_ok = (qc[None, :] >= col_start[:, None]) & (qc[None, :] < col_start[:, None] + NA_WIN_C)
    col_j = qc[None, :] - qc[:, None] + NA_WIN_C - 1
    sel_c = ((col_j[None] == np.arange(2 * NA_WIN_C - 1)[:, None, None]) & col_ok[None]).astype(np.float32)
    toeplitz = jnp.einsum("hij,jqk->hikq", rpb, sel_c, precision=lax.Precision.HIGHEST) * LOG2E
    toeplitz = jnp.where(col_ok.T[None, None], toeplitz, NEG)
    n_rel = 2 * NA_WIN_R - 1
    toeplitz = jnp.concatenate([toeplitz, jnp.full((n_heads, 1, GRID_W, GRID_W), NEG, F32)], axis=1)
    rel = np.full((3, NA_QROWS, win_rows), n_rel, np.int32)
    for kind, j in enumerate((0, 1, nblk - 1)):
        first_row = NA_QROWS * int(np.clip(j - 1, 0, nblk - 3))
        for a in range(NA_QROWS):
            r = NA_QROWS * j + a
            row_start = int(np.clip(r - NA_WIN_R // 2, 0, rows - NA_WIN_R))
            for b in range(win_rows):
                kr = first_row + b
                if row_start <= kr < row_start + NA_WIN_R:
                    rel[kind, a, b] = kr - r + NA_WIN_R - 1
    grid_spec = pltpu.PrefetchScalarGridSpec(
        num_scalar_prefetch=1,
        grid=(3, n_heads),
        in_specs=[pl.BlockSpec((None, n_rel + 1, GRID_W, GRID_W), lambda c, h, rel: (h, 0, 0, 0))],
        out_specs=pl.BlockSpec((None, None, NA_KB, NA_QB), lambda c, h, rel: (c, h, 0, 0)),
    )
    return pl.pallas_call(
        _na_bias_kernel,
        grid_spec=grid_spec,
        out_shape=jax.ShapeDtypeStruct((3, n_heads, NA_KB, NA_QB), F32),
        compiler_params=_params(("parallel", "parallel")),
        name="na_bias",
    )(jnp.asarray(rel.reshape(-1)), toeplitz)


def _na_bias_kernel(rel_ref, t_ref, o_ref):
    kind = pl.program_id(0)
    win_rows = NA_KB // GRID_W
    for b in range(win_rows):
        for ap in range(NA_QROWS // 2):
            rel = [rel_ref[(kind * NA_QROWS + 2 * ap + s) * win_rows + b] for s in range(2)]
            pair = jnp.concatenate([t_ref[rel[0]], t_ref[rel[1]]], axis=1)
            o_ref[GRID_W * b:GRID_W * (b + 1), 2 * GRID_W * ap:2 * GRID_W * (ap + 1)] = pair


def _col_reduce(x_t, combine, reduce):
    parts = [x_t[r:r + 64] for r in range(0, x_t.shape[0], 64)]
    while len(parts) > 1:
        parts = [combine(parts[j], parts[j + 1]) for j in range(0, len(parts) - 1, 2)] + parts[len(parts) & ~1:]
    return reduce(parts[0], axis=0, keepdims=True)


def _col_max(s_t):
    return _col_reduce(s_t, jnp.maximum, jnp.max)


def _col_sum(p_t):
    return _col_reduce(p_t, jnp.add, jnp.sum)


def _na_kernel(qt_ref, k0_ref, k1_ref, k2_ref, vt0_ref, vt1_ref, vt2_ref, kc_ref, vtc_ref, bias_ref, o_ref):
    jj = pl.program_id(2)
    row = lax.broadcasted_iota(jnp.int32, (NA_STEP_HEADS * NA_HEAD_DIM, 1), 0)
    qt = qt_ref[...]
    zero = jnp.zeros_like(qt)
    heads = range(NA_STEP_HEADS)

    def head_q(hh):
        mine = jnp.logical_and(row >= NA_HEAD_DIM * hh, row < NA_HEAD_DIM * (hh + 1))
        return jnp.where(mine, qt, zero)

    def attend(s_t, vt_refs, hh):
        m = functools.reduce(jnp.maximum, [_col_max(x) for x in s_t])
        p_t = [jnp.exp2(x - m) for x in s_t]
        l = functools.reduce(jnp.add, [_col_sum(x) for x in p_t])
        vrows = slice(NA_HEAD_DIM * hh, NA_HEAD_DIM * (hh + 1))
        o_t = functools.reduce(jnp.add, [_dot(vt[vrows, :], x.astype(BF16)) for vt, x in zip(vt_refs, p_t)])
        return o_t / l

    def finish(outs):
        o_ref[...] = jnp.concatenate(outs, axis=0).T.astype(o_ref.dtype)

    @pl.when(jj == 0)
    def _ctx_queries():
        finish([attend([_dot(kc_ref[...], head_q(hh))], [vtc_ref], hh) for hh in heads])

    @pl.when(jj > 0)
    def _latent_queries():
        k_refs = (k0_ref, k1_ref, k2_ref)
        s_all = []
        for hh in heads:
            qh = head_q(hh)
            s_t = [_dot(k_refs[i][...], qh) + bias_ref[hh, NA_QB * i:NA_QB * (i + 1), :] for i in range(3)]
            s_all.append(s_t + [_dot(kc_ref[...], qh)])
        finish([attend(s_all[hh], (vt0_ref, vt1_ref, vt2_ref, vtc_ref), hh) for hh in heads])


def _na_attention(qt, k, vt, bias, n_batch, seq, n_ctx):
    t = k.shape[0]
    assert n_ctx == NA_QB and seq % NA_QB == 0
    nblk = seq // NA_QB
    assert nblk >= 3
    hp = NA_HEADS // NA_STEP_HEADS
    width = NA_STEP_HEADS * NA_HEAD_DIM
    ctx_blk = n_batch * nblk

    def q_blk(b, j):
        return jnp.where(j == 0, ctx_blk + b, b * nblk + j - 1)

    def win_blk(b, j, i):
        return b * nblk + jnp.clip(j - 2, 0, nblk - 3) + i

    def bias_map(b, h, j):
        return (jnp.where(j <= 1, 0, jnp.where(j == nblk, 2, 1)), h, 0, 0)

    tok, feat = (NA_QB, width), (width, NA_QB)
    in_specs = ([pl.BlockSpec(feat, lambda b, h, j: (h, q_blk(b, j)))]
                + [pl.BlockSpec(tok, functools.partial(lambda i, b, h, j: (win_blk(b, j, i), h), i)) for i in range(3)]
                + [pl.BlockSpec(feat, functools.partial(lambda i, b, h, j: (h, win_blk(b, j, i)), i)) for i in range(3)]
                + [pl.BlockSpec(tok, lambda b, h, j: (ctx_blk + b, h)),
                   pl.BlockSpec(feat, lambda b, h, j: (h, ctx_blk + b)),
                   pl.BlockSpec((None, NA_STEP_HEADS, NA_KB, NA_QB), bias_map)])
    return pl.pallas_call(
        _na_kernel,
        grid=(n_batch, hp, nblk + 1),
        in_specs=in_specs,
        out_specs=pl.BlockSpec(tok, lambda b, h, j: (q_blk(b, j), h)),
        out_shape=jax.ShapeDtypeStruct((t, A_WIDTH), BF16),
        compiler_params=_params(("parallel", "parallel", "arbitrary")),
        name="na_attention",
    )(qt, k, k, k, vt, vt, vt, k, vt, bias)


def _seg_mean(x, avg):
    hi = x.astype(BF16)
    lo = (x - hi.astype(F32)).astype(BF16)
    return _dot(hi, avg) + _dot(lo, avg)


def _sg_kernel(ug_ref, ws_ref, bm_ref, ng_ref, avg_ref, o_ref):
    lane = lax.broadcasted_iota(jnp.int32, (1, LANES), 1)
    first = lane < SG_GROUP_DIM
    avg = avg_ref[...]
    for c in range(ROW_TILE // SG_CHUNK):
        rows = slice(SG_CHUNK * c, SG_CHUNK * (c + 1))
        u = _gelu_tanh(ug_ref[rows, 0:B_WIDTH])
        g = _gelu_tanh(ug_ref[rows, B_WIDTH:2 * B_WIDTH])
        d = g - _seg_mean(g, avg)
        var = _seg_mean(d * d, avg)
        gn = (d * lax.rsqrt(var + EPS) * ng_ref[...]).astype(BF16)
        parts = []
        for j in range(SG_GROUPS // 2):
            gj = gn[:, LANES * j:LANES * (j + 1)]
            parts.append(jnp.where(first, _dot(ws_ref[2 * j], gj), _dot(ws_ref[2 * j + 1], gj)))
        mixed = jnp.concatenate(parts, axis=1) + bm_ref[...]
        o_ref[rows, :] = (u * mixed).astype(o_ref.dtype)


def _spatial_gating(ug, ws, bmat, ng, avg):
    t = ug.shape[0]
    tm = ROW_TILE
    return pl.pallas_call(
        _sg_kernel,
        grid=(t // tm,),
        in_specs=[pl.BlockSpec((tm, 2 * B_WIDTH), lambda i: (i, 0)), _full(ws.shape), _full(bmat.shape),
                  _full(ng.shape), _full(avg.shape)],
        out_specs=pl.BlockSpec((tm, B_WIDTH), lambda i: (i, 0)),
        out_shape=jax.ShapeDtypeStruct((t, B_WIDTH), BF16),
        compiler_params=_params(("parallel",)),
        name="spatial_gating",
    )(ug, ws, bmat, ng, avg)


def _mixer_residual(x, gate_ref, a_refs, w_refs):
    y = functools.reduce(jnp.add, [_dot(a[...], w[...]) for a, w in zip(a_refs, w_refs)])
    return x + gate_ref[...] * y


def _ffn_kernel(n_chunks, n_in, n_lat_tiles, xl_ref, xc_ref, gate_m_ref, g_ref, sh_ref, sc_ref, gate_ref, *refs):
    a_refs, w_refs = refs[:n_in], refs[n_in:2 * n_in]
    wg_ref, wu_ref, wd_ref, o_ref = refs[2 * n_in:]
    x = _mixer_residual(_stream_tile(xl_ref, xc_ref, n_lat_tiles), gate_m_ref, a_refs, w_refs)
    h = _norm_mod(x, g_ref[...], sh_ref[...], sc_ref[...]).astype(BF16)
    fc = wg_ref.shape[1] // n_chunks
    acc = None
    for c in range(n_chunks):
        cs = slice(fc * c, fc * (c + 1))
        act = (_silu(_dot(h, wg_ref[:, cs])) * _dot(h, wu_ref[:, cs])).astype(BF16)
        part = _dot(act, wd_ref[cs, :])
        acc = part if acc is None else acc + part
    o_ref[...] = x + gate_ref[...] * acc


def _ffn(x_lat, x_ctx, gate_m, acts, weights, g, sh, sc, gate, wg, wu, wd, tiles_per_batch, n_batch):
    d = x_lat.shape[1]
    t = x_lat.shape[0] + x_ctx.shape[0]
    tm = ROW_TILE
    f = wg.shape[1]
    n_chunks = 2 if f % (2 * LANES) == 0 else 1
    mod = _mod_spec(d, tiles_per_batch, n_batch)
    row = lambda i: (i, 0)
    resident = lambda w: pl.BlockSpec(w.shape, lambda i: (0, 0), pipeline_mode=pl.Buffered(1))
    return pl.pallas_call(
        functools.partial(_ffn_kernel, n_chunks, len(acts), x_lat.shape[0] // tm),
        grid=(t // tm,),
        in_specs=(_stream_specs(x_lat, x_ctx) + [mod, _full((1, d)), mod, mod, mod]
                  + [pl.BlockSpec((tm, a.shape[1]), row) for a in acts] + [resident(w) for w in weights]
                  + [resident(wg), resident(wu), resident(wd)]),
        out_specs=pl.BlockSpec((tm, d), row),
        out_shape=jax.ShapeDtypeStruct((t, d), F32),
        compiler_params=_params(("parallel",)),
        name="ffn",
    )(x_lat, x_ctx, gate_m, g, sh, sc, gate, *acts, *weights, wg, wu, wd)


def _rope_tables(n_batch, seq, n_ctx):
    t = np.arange(seq)
    pos = np.stack([t // GRID_W, t % GRID_W], axis=-1).astype(np.float32)
    n_freq = MLA_ROPE // 4
    inv = jnp.power(ROPE_BASE, -jnp.arange(n_freq, dtype=F32) / n_freq)
    ang = jnp.asarray(pos)[:, :, None] * inv
    cos, sin = jnp.cos(ang), jnp.sin(ang)
    cos_r = jnp.concatenate([cos, cos], axis=-1).reshape(seq, MLA_ROPE)
    sin_r = jnp.concatenate([-sin, sin], axis=-1).reshape(seq, MLA_ROPE)
    ones = jnp.ones((seq, MLA_NOPE), F32)
    pad = LANES - MLA_NOPE - MLA_ROPE
    cos_l = jnp.concatenate([ones, cos_r, jnp.ones((seq, pad), F32)], axis=-1)
    sin_l = jnp.concatenate([0 * ones, sin_r, jnp.zeros((seq, pad), F32)], axis=-1)
    cos_all = jnp.concatenate([jnp.tile(cos_l, (n_batch, 1)), jnp.ones((n_batch * n_ctx, LANES), F32)])
    sin_all = jnp.concatenate([jnp.tile(sin_l, (n_batch, 1)), jnp.zeros((n_batch * n_ctx, LANES), F32)])
    return cos_all, sin_all


def _swap_rope_cols(w_rope):
    q = MLA_ROPE // 4
    return jnp.concatenate([w_rope[..., q:2 * q], w_rope[..., 0:q], w_rope[..., 3 * q:4 * q],
                            w_rope[..., 2 * q:3 * q]], axis=-1)


def _mla_weights(w_in, w_uq, w_ukv):
    d = w_in.shape[0]
    pad = LANES - MLA_NOPE - MLA_ROPE
    w_kr = w_in[:, MLA_Q_RANK + MLA_KV_RANK:]
    lanes_kr = lambda w: jnp.concatenate([jnp.zeros((d, MLA_NOPE), F32), w, jnp.zeros((d, pad), F32)], axis=1)
    win = jnp.concatenate([w_in[:, :MLA_Q_RANK + MLA_KV_RANK], lanes_kr(w_kr), lanes_kr(_swap_rope_cols(w_kr))],
                          axis=1)
    uq = w_uq.reshape(MLA_Q_RANK, MLA_HEADS, MLA_NOPE + MLA_ROPE)
    zq = jnp.zeros((MLA_Q_RANK, MLA_HEADS, pad), F32)
    uq_a = jnp.concatenate([uq, zq], axis=-1).reshape(MLA_Q_RANK, MLA_HEADS * LANES)
    uq_b = jnp.concatenate([jnp.zeros((MLA_Q_RANK, MLA_HEADS, MLA_NOPE), F32),
                            _swap_rope_cols(uq[..., MLA_NOPE:]), zq], axis=-1).reshape(MLA_Q_RANK, MLA_HEADS * LANES)
    wuq_t = jnp.concatenate([uq_a, uq_b], axis=1).T
    ukv = w_ukv.reshape(MLA_KV_RANK, MLA_HEADS, MLA_NOPE + MLA_V)
    zk = jnp.zeros((MLA_KV_RANK, MLA_HEADS, LANES - MLA_NOPE), F32)
    uk = jnp.concatenate([ukv[..., :MLA_NOPE], zk], axis=-1).reshape(MLA_KV_RANK, MLA_HEADS * LANES)
    uv_t = jnp.transpose(ukv[..., MLA_NOPE:], (1, 2, 0))
    uv_t = jnp.concatenate([uv_t, jnp.zeros((MLA_HEADS, MLA_VROWS - MLA_V, MLA_KV_RANK), F32)], axis=1)
    return (win.astype(BF16), wuq_t.astype(BF16), uk.astype(BF16),
            uv_t.reshape(MLA_HEADS * MLA_VROWS, MLA_KV_RANK).astype(BF16))


def _mla_pre_kernel(x_ref, g_ref, sh_ref, sc_ref, win_ref, qg_ref, kvg_ref, wuqt_ref, wuk_ref, wuvt_ref,
                    cos_ref, sin_ref, cost_ref, sint_ref, qt_ref, k_ref, vt_ref):
    h = _norm_mod(x_ref[...], g_ref[...], sh_ref[...], sc_ref[...]).astype(BF16)
    p = _dot(h, win_ref[...])
    cq = _rms(p[:, :MLA_Q_RANK], qg_ref[...])
    ckv = _rms(p[:, MLA_Q_RANK:MLA_Q_RANK + MLA_KV_RANK], kvg_ref[...])
    cq_t = cq.T.astype(BF16)
    ckv_t = ckv.T.astype(BF16)
    ckv_b = ckv.astype(BF16)
    o_kr = MLA_Q_RANK + MLA_KV_RANK
    kr_rot = p[:, o_kr:o_kr + LANES] * cos_ref[...] + p[:, o_kr + LANES:o_kr + 2 * LANES] * sin_ref[...]
    kr2 = jnp.concatenate([kr_rot, kr_rot], axis=1)
    cos_t2 = jnp.concatenate([cost_ref[...], cost_ref[...]], axis=0)
    sin_t2 = jnp.concatenate([sint_ref[...], sint_ref[...]], axis=0)
    half = MLA_HEADS * LANES
    for c in range(MLA_HEADS // 2):
        rs = slice(2 * LANES * c, 2 * LANES * (c + 1))
        rs_b = slice(half + 2 * LANES * c, half + 2 * LANES * (c + 1))
        q_t = _dot(wuqt_ref[rs, :], cq_t) * cos_t2 + _dot(wuqt_ref[rs_b, :], cq_t) * sin_t2
        qt_ref[rs, :] = (q_t * (MLA_SCALE * LOG2E)).astype(qt_ref.dtype)
        k_ref[:, rs] = (_dot(ckv_b, wuk_ref[:, rs]) + kr2).astype(k_ref.dtype)
    group = 4 * MLA_VROWS
    row = lax.broadcasted_iota(jnp.int32, (group, 1), 0)
    ones_rows = functools.reduce(jnp.add, [jnp.where(row == MLA_VROWS * j + MLA_V, 1.0, 0.0) for j in range(4)])
    for c in range(MLA_HEADS // 4):
        rv = slice(group * c, group * (c + 1))
        vt_ref[rv, :] = (_dot(wuvt_ref[rv, :], ckv_t) + ones_rows).astype(vt_ref.dtype)


def _mla_pre(x, g, sh, sc, win, qg, kvg, wuq_t, wuk, wuv_t, cos, sin, tiles_per_batch, n_batch):
    t, d = x.shape
    tm = ROW_TILE
    n = MLA_HEADS * LANES
    nv = MLA_HEADS * MLA_VROWS
    mod = _mod_spec(d, tiles_per_batch, n_batch)
    row = lambda i: (i, 0)
    col = lambda i: (0, i)
    return pl.pallas_call(
        _mla_pre_kernel,
        grid=(t // tm,),
        in_specs=[pl.BlockSpec((tm, d), row), _full((1, d)), mod, mod, _full(win.shape), _full(qg.shape),
                  _full(kvg.shape), _full(wuq_t.shape), _full(wuk.shape), _full(wuv_t.shape),
                  pl.BlockSpec((tm, LANES), row), pl.BlockSpec((tm, LANES), row),
                  pl.BlockSpec((LANES, tm), col), pl.BlockSpec((LANES, tm), col)],
        out_specs=[pl.BlockSpec((None, n, tm), lambda i: (i, 0, 0)), pl.BlockSpec((tm, n), row),
                   pl.BlockSpec((None, nv, tm), lambda i: (i, 0, 0))],
        out_shape=[jax.ShapeDtypeStruct((t // tm, n, tm), BF16), jax.ShapeDtypeStruct((t, n), BF16),
                   jax.ShapeDtypeStruct((t // tm, nv, tm), BF16)],
        compiler_params=_params(("parallel",)),
        name="mla_pre",
    )(x, g, sh, sc, win, qg, kvg, wuq_t, wuk, wuv_t, cos, sin, cos.T, sin.T)


def _flash_kernel(qt_ref, k_ref, vt_ref, kc_ref, vtc_ref, o_ref, sa0_ref, sa1_ref, sb0_ref, sb1_ref):
    tq = qt_ref.shape[1]
    n_blocks, _, tk = vt_ref.shape
    n_ctx = kc_ref.shape[0]
    heads = range(2)
    s_a, s_b = (sa0_ref, sa1_ref), (sb0_ref, sb1_ref)
    lanes = lambda hh: slice(LANES * hh, LANES * (hh + 1))
    vrows = lambda hh: slice(MLA_VROWS * hh, MLA_VROWS * (hh + 1))

    def scores(kb_of, bufs, rows):
        cmax = []
        for hh in heads:
            s_t = _dot(kb_of(hh), qt_ref[lanes(hh), :])
            bufs[hh][0:rows, :] = s_t
            cmax.append(_col_max(s_t))
        return tuple(cmax)

    def absorb(bufs, rows, cmax, vt_of, state):
        new = []
        for hh in heads:
            m, acc = state[hh]
            m_new = jnp.maximum(m, cmax[hh])
            p_t = jnp.exp2(bufs[hh][0:rows, :] - m_new).astype(BF16)
            new.append((m_new, acc * jnp.exp2(m - m_new) + _dot(vt_of(hh), p_t)))
        return tuple(new)

    def lat_k(blk):
        off = blk * tk if isinstance(blk, int) else pl.multiple_of(blk * tk, tk)
        return lambda hh: k_ref[pl.ds(off, tk), lanes(hh)]

    lat_v = lambda blk: (lambda hh: vt_ref[blk, vrows(hh), :])
    ctx_k = lambda hh: kc_ref[:, lanes(hh)]
    ctx_v = lambda hh: vtc_ref[vrows(hh), :]

    def pair(blk, cmax_a, state, last):
        cmax_b = scores(lat_k(blk + 1), s_b, tk)
        state = absorb(s_a, tk, cmax_a, lat_v(blk), state)
        cmax_a = scores(ctx_k, s_a, n_ctx) if last else scores(lat_k(blk + 2), s_a, tk)
        return cmax_a, absorb(s_b, tk, cmax_b, lat_v(blk + 1), state)

    def two_pairs(j, carry):
        return pair(4 * j + 2, *pair(4 * j, *carry, last=False), last=False)

    init = tuple((jnp.full((1, tq), NEG, F32), jnp.zeros((MLA_VROWS, tq), F32)) for _ in heads)
    carry = (scores(lat_k(0), s_a, tk), init)
    inner_pairs = n_blocks // 2 - 1
    carry = lax.fori_loop(0, inner_pairs // 2, two_pairs, carry)
    if inner_pairs % 2:
        carry = pair(n_blocks - 4, *carry, last=False)
    cmax_a, state = pair(n_blocks - 2, *carry, last=True)
    state = absorb(s_a, n_ctx, cmax_a, ctx_v, state)
    outs = [acc[:MLA_V] / acc[MLA_V:MLA_V + 1] for _, acc in state]
    o_ref[...] = jnp.concatenate(outs, axis=0).T.astype(o_ref.dtype)


def _mla_attention(qt, k, vt, vtc, n_batch, seq, n_ctx):
    tq = ROW_TILE
    nq = seq // tq
    ctx0 = n_batch * seq // n_ctx
    return pl.pallas_call(
        _flash_kernel,
        grid=(n_batch, MLA_HEADS // 2, nq),
        in_specs=[pl.BlockSpec((None, 2 * LANES, tq), lambda b, h, i: (b * nq + i, h, 0)),
                  pl.BlockSpec((seq, 2 * LANES), lambda b, h, i: (b, h)),
                  pl.BlockSpec((nq, 2 * MLA_VROWS, tq), lambda b, h, i: (b, h, 0)),
                  pl.BlockSpec((n_ctx, 2 * LANES), lambda b, h, i: (ctx0 + b, h)),
                  pl.BlockSpec((None, 2 * MLA_VROWS, n_ctx), lambda b, h, i: (b, h, 0))],
        out_specs=pl.BlockSpec((tq, LANES), lambda b, h, i: (b * nq + i, h)),
        out_shape=jax.ShapeDtypeStruct((n_batch * seq, MLA_HEADS * MLA_V), BF16),
        scratch_shapes=[pltpu.VMEM((tq, tq), F32)] * 4,
        compiler_params=_params(("parallel", "parallel", "arbitrary")),
        name="mla_flash",
    )(qt, k, vt, k, vtc)


def _router_kernel(x_ref, gate_m_ref, o_ref, wo_ref, g_ref, sh_ref, sc_ref, wr_ref, br_ref, tri_ref,
                   x1_ref, h_ref, mi_ref, mf_ref, cnt_ref, run_ref):
    @pl.when(pl.program_id(0) == 0)
    def _init():
        run_ref[...] = jnp.zeros_like(run_ref)

    x1 = _mixer_residual(x_ref[...], gate_m_ref, [o_ref], [wo_ref])
    x1_ref[...] = x1
    h = _norm_mod(x1, g_ref[...], sh_ref[...], sc_ref[...])
    h_ref[...] = h
    h_hi = h.astype(BF16)
    h_lo = (h - h_hi.astype(F32)).astype(BF16)
    hw = _dot(h_hi, wr_ref[...])
    logits = hw[:, :LANES] + hw[:, LANES:] + _dot(h_lo, wr_ref[:, :LANES]) + br_ref[...]
    lane = lax.broadcasted_iota(jnp.int32, logits.shape, 1).astype(F32)
    m1 = jnp.max(logits, axis=-1, keepdims=True)
    i1 = jnp.min(jnp.where(logits == m1, lane, float(LANES)), axis=-1, keepdims=True)
    rest = jnp.where(lane == i1, NEG, logits)
    m2 = jnp.max(rest, axis=-1, keepdims=True)
    i2 = jnp.min(jnp.where(rest == m2, lane, float(LANES)), axis=-1, keepdims=True)
    e = jnp.exp(m2 - m1)
    w1 = 1.0 / (1.0 + e)
    w2 = e / (1.0 + e)
    hit1, hit2 = lane == i1, lane == i2
    onehot = jnp.where(jnp.logical_or(hit1, hit2), 1.0, 0.0)
    before = _dot(tri_ref[...], onehot.astype(BF16)) + run_ref[0:1, :]
    r1 = jnp.sum(jnp.where(hit1, before, 0.0), axis=-1, keepdims=True)
    r2 = jnp.sum(jnp.where(hit2, before, 0.0), axis=-1, keepdims=True)
    run_ref[...] = run_ref[...] + jnp.sum(onehot, axis=0, keepdims=True)
    meta = jnp.where(lane == 0.0, i1, jnp.where(lane == 1.0, i2, jnp.where(lane == 2.0, r1, jnp.where(lane == 3.0, r2, 0.0))))
    mi_ref[...] = meta.astype(jnp.int32)
    mf_ref[...] = jnp.where(lane == 0.0, w1, jnp.where(lane == 1.0, w2, 0.0))
    cnt_ref[...] = run_ref[...]


def _router(x, gate_m, o, wo, g, sh, sc, wr, br, tri, n_rows, tiles_per_batch, n_batch):
    d = x.shape[1]
    tm = ROW_TILE
    mod = _mod_spec(d, tiles_per_batch, n_batch)
    row = lambda i: (i, 0)
    return pl.pallas_call(
        _router_kernel,
        grid=(n_rows // tm,),
        in_specs=[pl.BlockSpec((tm, d), row), mod, pl.BlockSpec((tm, o.shape[1]), row), _full(wo.shape),
                  _full((1, d)), mod, mod, _full(wr.shape), _full(br.shape), _full(tri.shape)],
        out_specs=[pl.BlockSpec((tm, d), row), pl.BlockSpec((tm, d), row), pl.BlockSpec((tm, LANES), row),
                   pl.BlockSpec((tm, LANES), row), _full((8, LANES))],
        out_shape=[jax.ShapeDtypeStruct((n_rows, d), F32), jax.ShapeDtypeStruct((n_rows, d), F32),
                   jax.ShapeDtypeStruct((n_rows, LANES), jnp.int32), jax.ShapeDtypeStruct((n_rows, LANES), F32),
                   jax.ShapeDtypeStruct((8, LANES), F32)],
        scratch_shapes=[pltpu.VMEM((8, LANES), F32)],
        compiler_params=_params(("arbitrary",)),
        name="moe_router",
    )(x, gate_m, o, wo, g, sh, sc, wr, br, tri)


def _slot_owner_kernel(slot_ref, owner_ref):
    def clear(j, carry):
        owner_ref[j] = 0
        return carry

    def claim(j, carry):
        owner_ref[slot_ref[j]] = j
        return carry

    lax.fori_loop(0, owner_ref.shape[0], clear, 0, unroll=8)
    lax.fori_loop(0, slot_ref.shape[0], claim, 0, unroll=8)


def _slot_owner(slots, n_slots):
    smem = pl.BlockSpec(memory_space=pltpu.SMEM)
    return pl.pallas_call(
        _slot_owner_kernel,
        in_specs=[smem],
        out_specs=smem,
        out_shape=jax.ShapeDtypeStruct((n_slots,), jnp.int32),
        name="moe_slot_owner",
    )(slots)


def _moe_kernel(te_ref, src_ref, dst_ref, rows_ref, nv_ref, h_hbm, wg_ref, wu_ref, wd_ref, y_hbm, hbuf, ybuf,
                sem_g, sem_s):
    i, f = pl.program_id(0), pl.program_id(1)
    _, tm, _ = hbuf.shape
    half = tm // MOE_FCHUNKS
    nv = nv_ref[0]
    valid = i < nv
    slot = lax.rem(i, 2)
    other = 1 - slot

    def gather_row(idx, s, r):
        return pltpu.make_async_copy(h_hbm.at[pl.ds(idx, 1), :], hbuf.at[s, pl.ds(r, 1), :], sem_g.at[s])

    def scatter_row(idx, s, r):
        return pltpu.make_async_copy(ybuf.at[s, pl.ds(r, 1), :], y_hbm.at[pl.ds(idx, 1), :], sem_s.at[s])

    def wait_gather(s):
        pltpu.make_async_copy(h_hbm.at[pl.ds(0, tm), :], hbuf.at[s], sem_g.at[s]).wait()

    def wait_scatter(s, n):
        n8 = pl.multiple_of((n // 8) * 8, 8)

        @pl.when(n8 > 0)
        def _():
            pltpu.make_async_copy(ybuf.at[s, pl.ds(0, n8), :], y_hbm.at[pl.ds(0, n8), :], sem_s.at[s]).wait()

        def one_row(j, carry):
            scatter_row(0, s, 0).wait()
            return carry

        lax.fori_loop(0, n - n8, one_row, 0)

    @pl.when(jnp.logical_and(i == 0, f == 0))
    def _prologue():
        ybuf[...] = jnp.zeros_like(ybuf)

        def issue(j, carry):
            gather_row(src_ref[j], 0, j).start()
            return carry

        lax.fori_loop(0, tm, issue, 0)

    @pl.when(jnp.logical_and(f == 0, i <= nv))
    def _rows_ready():
        wait_gather(slot)

    @pl.when(jnp.logical_and(f == 0, jnp.logical_and(i >= 1, i <= nv)))
    def _slot_free():
        wait_scatter(slot, rows_ref[jnp.maximum(i - 1, 0)])

    @pl.when(valid)
    def _compute():
        hb = hbuf[slot].astype(BF16)
        r0 = f * half
        n_prev = rows_ref[i]
        for j in range(half):
            gather_row(src_ref[(i + 1) * tm + r0 + j], other, r0 + j).start()

            @pl.when(r0 + j < n_prev)
            def _():
                scatter_row(dst_ref[i * tm + r0 + j], other, r0 + j).start()
        act = (_silu(_dot(hb, wg_ref[...])) * _dot(hb, wu_ref[...])).astype(BF16)
        part = _dot(act, wd_ref[...])
        ybuf[slot] = jnp.where(f == 0, part, ybuf[slot] + part)

    @pl.when(jnp.logical_and(i == nv, f == 0))
    def _flush():
        n_last = rows_ref[i]

        def issue(j, carry):
            scatter_row(dst_ref[i * tm + j], other, j).start()
            return carry

        lax.fori_loop(0, n_last, issue, 0)
        wait_scatter(other, n_last)


def _moe_experts(tile_expert, src, dst, rows, n_valid, h, wg, wu, wd, n_tiles, y_rows):
    d = h.shape[1]
    tm = MOE_TILE
    nf = MOE_FCHUNKS
    fc = wg.shape[2] // nf

    def f_idx(i, f, nv):
        return jnp.where(i < nv[0], f, nf - 1)

    grid_spec = pltpu.PrefetchScalarGridSpec(
        num_scalar_prefetch=5,
        grid=(n_tiles + 1, nf),
        in_specs=[pl.BlockSpec(memory_space=pl.ANY),
                  pl.BlockSpec((None, d, fc), lambda i, f, te, src, dst, rows, nv: (te[i], 0, f_idx(i, f, nv))),
                  pl.BlockSpec((None, d, fc), lambda i, f, te, src, dst, rows, nv: (te[i], 0, f_idx(i, f, nv))),
                  pl.BlockSpec((None, fc, d), lambda i, f, te, src, dst, rows, nv: (te[i], f_idx(i, f, nv), 0))],
        out_specs=pl.BlockSpec(memory_space=pl.ANY),
        scratch_shapes=[pltpu.VMEM((2, tm, d), F32), pltpu.VMEM((2, tm, d), F32),
                        pltpu.SemaphoreType.DMA((2,)), pltpu.SemaphoreType.DMA((2,))],
    )
    return pl.pallas_call(
        _moe_kernel,
        grid_spec=grid_spec,
        out_shape=jax.ShapeDtypeStruct((y_rows, d), F32),
        compiler_params=_params(("arbitrary", "arbitrary")),
        name="moe_experts",
    )(tile_expert, src, dst, rows, n_valid, h, wg, wu, wd)


def _combine_kernel(x_ref, gate_ref, mf_ref, fg_ref, y1_ref, y2_ref, o_ref):
    w = mf_ref[...]
    y = w[:, 0:1] * y1_ref[...] + w[:, 1:2] * y2_ref[...]
    o_ref[...] = _rms(x_ref[...] + gate_ref[...] * y, fg_ref[...])


def _moe_combine(x, gate, mf, fg, y, tiles_per_batch, n_batch):
    t, d = x.shape
    tm = ROW_TILE
    row = lambda i: (i, 0)
    return pl.pallas_call(
        _combine_kernel,
        grid=(t // tm,),
        in_specs=[pl.BlockSpec((tm, d), row), _mod_spec(d, tiles_per_batch, n_batch), pl.BlockSpec((tm, LANES), row),
                  _full((1, d)), pl.BlockSpec((tm, d), row), pl.BlockSpec((tm, d), lambda i: (t // tm + i, 0))],
        out_specs=pl.BlockSpec((tm, d), row),
        out_shape=jax.ShapeDtypeStruct((t, d), F32),
        compiler_params=_params(("parallel",)),
        name="moe_combine",
    )(x, gate, mf, fg, y, y)


def kernel(x, c, ctx, c_ctx, ada_w, ada_b, norm_g, final_g, na_w_in, na_rpb, sg_w, sg_b, sg_norm_g, even_w_out,
           ffn_w_gate, ffn_w_up, ffn_w_down, mla_w_in, mla_q_norm_g, mla_kv_norm_g, mla_w_uq, mla_w_ukv, mla_w_out,
           moe_w_router, moe_b_router, moe_w_gate, moe_w_up, moe_w_down):
    n_batch, seq, d = x.shape
    n_ctx = ctx.shape[1]
    n_lat = n_batch * seq
    assert ada_w.shape[0] == 2 and seq % ROW_TILE == 0 and n_batch * n_ctx == ROW_TILE
    assert n_batch + 1 <= 8 and seq % GRID_W == 0
    tpb = seq // ROW_TILE

    cond8 = jnp.concatenate([c, c_ctx[None, :], jnp.zeros((8 - n_batch - 1, d), F32)], axis=0)
    mods = _ada_mod(cond8, ada_w, ada_b)[:, :n_batch + 1]
    mod = lambda layer, k: mods[layer, :, k * d:(k + 1) * d].reshape(n_batch + 1, 1, d)
    x_lat, x_ctx = x.reshape(n_lat, d), ctx.reshape(n_batch * n_ctx, d)

    w_in = na_w_in[0].astype(BF16)
    qt, k, vt, ug = _even_in(x_lat, x_ctx, norm_g[0, 0][None], mod(0, 0), mod(0, 1), w_in[:, :3 * A_WIDTH],
                             w_in[:, 3 * A_WIDTH:], tpb, n_batch)
    bias = _na_bias_table(na_rpb[0], seq // GRID_W)
    attn = _na_attention(qt, k, vt, bias, n_batch, seq, n_ctx)
    bmat = jnp.repeat(sg_b[0].T, SG_GROUP_DIM, axis=1)
    avg = jnp.asarray(np.kron(np.eye(SG_GROUPS), np.full((SG_GROUP_DIM, SG_GROUP_DIM), 1.0 / SG_GROUP_DIM)), BF16)
    gated = _spatial_gating(ug, sg_w[0].astype(BF16), bmat, sg_norm_g[0][None], avg)
    w_out = even_w_out[0].astype(BF16)
    xs = _ffn(x_lat, x_ctx, mod(0, 2), [attn, gated], [w_out[:A_WIDTH], w_out[A_WIDTH:]], norm_g[0, 1][None], mod(0, 3),
              mod(0, 4), mod(0, 5), ffn_w_gate[0].astype(BF16), ffn_w_up[0].astype(BF16), ffn_w_down[0].astype(BF16),
              tpb, n_batch)

    win, wuq_t, wuk, wuv_t = _mla_weights(mla_w_in[0], mla_w_uq[0], mla_w_ukv[0])
    cos, sin = _rope_tables(n_batch, seq, n_ctx)
    qt, k, vt = _mla_pre(xs, norm_g[1, 0][None], mod(1, 0), mod(1, 1), win, mla_q_norm_g[0][None],
                         mla_kv_norm_g[0][None], wuq_t, wuk, wuv_t, cos, sin, tpb, n_batch)
    vtc = jnp.transpose(vt[n_batch * tpb].reshape(-1, n_batch, n_ctx), (1, 0, 2))
    o = _mla_attention(qt, k, vt, vtc, n_batch, seq, n_ctx)
    wr = jnp.concatenate([moe_w_router[0], jnp.zeros((d, LANES - N_EXPERTS), F32)], axis=1)
    wr_hi = wr.astype(BF16)
    wr = jnp.concatenate([wr_hi, (wr - wr_hi.astype(F32)).astype(BF16)], axis=1)
    br = jnp.concatenate([moe_b_router[0], jnp.full((LANES - N_EXPERTS,), NEG, F32)])[None]
    tri = jnp.asarray(np.tril(np.ones((ROW_TILE, ROW_TILE), np.float32), -1), BF16)
    x1, h, mi, mf, cnt = _router(xs, mod(1, 2), o, mla_w_out[0].astype(BF16), norm_g[1, 1][None], mod(1, 3),
                                 mod(1, 4), wr, br, tri, n_lat, tpb, n_batch)

    counts = cnt[0, :N_EXPERTS].astype(jnp.int32)
    tiles_e = (counts + MOE_TILE - 1) // MOE_TILE
    tile_end = jnp.cumsum(tiles_e)
    start = (tile_end - tiles_e) * MOE_TILE
    slot12 = jnp.concatenate([start[mi[:, 0]] + mi[:, 2], start[mi[:, 1]] + mi[:, 3]])
    n_tiles = 2 * n_lat // MOE_TILE + N_EXPERTS
    n_slots = n_tiles * MOE_TILE
    dst = _slot_owner(slot12.astype(jnp.int32), n_slots)
    spare_tile = jnp.zeros((MOE_TILE,), jnp.int32)
    src = jnp.concatenate([dst % n_lat, spare_tile])
    dst = jnp.concatenate([spare_tile, dst])
    n_valid = tile_end[-1:]
    last_tile = jnp.maximum(n_valid[0] - 1, 0)
    all_tiles = jnp.arange(n_tiles + 1, dtype=jnp.int32)
    tile_ids = jnp.minimum(all_tiles, last_tile)
    tile_expert = jnp.minimum(jnp.sum((tile_ids[:, None] >= tile_end[None, :]).astype(jnp.int32), axis=1),
                              N_EXPERTS - 1)
    filled = counts[tile_expert] - (all_tiles - (tile_end - tiles_e)[tile_expert]) * MOE_TILE
    rows = jnp.where(all_tiles < n_valid[0], jnp.clip(filled, 0, MOE_TILE), 0)
    rows = jnp.concatenate([jnp.zeros((1,), jnp.int32), rows]).astype(jnp.int32)
    y = _moe_experts(tile_expert, src, dst, rows, n_valid.astype(jnp.int32), h, moe_w_gate[0].astype(BF16),
                     moe_w_up[0].astype(BF16), moe_w_down[0].astype(BF16), n_tiles, 2 * n_lat)
    out = _moe_combine(x1, mod(1, 5), mf, final_g[None], y, tpb, n_batch)
    return out.reshape(n_batch, seq, d)
```

```python
import functools

import numpy as np
import jax
import jax.numpy as jnp
from jax import lax
from jax.experimental import pallas as pl
from jax.experimental.pallas import tpu as pltpu

F32 = jnp.float32
BF16 = jnp.bfloat16
EPS = 1e-6
NEG = -1e30

LANES = 128
VMEM_LIMIT_BYTES = 56 * 1024 * 1024

GRID_W = 64
NA_HEADS = 8
NA_HEAD_DIM = 64
NA_WIN_R = 8
NA_WIN_C = 16
SG_GROUPS = 8
SG_GROUP_DIM = 64
SG_CHUNK = 128
A_WIDTH = NA_HEADS * NA_HEAD_DIM
B_WIDTH = SG_GROUPS * SG_GROUP_DIM
MLA_HEADS = 16
MLA_NOPE = 64
MLA_ROPE = 32
MLA_V = 64
MLA_Q_RANK = 384
MLA_KV_RANK = 256
MLA_SCALE = (MLA_NOPE + MLA_ROPE) ** -0.5
ROPE_BASE = 10000.0
N_EXPERTS = 8

ROW_TILE = 512
NA_QROWS = 4
NA_QB = NA_QROWS * GRID_W
NA_KB = 3 * NA_QB
NA_STEP_HEADS = 4
MLA_VROWS = 80
FLASH_QTILES = 4
LOG2E = 1.4426950408889634
MOE_TILE = 512
MOE_FCHUNKS = 2


def _params(sem):
    return pltpu.CompilerParams(dimension_semantics=sem, vmem_limit_bytes=VMEM_LIMIT_BYTES)


def _dot(a, b):
    return jnp.dot(a, b, preferred_element_type=F32)


def _dot_nt(a, b):
    return lax.dot_general(a, b, (((1,), (1,)), ((), ())), preferred_element_type=F32)


def _silu(x):
    return x / (1.0 + jnp.exp(-x))


def _gelu_tanh(x):
    return 0.5 * x * (1.0 + jnp.tanh(0.7978845608028654 * (x + 0.044715 * (x * x * x))))


def _rms(x, g):
    return x * lax.rsqrt(jnp.mean(x * x, axis=-1, keepdims=True) + EPS) * g


def _norm_mod(x, g, sh, sc):
    return _rms(x, g) * (1.0 + sc) + sh


def _full(shape):
    n = len(shape)
    return pl.BlockSpec(shape, lambda *_: (0,) * n)


def _ada_kernel(cond_ref, w_ref, b_ref, o_ref):
    c = cond_ref[...]
    o_ref[...] = jnp.dot(_silu(c), w_ref[...], preferred_element_type=F32,
                         precision=lax.Precision.HIGHEST) + b_ref[...]


def _ada_mod(cond8, ada_w, ada_b):
    depth, d, n = ada_w.shape
    tn = n // 4
    return pl.pallas_call(
        _ada_kernel,
        grid=(depth, n // tn),
        in_specs=[_full((8, d)),
                  pl.BlockSpec((None, d, tn), lambda l, j: (l, 0, j)),
                  pl.BlockSpec((None, 1, tn), lambda l, j: (l, 0, j))],
        out_specs=pl.BlockSpec((None, 8, tn), lambda l, j: (l, 0, j)),
        out_shape=jax.ShapeDtypeStruct((depth, 8, n), F32),
        compiler_params=_params(("parallel", "parallel")),
        name="ada_mod",
    )(cond8, ada_w, ada_b.reshape(depth, 1, n))


def _group_map(tiles_per_batch, n_batch):
    return lambda t: (jnp.minimum(t // tiles_per_batch, n_batch), 0, 0)


def _mod_spec(d, tiles_per_batch, n_batch):
    return pl.BlockSpec((None, 1, d), _group_map(tiles_per_batch, n_batch))


def _stream_specs(x_lat, x_ctx):
    tm, d = ROW_TILE, x_lat.shape[1]
    assert x_ctx.shape[0] == tm
    last = x_lat.shape[0] // tm - 1
    return [pl.BlockSpec((tm, d), lambda i: (jnp.minimum(i, last), 0)), pl.BlockSpec((tm, d), lambda i: (0, 0))]


def _stream_tile(xl_ref, xc_ref, n_lat_tiles):
    return jnp.where(pl.program_id(0) < n_lat_tiles, xl_ref[...], xc_ref[...])


def _even_in_kernel(n_lat_tiles, xl_ref, xc_ref, g_ref, sh_ref, sc_ref, wqkv_ref, wug_ref, qt_ref, k_ref, vt_ref,
                    ug_ref):
    x = _stream_tile(xl_ref, xc_ref, n_lat_tiles)
    h = _norm_mod(x, g_ref[...], sh_ref[...], sc_ref[...]).astype(BF16)
    qkv = _dot(h, wqkv_ref[...])
    qt_ref[...] = (qkv[:, :A_WIDTH] * (NA_HEAD_DIM ** -0.5 * LOG2E)).T.astype(qt_ref.dtype)
    k_ref[...] = qkv[:, A_WIDTH:2 * A_WIDTH].astype(k_ref.dtype)
    vt_ref[...] = qkv[:, 2 * A_WIDTH:].T.astype(vt_ref.dtype)
    ug_ref[...] = _dot(h, wug_ref[...])


def _even_in(x_lat, x_ctx, g, sh, sc, wqkv, wug, tiles_per_batch, n_batch):
    d = x_lat.shape[1]
    t = x_lat.shape[0] + x_ctx.shape[0]
    tm = ROW_TILE
    mod = _mod_spec(d, tiles_per_batch, n_batch)
    return pl.pallas_call(
        functools.partial(_even_in_kernel, x_lat.shape[0] // tm),
        grid=(t // tm,),
        in_specs=_stream_specs(x_lat, x_ctx) + [_full((1, d)), mod, mod, _full(wqkv.shape), _full(wug.shape)],
        out_specs=[pl.BlockSpec((A_WIDTH, tm), lambda i: (0, i)), pl.BlockSpec((tm, A_WIDTH), lambda i: (i, 0)),
                   pl.BlockSpec((A_WIDTH, tm), lambda i: (0, i)), pl.BlockSpec((tm, wug.shape[1]), lambda i: (i, 0))],
        out_shape=[jax.ShapeDtypeStruct((A_WIDTH, t), BF16), jax.ShapeDtypeStruct((t, A_WIDTH), BF16),
                   jax.ShapeDtypeStruct((A_WIDTH, t), BF16), jax.ShapeDtypeStruct((t, wug.shape[1]), F32)],
        compiler_params=_params(("parallel",)),
        name="even_in",
    )(x_lat, x_ctx, g, sh, sc, wqkv, wug)


def _na_bias_table(rpb, rows):
    nblk = rows // NA_QROWS
    n_heads = rpb.shape[0]
    win_rows = NA_KB // GRID_W
    qc = np.arange(GRID_W)
    col_start = np.clip(qc - NA_WIN_C // 2, 0, GRID_W - NA_WIN_C)
    col_ok = (qc[None, :] >= col_start[:, None]) & (qc[None, :] < col_start[:, None] + NA_WIN_C)
    col_j = qc[None, :] - qc[:, None] + NA_WIN_C - 1
    sel_c = ((col_j[None] == np.arange(2 * NA_WIN_C - 1)[:, None, None]) & col_ok[None]).astype(np.float32)
    toeplitz = jnp.einsum("hij,jqk->hikq", rpb, sel_c, precision=lax.Precision.HIGHEST) * LOG2E
    toeplitz = jnp.where(col_ok.T[None, None], toeplitz, NEG)
    n_rel = 2 * NA_WIN_R - 1
    toeplitz = jnp.concatenate([toeplitz, jnp.full((n_heads, 1, GRID_W, GRID_W), NEG, F32)], axis=1)
    rel = np.full((3, NA_QROWS, win_rows), n_rel, np.int32)
    for kind, j in enumerate((0, 1, nblk - 1)):
        first_row = NA_QROWS * int(np.clip(j - 1, 0, nblk - 3))
        for a in range(NA_QROWS):
            r = NA_QROWS * j + a
            row_start = int(np.clip(r - NA_WIN_R // 2, 0, rows - NA_WIN_R))
            for b in range(win_rows):
                kr = first_row + b
                if row_start <= kr < row_start + NA_WIN_R:
                    rel[kind, a, b] = kr - r + NA_WIN_R - 1
    grid_spec = pltpu.PrefetchScalarGridSpec(
        num_scalar_prefetch=1,
        grid=(3, n_heads),
        in_specs=[pl.BlockSpec((None, n_rel + 1, GRID_W, GRID_W), lambda c, h, rel: (h, 0, 0, 0))],
        out_specs=pl.BlockSpec((None, None, NA_KB, NA_QB), lambda c, h, rel: (c, h, 0, 0)),
    )
    return pl.pallas_call(
        _na_bias_kernel,
        grid_spec=grid_spec,
        out_shape=jax.ShapeDtypeStruct((3, n_heads, NA_KB, NA_QB), F32),
        compiler_params=_params(("parallel", "parallel")),
        name="na_bias",
    )(jnp.asarray(rel.reshape(-1)), toeplitz)


def _na_bias_kernel(rel_ref, t_ref, o_ref):
    kind = pl.program_id(0)
    win_rows = NA_KB // GRID_W
    for b in range(win_rows):
        for ap in range(NA_QROWS // 2):
            rel = [rel_ref[(kind * NA_QROWS + 2 * ap + s) * win_rows + b] for s in range(2)]
            pair = jnp.concatenate([t_ref[rel[0]], t_ref[rel[1]]], axis=1)
            o_ref[GRID_W * b:GRID_W * (b + 1), 2 * GRID_W * ap:2 * GRID_W * (ap + 1)] = pair


def _col_reduce(x_t, combine, reduce):
    parts = [x_t[r:r + 64] for r in range(0, x_t.shape[0], 64)]
    while len(parts) > 1:
        parts = [combine(parts[j], parts[j + 1]) for j in range(0, len(parts) - 1, 2)] + parts[len(parts) & ~1:]
    return reduce(parts[0], axis=0, keepdims=True)


def _col_max(s_t):
    return _col_reduce(s_t, jnp.maximum, jnp.max)


def _col_sum(p_t):
    return _col_reduce(p_t, jnp.add, jnp.sum)


def _na_kernel(qt_ref, k0_ref, k1_ref, k2_ref, vt0_ref, vt1_ref, vt2_ref, kc_ref, vtc_ref, bias_ref, o_ref):
    jj = pl.program_id(2)
    row = lax.broadcasted_iota(jnp.int32, (NA_STEP_HEADS * NA_HEAD_DIM, 1), 0)
    qt = qt_ref[...]
    zero = jnp.zeros_like(qt)
    heads = range(NA_STEP_HEADS)

    def head_q(hh):
        mine = jnp.logical_and(row >= NA_HEAD_DIM * hh, row < NA_HEAD_DIM * (hh + 1))
        return jnp.where(mine, qt, zero)

    def attend(s_t, vt_refs, hh):
        m = functools.reduce(jnp.maximum, [_col_max(x) for x in s_t])
        p_t = [jnp.exp2(x - m) for x in s_t]
        l = functools.reduce(jnp.add, [_col_sum(x) for x in p_t])
        vrows = slice(NA_HEAD_DIM * hh, NA_HEAD_DIM * (hh + 1))
        o_t = functools.reduce(jnp.add, [_dot(vt[vrows, :], x.astype(BF16)) for vt, x in zip(vt_refs, p_t)])
        return o_t / l

    def finish(outs):
        o_ref[...] = jnp.concatenate(outs, axis=0).T.astype(o_ref.dtype)

    @pl.when(jj == 0)
    def _ctx_queries():
        finish([attend([_dot(kc_ref[...], head_q(hh))], [vtc_ref], hh) for hh in heads])

    @pl.when(jj > 0)
    def _latent_queries():
        k_refs = (k0_ref, k1_ref, k2_ref)
        s_all = []
        for hh in heads:
            qh = head_q(hh)
            s_t = [_dot(k_refs[i][...], qh) + bias_ref[hh, NA_QB * i:NA_QB * (i + 1), :] for i in range(3)]
            s_all.append(s_t + [_dot(kc_ref[...], qh)])
        finish([attend(s_all[hh], (vt0_ref, vt1_ref, vt2_ref, vtc_ref), hh) for hh in heads])


def _na_attention(qt, k, vt, bias, n_batch, seq, n_ctx):
    t = k.shape[0]
    assert n_ctx == NA_QB and seq % NA_QB == 0
    nblk = seq // NA_QB
    assert nblk >= 3
    hp = NA_HEADS // NA_STEP_HEADS
    width = NA_STEP_HEADS * NA_HEAD_DIM
    ctx_blk = n_batch * nblk

    def q_blk(b, j):
        return jnp.where(j == 0, ctx_blk + b, b * nblk + j - 1)

    def win_blk(b, j, i):
        return b * nblk + jnp.clip(j - 2, 0, nblk - 3) + i

    def bias_map(b, h, j):
        return (jnp.where(j <= 1, 0, jnp.where(j == nblk, 2, 1)), h, 0, 0)

    tok, feat = (NA_QB, width), (width, NA_QB)
    in_specs = ([pl.BlockSpec(feat, lambda b, h, j: (h, q_blk(b, j)))]
                + [pl.BlockSpec(tok, functools.partial(lambda i, b, h, j: (win_blk(b, j, i), h), i)) for i in range(3)]
                + [pl.BlockSpec(feat, functools.partial(lambda i, b, h, j: (h, win_blk(b, j, i)), i)) for i in range(3)]
                + [pl.BlockSpec(tok, lambda b, h, j: (ctx_blk + b, h)),
                   pl.BlockSpec(feat, lambda b, h, j: (h, ctx_blk + b)),
                   pl.BlockSpec((None, NA_STEP_HEADS, NA_KB, NA_QB), bias_map)])
    return pl.pallas_call(
        _na_kernel,
        grid=(n_batch, hp, nblk + 1),
        in_specs=in_specs,
        out_specs=pl.BlockSpec(tok, lambda b, h, j: (q_blk(b, j), h)),
        out_shape=jax.ShapeDtypeStruct((t, A_WIDTH), BF16),
        compiler_params=_params(("parallel", "parallel", "arbitrary")),
        name="na_attention",
    )(qt, k, k, k, vt, vt, vt, k, vt, bias)


def _seg_mean(x, avg):
    hi = x.astype(BF16)
    lo = (x - hi.astype(F32)).astype(BF16)
    return _dot(hi, avg) + _dot(lo, avg)


def _sg_kernel(ug_ref, ws_ref, bm_ref, ng_ref, avg_ref, o_ref):
    lane = lax.broadcasted_iota(jnp.int32, (1, LANES), 1)
    first = lane < SG_GROUP_DIM
    avg = avg_ref[...]
    g = _gelu_tanh(ug_ref[:, B_WIDTH:2 * B_WIDTH])
    d = g - _seg_mean(g, avg)
    var = _seg_mean(d * d, avg)
    gn = (d * lax.rsqrt(var + EPS) * ng_ref[...]).astype(BF16)
    for c in range(ROW_TILE // SG_CHUNK):
        rows = slice(SG_CHUNK * c, SG_CHUNK * (c + 1))
        parts = []
        for j in range(SG_GROUPS // 2):
            gj = gn[rows, LANES * j:LANES * (j + 1)]
            parts.append(jnp.where(first, _dot(ws_ref[2 * j], gj), _dot(ws_ref[2 * j + 1], gj)))
        mixed = jnp.concatenate(parts, axis=1) + bm_ref[...]
        o_ref[rows, :] = (_gelu_tanh(ug_ref[rows, 0:B_WIDTH]) * mixed).astype(o_ref.dtype)


def _spatial_gating(ug, ws, bmat, ng, avg):
    t = ug.shape[0]
    tm = ROW_TILE
    return pl.pallas_call(
        _sg_kernel,
        grid=(t // tm,),
        in_specs=[pl.BlockSpec((tm, 2 * B_WIDTH), lambda i: (i, 0)), _full(ws.shape), _full(bmat.shape),
                  _full(ng.shape), _full(avg.shape)],
        out_specs=pl.BlockSpec((tm, B_WIDTH), lambda i: (i, 0)),
        out_shape=jax.ShapeDtypeStruct((t, B_WIDTH), BF16),
        compiler_params=_params(("parallel",)),
        name="spatial_gating",
    )(ug, ws, bmat, ng, avg)


def _mixer_residual(x, gate_ref, a_refs, w_refs):
    y = functools.reduce(jnp.add, [_dot(a[...], w[...]) for a, w in zip(a_refs, w_refs)])
    return x + gate_ref[...] * y


def _ffn_kernel(n_chunks, n_in, n_lat_tiles, xl_ref, xc_ref, gate_m_ref, g_ref, sh_ref, sc_ref, gate_ref, *refs):
    a_refs, w_refs = refs[:n_in], refs[n_in:2 * n_in]
    wg_ref, wu_ref, wd_ref, o_ref = refs[2 * n_in:]
    x = _mixer_residual(_stream_tile(xl_ref, xc_ref, n_lat_tiles), gate_m_ref, a_refs, w_refs)
    h = _norm_mod(x, g_ref[...], sh_ref[...], sc_ref[...]).astype(BF16)
    fc = wg_ref.shape[1] // n_chunks
    acc = None
    for c in range(n_chunks):
        cs = slice(fc * c, fc * (c + 1))
        act = (_silu(_dot(h, wg_ref[:, cs])) * _dot(h, wu_ref[:, cs])).astype(BF16)
        part = _dot(act, wd_ref[cs, :])
        acc = part if acc is None else acc + part
    o_ref[...] = x + gate_ref[...] * acc


def _ffn(x_lat, x_ctx, gate_m, acts, weights, g, sh, sc, gate, wg, wu, wd, tiles_per_batch, n_batch):
    d = x_lat.shape[1]
    t = x_lat.shape[0] + x_ctx.shape[0]
    tm = ROW_TILE
    f = wg.shape[1]
    n_chunks = 2 if f % (2 * LANES) == 0 else 1
    mod = _mod_spec(d, tiles_per_batch, n_batch)
    row = lambda i: (i, 0)
    resident = lambda w: pl.BlockSpec(w.shape, lambda i: (0, 0), pipeline_mode=pl.Buffered(1))
    return pl.pallas_call(
        functools.partial(_ffn_kernel, n_chunks, len(acts), x_lat.shape[0] // tm),
        grid=(t // tm,),
        in_specs=(_stream_specs(x_lat, x_ctx) + [mod, _full((1, d)), mod, mod, mod]
                  + [pl.BlockSpec((tm, a.shape[1]), row) for a in acts] + [resident(w) for w in weights]
                  + [resident(wg), resident(wu), resident(wd)]),
        out_specs=pl.BlockSpec((tm, d), row),
        out_shape=jax.ShapeDtypeStruct((t, d), F32),
        compiler_params=_params(("parallel",)),
        name="ffn",
    )(x_lat, x_ctx, gate_m, g, sh, sc, gate, *acts, *weights, wg, wu, wd)


def _rope_tables(n_batch, seq, n_ctx):
    t = np.arange(seq)
    pos = np.stack([t // GRID_W, t % GRID_W], axis=-1).astype(np.float32)
    n_freq = MLA_ROPE // 4
    inv = jnp.power(ROPE_BASE, -jnp.arange(n_freq, dtype=F32) / n_freq)
    ang = jnp.asarray(pos)[:, :, None] * inv
    cos, sin = jnp.cos(ang), jnp.sin(ang)
    cos_r = jnp.concatenate([cos, cos], axis=-1).reshape(seq, MLA_ROPE)
    sin_r = jnp.concatenate([-sin, sin], axis=-1).reshape(seq, MLA_ROPE)
    ones = jnp.ones((seq, MLA_NOPE), F32)
    pad = LANES - MLA_NOPE - MLA_ROPE
    cos_l = jnp.concatenate([ones, cos_r, jnp.ones((seq, pad), F32)], axis=-1)
    sin_l = jnp.concatenate([0 * ones, sin_r, jnp.zeros((seq, pad), F32)], axis=-1)
    cos_all = jnp.concatenate([jnp.tile(cos_l, (n_batch, 1)), jnp.ones((n_batch * n_ctx, LANES), F32)])
    sin_all = jnp.concatenate([jnp.tile(sin_l, (n_batch, 1)), jnp.zeros((n_batch * n_ctx, LANES), F32)])
    return cos_all, sin_all


def _swap_rope_cols(w_rope):
    q = MLA_ROPE // 4
    return jnp.concatenate([w_rope[..., q:2 * q], w_rope[..., 0:q], w_rope[..., 3 * q:4 * q],
                            w_rope[..., 2 * q:3 * q]], axis=-1)


def _mla_weights(w_in, w_uq, w_ukv):
    d = w_in.shape[0]
    pad = LANES - MLA_NOPE - MLA_ROPE
    w_kr = w_in[:, MLA_Q_RANK + MLA_KV_RANK:]
    lanes_kr = lambda w: jnp.concatenate([jnp.zeros((d, MLA_NOPE), F32), w, jnp.zeros((d, pad), F32)], axis=1)
    win = jnp.concatenate([w_in[:, :MLA_Q_RANK + MLA_KV_RANK], lanes_kr(w_kr), lanes_kr(_swap_rope_cols(w_kr))],
                          axis=1)
    uq = w_uq.reshape(MLA_Q_RANK, MLA_HEADS, MLA_NOPE + MLA_ROPE)
    zq = jnp.zeros((MLA_Q_RANK, MLA_HEADS, pad), F32)
    uq_a = jnp.concatenate([uq, zq], axis=-1).reshape(MLA_Q_RANK, MLA_HEADS * LANES)
    uq_b = jnp.concatenate([jnp.zeros((MLA_Q_RANK, MLA_HEADS, MLA_NOPE), F32),
                            _swap_rope_cols(uq[..., MLA_NOPE:]), zq], axis=-1).reshape(MLA_Q_RANK, MLA_HEADS * LANES)
    wuq_t = jnp.concatenate([uq_a, uq_b], axis=1).T
    ukv = w_ukv.reshape(MLA_KV_RANK, MLA_HEADS, MLA_NOPE + MLA_V)
    zk = jnp.zeros((MLA_KV_RANK, MLA_HEADS, LANES - MLA_NOPE), F32)
    uk = jnp.concatenate([ukv[..., :MLA_NOPE], zk], axis=-1).reshape(MLA_KV_RANK, MLA_HEADS * LANES)
    uv_t = jnp.transpose(ukv[..., MLA_NOPE:], (1, 2, 0))
    uv_t = jnp.concatenate([uv_t, jnp.zeros((MLA_HEADS, MLA_VROWS - MLA_V, MLA_KV_RANK), F32)], axis=1)
    return (win.astype(BF16), wuq_t.astype(BF16), uk.astype(BF16),
            uv_t.reshape(MLA_HEADS * MLA_VROWS, MLA_KV_RANK).astype(BF16))


def _mla_pre_kernel(x_ref, g_ref, sh_ref, sc_ref, win_ref, qg_ref, kvg_ref, wuqt_ref, wuk_ref, wuvt_ref,
                    cos_ref, sin_ref, cost_ref, sint_ref, qt_ref, k_ref, vt_ref):
    h = _norm_mod(x_ref[...], g_ref[...], sh_ref[...], sc_ref[...]).astype(BF16)
    p = _dot(h, win_ref[...])
    cq = _rms(p[:, :MLA_Q_RANK], qg_ref[...])
    ckv = _rms(p[:, MLA_Q_RANK:MLA_Q_RANK + MLA_KV_RANK], kvg_ref[...])
    cq_t = cq.T.astype(BF16)
    ckv_t = ckv.T.astype(BF16)
    ckv_b = ckv.astype(BF16)
    o_kr = MLA_Q_RANK + MLA_KV_RANK
    kr_rot = p[:, o_kr:o_kr + LANES] * cos_ref[...] + p[:, o_kr + LANES:o_kr + 2 * LANES] * sin_ref[...]
    kr2 = jnp.concatenate([kr_rot, kr_rot], axis=1)
    cos_t2 = jnp.concatenate([cost_ref[...], cost_ref[...]], axis=0)
    sin_t2 = jnp.concatenate([sint_ref[...], sint_ref[...]], axis=0)
    half = MLA_HEADS * LANES
    for c in range(MLA_HEADS // 2):
        rs = slice(2 * LANES * c, 2 * LANES * (c + 1))
        rs_b = slice(half + 2 * LANES * c, half + 2 * LANES * (c + 1))
        q_t = _dot(wuqt_ref[rs, :], cq_t) * cos_t2 + _dot(wuqt_ref[rs_b, :], cq_t) * sin_t2
        qt_ref[rs, :] = (q_t * (MLA_SCALE * LOG2E)).astype(qt_ref.dtype)
        k_ref[:, rs] = (_dot(ckv_b, wuk_ref[:, rs]) + kr2).astype(k_ref.dtype)
    group = 4 * MLA_VROWS
    row = lax.broadcasted_iota(jnp.int32, (group, 1), 0)
    ones_rows = functools.reduce(jnp.add, [jnp.where(row == MLA_VROWS * j + MLA_V, 1.0, 0.0) for j in range(4)])
    for c in range(MLA_HEADS // 4):
        rv = slice(group * c, group * (c + 1))
        vt_ref[rv, :] = (_dot(wuvt_ref[rv, :], ckv_t) + ones_rows).astype(vt_ref.dtype)


def _mla_pre(x, g, sh, sc, win, qg, kvg, wuq_t, wuk, wuv_t, cos, sin, tiles_per_batch, n_batch):
    t, d = x.shape
    tm = ROW_TILE
    n = MLA_HEADS * LANES
    nv = MLA_HEADS * MLA_VROWS
    mod = _mod_spec(d, tiles_per_batch, n_batch)
    row = lambda i: (i, 0)
    col = lambda i: (0, i)
    return pl.pallas_call(
        _mla_pre_kernel,
        grid=(t // tm,),
        in_specs=[pl.BlockSpec((tm, d), row), _full((1, d)), mod, mod, _full(win.shape), _full(qg.shape),
                  _full(kvg.shape), _full(wuq_t.shape), _full(wuk.shape), _full(wuv_t.shape),
                  pl.BlockSpec((tm, LANES), row), pl.BlockSpec((tm, LANES), row),
                  pl.BlockSpec((LANES, tm), col), pl.BlockSpec((LANES, tm), col)],
        out_specs=[pl.BlockSpec((None, n, tm), lambda i: (i, 0, 0)), pl.BlockSpec((tm, n), row),
                   pl.BlockSpec((None, nv, tm), lambda i: (i, 0, 0))],
        out_shape=[jax.ShapeDtypeStruct((t // tm, n, tm), BF16), jax.ShapeDtypeStruct((t, n), BF16),
                   jax.ShapeDtypeStruct((t // tm, nv, tm), BF16)],
        compiler_params=_params(("parallel",)),
        name="mla_pre",
    )(x, g, sh, sc, win, qg, kvg, wuq_t, wuk, wuv_t, cos, sin, cos.T, sin.T)


def _flash_kernel(qts_ref, k_ref, vt_ref, kc_ref, vtc_ref, os_ref, *s_refs):
    for t in range(qts_ref.shape[0]):
        _flash_tile(qts_ref.at[t], k_ref, vt_ref, kc_ref, vtc_ref, os_ref.at[pl.ds(t * qts_ref.shape[2], qts_ref.shape[2])],
                    *s_refs[4 * t:4 * t + 4])


def _flash_tile(qt_ref, k_ref, vt_ref, kc_ref, vtc_ref, o_ref, sa0_ref, sa1_ref, sb0_ref, sb1_ref):
    tq = qt_ref.shape[1]
    n_blocks, _, tk = vt_ref.shape
    n_ctx = kc_ref.shape[0]
    heads = range(2)
    s_a, s_b = (sa0_ref, sa1_ref), (sb0_ref, sb1_ref)
    lanes = lambda hh: slice(LANES * hh, LANES * (hh + 1))
    vrows = lambda hh: slice(MLA_VROWS * hh, MLA_VROWS * (hh + 1))

    def scores(kb_of, bufs, rows):
        cmax = []
        for hh in heads:
            s_t = _dot(kb_of(hh), qt_ref[lanes(hh), :])
            bufs[hh][0:rows, :] = s_t
            cmax.append(_col_max(s_t))
        return tuple(cmax)

    def absorb(bufs, rows, cmax, vt_of, state):
        new = []
        for hh in heads:
            m, acc = state[hh]
            m_new = jnp.maximum(m, cmax[hh])
            p_t = jnp.exp2(bufs[hh][0:rows, :] - m_new).astype(BF16)
            new.append((m_new, acc * jnp.exp2(m - m_new) + _dot(vt_of(hh), p_t)))
        return tuple(new)

    def lat_k(blk):
        off = blk * tk if isinstance(blk, int) else pl.multiple_of(blk * tk, tk)
        return lambda hh: k_ref[pl.ds(off, tk), lanes(hh)]

    lat_v = lambda blk: (lambda hh: vt_ref[blk, vrows(hh), :])
    ctx_k = lambda hh: kc_ref[:, lanes(hh)]
    ctx_v = lambda hh: vtc_ref[vrows(hh), :]

    def pair(blk, cmax_a, state, last):
        cmax_b = scores(lat_k(blk + 1), s_b, tk)
        state = absorb(s_a, tk, cmax_a, lat_v(blk), state)
        cmax_a = scores(ctx_k, s_a, n_ctx) if last else scores(lat_k(blk + 2), s_a, tk)
        return cmax_a, absorb(s_b, tk, cmax_b, lat_v(blk + 1), state)

    def two_pairs(j, carry):
        return pair(4 * j + 2, *pair(4 * j, *carry, last=False), last=False)

    init = tuple((jnp.full((1, tq), NEG, F32), jnp.zeros((MLA_VROWS, tq), F32)) for _ in heads)
    carry = (scores(lat_k(0), s_a, tk), init)
    inner_pairs = n_blocks // 2 - 1
    carry = lax.fori_loop(0, inner_pairs // 2, two_pairs, carry)
    if inner_pairs % 2:
        carry = pair(n_blocks - 4, *carry, last=False)
    cmax_a, state = pair(n_blocks - 2, *carry, last=True)
    state = absorb(s_a, n_ctx, cmax_a, ctx_v, state)
    outs = [acc[:MLA_V] / acc[MLA_V:MLA_V + 1] for _, acc in state]
    o_ref[...] = jnp.concatenate(outs, axis=0).T.astype(o_ref.dtype)


def _mla_attention(qt, k, vt, vtc, n_batch, seq, n_ctx):
    tq = ROW_TILE
    nt = FLASH_QTILES
    assert seq % (nt * tq) == 0
    nq = seq // (nt * tq)
    ctx0 = n_batch * seq // n_ctx
    return pl.pallas_call(
        _flash_kernel,
        grid=(n_batch, MLA_HEADS // 2, nq),
        in_specs=[pl.BlockSpec((nt, 2 * LANES, tq), lambda b, h, i: (b * nq + i, h, 0)),
                  pl.BlockSpec((seq, 2 * LANES), lambda b, h, i: (b, h)),
                  pl.BlockSpec((seq // tq, 2 * MLA_VROWS, tq), lambda b, h, i: (b, h, 0)),
                  pl.BlockSpec((n_ctx, 2 * LANES), lambda b, h, i: (ctx0 + b, h)),
                  pl.BlockSpec((None, 2 * MLA_VROWS, n_ctx), lambda b, h, i: (b, h, 0))],
        out_specs=pl.BlockSpec((nt * tq, LANES), lambda b, h, i: (b * nq + i, h)),
        out_shape=jax.ShapeDtypeStruct((n_batch * seq, MLA_HEADS * MLA_V), BF16),
        scratch_shapes=[pltpu.VMEM((tq, tq), F32)] * (4 * nt),
        compiler_params=_params(("parallel", "parallel", "arbitrary")),
        name="mla_flash",
    )(qt, k, vt, k, vtc)


def _router_kernel(x_ref, gate_m_ref, o_ref, wo_ref, g_ref, sh_ref, sc_ref, wr_ref, br_ref, tri_ref,
                   x1_ref, h_ref, mi_ref, mf_ref, cnt_ref, run_ref):
    @pl.when(pl.program_id(0) == 0)
    def _init():
        run_ref[...] = jnp.zeros_like(run_ref)

    x1 = _mixer_residual(x_ref[...], gate_m_ref, [o_ref], [wo_ref])
    x1_ref[...] = x1
    h = _norm_mod(x1, g_ref[...], sh_ref[...], sc_ref[...])
    h_ref[...] = h
    h_hi = h.astype(BF16)
    h_lo = (h - h_hi.astype(F32)).astype(BF16)
    hw = _dot(h_hi, wr_ref[...])
    logits = hw[:, :LANES] + hw[:, LANES:] + _dot(h_lo, wr_ref[:, :LANES]) + br_ref[...]
    lane = lax.broadcasted_iota(jnp.int32, logits.shape, 1).astype(F32)
    m1 = jnp.max(logits, axis=-1, keepdims=True)
    i1 = jnp.min(jnp.where(logits == m1, lane, float(LANES)), axis=-1, keepdims=True)
    rest = jnp.where(lane == i1, NEG, logits)
    m2 = jnp.max(rest, axis=-1, keepdims=True)
    i2 = jnp.min(jnp.where(rest == m2, lane, float(LANES)), axis=-1, keepdims=True)
    e = jnp.exp(m2 - m1)
    w1 = 1.0 / (1.0 + e)
    w2 = e / (1.0 + e)
    hit1, hit2 = lane == i1, lane == i2
    onehot = jnp.where(jnp.logical_or(hit1, hit2), 1.0, 0.0)
    before = _dot(tri_ref[...], onehot.astype(BF16)) + run_ref[0:1, :]
    r1 = jnp.sum(jnp.where(hit1, before, 0.0), axis=-1, keepdims=True)
    r2 = jnp.sum(jnp.where(hit2, before, 0.0), axis=-1, keepdims=True)
    run_ref[...] = run_ref[...] + jnp.sum(onehot, axis=0, keepdims=True)
    meta = jnp.where(lane == 0.0, i1, jnp.where(lane == 1.0, i2, jnp.where(lane == 2.0, r1, jnp.where(lane == 3.0, r2, 0.0))))
    mi_ref[...] = meta.astype(jnp.int32)
    mf_ref[...] = jnp.where(lane == 0.0, w1, jnp.where(lane == 1.0, w2, 0.0))
    cnt_ref[...] = run_ref[...]


def _router(x, gate_m, o, wo, g, sh, sc, wr, br, tri, n_rows, tiles_per_batch, n_batch):
    d = x.shape[1]
    tm = ROW_TILE
    mod = _mod_spec(d, tiles_per_batch, n_batch)
    row = lambda i: (i, 0)
    return pl.pallas_call(
        _router_kernel,
        grid=(n_rows // tm,),
        in_specs=[pl.BlockSpec((tm, d), row), mod, pl.BlockSpec((tm, o.shape[1]), row), _full(wo.shape),
                  _full((1, d)), mod, mod, _full(wr.shape), _full(br.shape), _full(tri.shape)],
        out_specs=[pl.BlockSpec((tm, d), row), pl.BlockSpec((tm, d), row), pl.BlockSpec((tm, LANES), row),
                   pl.BlockSpec((tm, LANES), row), _full((8, LANES))],
        out_shape=[jax.ShapeDtypeStruct((n_rows, d), F32), jax.ShapeDtypeStruct((n_rows, d), F32),
                   jax.ShapeDtypeStruct((n_rows, LANES), jnp.int32), jax.ShapeDtypeStruct((n_rows, LANES), F32),
                   jax.ShapeDtypeStruct((8, LANES), F32)],
        scratch_shapes=[pltpu.VMEM((8, LANES), F32)],
        compiler_params=_params(("arbitrary",)),
        name="moe_router",
    )(x, gate_m, o, wo, g, sh, sc, wr, br, tri)


def _slot_owner_kernel(slot_ref, owner_ref):
    def clear(j, carry):
        owner_ref[j] = 0
        return carry

    def claim(j, carry):
        owner_ref[slot_ref[j]] = j
        return carry

    lax.fori_loop(0, owner_ref.shape[0], clear, 0, unroll=8)
    lax.fori_loop(0, slot_ref.shape[0], claim, 0, unroll=8)


def _slot_owner(slots, n_slots):
    smem = pl.BlockSpec(memory_space=pltpu.SMEM)
    return pl.pallas_call(
        _slot_owner_kernel,
        in_specs=[smem],
        out_specs=smem,
        out_shape=jax.ShapeDtypeStruct((n_slots,), jnp.int32),
        name="moe_slot_owner",
    )(slots)


def _moe_kernel(te_ref, src_ref, dst_ref, rows_ref, nv_ref, h_hbm, wg_ref, wu_ref, wd_ref, y_hbm, hbuf, ybuf,
                sem_g, sem_s):
    i, f = pl.program_id(0), pl.program_id(1)
    _, tm, _ = hbuf.shape
    half = tm // MOE_FCHUNKS
    nv = nv_ref[0]
    valid = i < nv
    slot = lax.rem(i, 2)
    other = 1 - slot

    def gather_row(idx, s, r):
        return pltpu.make_async_copy(h_hbm.at[pl.ds(idx, 1), :], hbuf.at[s, pl.ds(r, 1), :], sem_g.at[s])

    def scatter_row(idx, s, r):
        return pltpu.make_async_copy(ybuf.at[s, pl.ds(r, 1), :], y_hbm.at[pl.ds(idx, 1), :], sem_s.at[s])

    def wait_gather(s):
        pltpu.make_async_copy(h_hbm.at[pl.ds(0, tm), :], hbuf.at[s], sem_g.at[s]).wait()

    def wait_scatter(s, n):
        n8 = pl.multiple_of((n // 8) * 8, 8)

        @pl.when(n8 > 0)
        def _():
            pltpu.make_async_copy(ybuf.at[s, pl.ds(0, n8), :], y_hbm.at[pl.ds(0, n8), :], sem_s.at[s]).wait()

        def one_row(j, carry):
            scatter_row(0, s, 0).wait()
            return carry

        lax.fori_loop(0, n - n8, one_row, 0)

    @pl.when(jnp.logical_and(i == 0, f == 0))
    def _prologue():
        ybuf[...] = jnp.zeros_like(ybuf)

        def issue(j, carry):
            gather_row(src_ref[j], 0, j).start()
            return carry

        lax.fori_loop(0, tm, issue, 0)

    @pl.when(jnp.logical_and(f == 0, i <= nv))
    def _rows_ready():
        wait_gather(slot)

    @pl.when(jnp.logical_and(f == 0, jnp.logical_and(i >= 1, i <= nv)))
    def _slot_free():
        wait_scatter(slot, rows_ref[jnp.maximum(i - 1, 0)])

    @pl.when(valid)
    def _compute():
        hb = hbuf[slot].astype(BF16)
        r0 = f * half
        n_prev = rows_ref[i]
        for j in range(half):
            gather_row(src_ref[(i + 1) * tm + r0 + j], other, r0 + j).start()

            @pl.when(r0 + j < n_prev)
            def _():
                scatter_row(dst_ref[i * tm + r0 + j], other, r0 + j).start()
        act = (_silu(_dot(hb, wg_ref[...])) * _dot(hb, wu_ref[...])).astype(BF16)
        part = _dot(act, wd_ref[...])
        ybuf[slot] = jnp.where(f == 0, part, ybuf[slot] + part)

    @pl.when(jnp.logical_and(i == nv, f == 0))
    def _flush():
        n_last = rows_ref[i]

        def issue(j, carry):
            scatter_row(dst_ref[i * tm + j], other, j).start()
            return carry

        lax.fori_loop(0, n_last, issue, 0)
        wait_scatter(other, n_last)


def _moe_experts(tile_expert, src, dst, rows, n_valid, h, wg, wu, wd, n_tiles, y_rows):
    d = h.shape[1]
    tm = MOE_TILE
    nf = MOE_FCHUNKS
    fc = wg.shape[2] // nf

    def f_idx(i, f, nv):
        return jnp.where(i < nv[0], f, nf - 1)

    grid_spec = pltpu.PrefetchScalarGridSpec(
        num_scalar_prefetch=5,
        grid=(n_tiles + 1, nf),
        in_specs=[pl.BlockSpec(memory_space=pl.ANY),
                  pl.BlockSpec((None, d, fc), lambda i, f, te, src, dst, rows, nv: (te[i], 0, f_idx(i, f, nv))),
                  pl.BlockSpec((None, d, fc), lambda i, f, te, src, dst, rows, nv: (te[i], 0, f_idx(i, f, nv))),
                  pl.BlockSpec((None, fc, d), lambda i, f, te, src, dst, rows, nv: (te[i], f_idx(i, f, nv), 0))],
        out_specs=pl.BlockSpec(memory_space=pl.ANY),
        scratch_shapes=[pltpu.VMEM((2, tm, d), F32), pltpu.VMEM((2, tm, d), F32),
                        pltpu.SemaphoreType.DMA((2,)), pltpu.SemaphoreType.DMA((2,))],
    )
    return pl.pallas_call(
        _moe_kernel,
        grid_spec=grid_spec,
        out_shape=jax.ShapeDtypeStruct((y_rows, d), F32),
        compiler_params=_params(("arbitrary", "arbitrary")),
        name="moe_experts",
    )(tile_expert, src, dst, rows, n_valid, h, wg, wu, wd)


def _combine_kernel(x_ref, gate_ref, mf_ref, fg_ref, y1_ref, y2_ref, o_ref):
    w = mf_ref[...]
    y = w[:, 0:1] * y1_ref[...] + w[:, 1:2] * y2_ref[...]
    o_ref[...] = _rms(x_ref[...] + gate_ref[...] * y, fg_ref[...])


def _moe_combine(x, gate, mf, fg, y, tiles_per_batch, n_batch):
    t, d = x.shape
    tm = ROW_TILE
    row = lambda i: (i, 0)
    return pl.pallas_call(
        _combine_kernel,
        grid=(t // tm,),
        in_specs=[pl.BlockSpec((tm, d), row), _mod_spec(d, tiles_per_batch, n_batch), pl.BlockSpec((tm, LANES), row),
                  _full((1, d)), pl.BlockSpec((tm, d), row), pl.BlockSpec((tm, d), lambda i: (t // tm + i, 0))],
        out_specs=pl.BlockSpec((tm, d), row),
        out_shape=jax.ShapeDtypeStruct((t, d), F32),
        compiler_params=_params(("parallel",)),
        name="moe_combine",
    )(x, gate, mf, fg, y, y)


def kernel(x, c, ctx, c_ctx, ada_w, ada_b, norm_g, final_g, na_w_in, na_rpb, sg_w, sg_b, sg_norm_g, even_w_out,
           ffn_w_gate, ffn_w_up, ffn_w_down, mla_w_in, mla_q_norm_g, mla_kv_norm_g, mla_w_uq, mla_w_ukv, mla_w_out,
           moe_w_router, moe_b_router, moe_w_gate, moe_w_up, moe_w_down):
    n_batch, seq, d = x.shape
    n_ctx = ctx.shape[1]
    n_lat = n_batch * seq
    assert ada_w.shape[0] == 2 and seq % ROW_TILE == 0 and n_batch * n_ctx == ROW_TILE
    assert n_batch + 1 <= 8 and seq % GRID_W == 0
    tpb = seq // ROW_TILE

    cond8 = jnp.concatenate([c, c_ctx[None, :], jnp.zeros((8 - n_batch - 1, d), F32)], axis=0)
    mods = _ada_mod(cond8, ada_w, ada_b)[:, :n_batch + 1]
    mod = lambda layer, k: mods[layer, :, k * d:(k + 1) * d].reshape(n_batch + 1, 1, d)
    x_lat, x_ctx = x.reshape(n_lat, d), ctx.reshape(n_batch * n_ctx, d)

    w_in = na_w_in[0].astype(BF16)
    qt, k, vt, ug = _even_in(x_lat, x_ctx, norm_g[0, 0][None], mod(0, 0), mod(0, 1), w_in[:, :3 * A_WIDTH],
                             w_in[:, 3 * A_WIDTH:], tpb, n_batch)
    bias = _na_bias_table(na_rpb[0], seq // GRID_W)
    attn = _na_attention(qt, k, vt, bias, n_batch, seq, n_ctx)
    bmat = jnp.repeat(sg_b[0].T, SG_GROUP_DIM, axis=1)
    avg = jnp.asarray(np.kron(np.eye(SG_GROUPS), np.full((SG_GROUP_DIM, SG_GROUP_DIM), 1.0 / SG_GROUP_DIM)), BF16)
    gated = _spatial_gating(ug, sg_w[0].astype(BF16), bmat, sg_norm_g[0][None], avg)
    w_out = even_w_out[0].astype(BF16)
    xs = _ffn(x_lat, x_ctx, mod(0, 2), [attn, gated], [w_out[:A_WIDTH], w_out[A_WIDTH:]], norm_g[0, 1][None], mod(0, 3),
              mod(0, 4), mod(0, 5), ffn_w_gate[0].astype(BF16), ffn_w_up[0].astype(BF16), ffn_w_down[0].astype(BF16),
              tpb, n_batch)

    win, wuq_t, wuk, wuv_t = _mla_weights(mla_w_in[0], mla_w_uq[0], mla_w_ukv[0])
    cos, sin = _rope_tables(n_batch, seq, n_ctx)
    qt, k, vt = _mla_pre(xs, norm_g[1, 0][None], mod(1, 0), mod(1, 1), win, mla_q_norm_g[0][None],
                         mla_kv_norm_g[0][None], wuq_t, wuk, wuv_t, cos, sin, tpb, n_batch)
    vtc = jnp.transpose(vt[n_batch * tpb].reshape(-1, n_batch, n_ctx), (1, 0, 2))
    o = _mla_attention(qt, k, vt, vtc, n_batch, seq, n_ctx)
    wr = jnp.concatenate([moe_w_router[0], jnp.zeros((d, LANES - N_EXPERTS), F32)], axis=1)
    wr_hi = wr.astype(BF16)
    wr = jnp.concatenate([wr_hi, (wr - wr_hi.astype(F32)).astype(BF16)], axis=1)
    br = jnp.concatenate([moe_b_router[0], jnp.full((LANES - N_EXPERTS,), NEG, F32)])[None]
    tri = jnp.asarray(np.tril(np.ones((ROW_TILE, ROW_TILE), np.float32), -1), BF16)
    x1, h, mi, mf, cnt = _router(xs, mod(1, 2), o, mla_w_out[0].astype(BF16), norm_g[1, 1][None], mod(1, 3),
                                 mod(1, 4), wr, br, tri, n_lat, tpb, n_batch)

    counts = cnt[0, :N_EXPERTS].astype(jnp.int32)
    tiles_e = (counts + MOE_TILE - 1) // MOE_TILE
    tile_end = jnp.cumsum(tiles_e)
    start = (tile_end - tiles_e) * MOE_TILE
    slot12 = jnp.concatenate([start[mi[:, 0]] + mi[:, 2], start[mi[:, 1]] + mi[:, 3]])
    n_tiles = 2 * n_lat // MOE_TILE + N_EXPERTS
    n_slots = n_tiles * MOE_TILE
    dst = _slot_owner(slot12.astype(jnp.int32), n_slots)
    spare_tile = jnp.zeros((MOE_TILE,), jnp.int32)
    src = jnp.concatenate([dst % n_lat, spare_tile])
    dst = jnp.concatenate([spare_tile, dst])
    n_valid = tile_end[-1:]
    last_tile = jnp.maximum(n_valid[0] - 1, 0)
    all_tiles = jnp.arange(n_tiles + 1, dtype=jnp.int32)
    tile_ids = jnp.minimum(all_tiles, last_tile)
    tile_expert = jnp.minimum(jnp.sum((tile_ids[:, None] >= tile_end[None, :]).astype(jnp.int32), axis=1),
                              N_EXPERTS - 1)
    filled = counts[tile_expert] - (all_tiles - (tile_end - tiles_e)[tile_expert]) * MOE_TILE
    rows = jnp.where(all_tiles < n_valid[0], jnp.clip(filled, 0, MOE_TILE), 0)
    rows = jnp.concatenate([jnp.zeros((1,), jnp.int32), rows]).astype(jnp.int32)
    y = _moe_experts(tile_expert, src, dst, rows, n_valid.astype(jnp.int32), h, moe_w_gate[0].astype(BF16),
                     moe_w_up[0].astype(BF16), moe_w_down[0].astype(BF16), n_tiles, 2 * n_lat)
    out = _moe_combine(x1, mod(1, 5), mf, final_g[None], y, tpb, n_batch)
    return out.reshape(n_batch, seq, d)
```

```python
import functools

import numpy as np
import jax
import jax.numpy as jnp
from jax import lax
from jax.experimental import pallas as pl
from jax.experimental.pallas import tpu as pltpu

F32 = jnp.float32
BF16 = jnp.bfloat16
EPS = 1e-6
NEG = -1e30

LANES = 128
VMEM_LIMIT_BYTES = 56 * 1024 * 1024

GRID_W = 64
NA_HEADS = 8
NA_HEAD_DIM = 64
NA_WIN_R = 8
NA_WIN_C = 16
SG_GROUPS = 8
SG_GROUP_DIM = 64
SG_CHUNK = 128
A_WIDTH = NA_HEADS * NA_HEAD_DIM
B_WIDTH = SG_GROUPS * SG_GROUP_DIM
MLA_HEADS = 16
MLA_NOPE = 64
MLA_ROPE = 32
MLA_V = 64
MLA_Q_RANK = 384
MLA_KV_RANK = 256
MLA_SCALE = (MLA_NOPE + MLA_ROPE) ** -0.5
ROPE_BASE = 10000.0
N_EXPERTS = 8

ROW_TILE = 512
NA_QROWS = 4
NA_QB = NA_QROWS * GRID_W
NA_KB = 3 * NA_QB
NA_STEP_HEADS = 4
MLA_VROWS = 80
FLASH_QTILES = 4
LOG2E = 1.4426950408889634
MOE_TILE = 512
MOE_FCHUNKS = 2


def _params(sem):
    return pltpu.CompilerParams(dimension_semantics=sem, vmem_limit_bytes=VMEM_LIMIT_BYTES)


def _dot(a, b):
    return jnp.dot(a, b, preferred_element_type=F32)


def _dot_nt(a, b):
    return lax.dot_general(a, b, (((1,), (1,)), ((), ())), preferred_element_type=F32)


def _silu(x):
    return x / (1.0 + jnp.exp(-x))


def _gelu_tanh(x):
    return 0.5 * x * (1.0 + jnp.tanh(0.7978845608028654 * (x + 0.044715 * (x * x * x))))


def _rms(x, g):
    return x * lax.rsqrt(jnp.mean(x * x, axis=-1, keepdims=True) + EPS) * g


def _norm_mod(x, g, sh, sc):
    return _rms(x, g) * (1.0 + sc) + sh


def _full(shape):
    n = len(shape)
    return pl.BlockSpec(shape, lambda *_: (0,) * n)


def _ada_kernel(cond_ref, w_ref, b_ref, o_ref):
    c = cond_ref[...]
    o_ref[...] = jnp.dot(_silu(c), w_ref[...], preferred_element_type=F32,
                         precision=lax.Precision.HIGHEST) + b_ref[...]


def _ada_mod(cond8, ada_w, ada_b):
    depth, d, n = ada_w.shape
    tn = n // 4
    return pl.pallas_call(
        _ada_kernel,
        grid=(depth, n // tn),
        in_specs=[_full((8, d)),
                  pl.BlockSpec((None, d, tn), lambda l, j: (l, 0, j)),
                  pl.BlockSpec((None, 1, tn), lambda l, j: (l, 0, j))],
        out_specs=pl.BlockSpec((None, 8, tn), lambda l, j: (l, 0, j)),
        out_shape=jax.ShapeDtypeStruct((depth, 8, n), F32),
        compiler_params=_params(("parallel", "parallel")),
        name="ada_mod",
    )(cond8, ada_w, ada_b.reshape(depth, 1, n))


def _group_map(tiles_per_batch, n_batch):
    return lambda t: (jnp.minimum(t // tiles_per_batch, n_batch), 0, 0)


def _mod_spec(d, tiles_per_batch, n_batch):
    return pl.BlockSpec((None, 1, d), _group_map(tiles_per_batch, n_batch))


def _stream_specs(x_lat, x_ctx):
    tm, d = ROW_TILE, x_lat.shape[1]
    assert x_ctx.shape[0] == tm
    last = x_lat.shape[0] // tm - 1
    return [pl.BlockSpec((tm, d), lambda i: (jnp.minimum(i, last), 0)), pl.BlockSpec((tm, d), lambda i: (0, 0))]


def _stream_tile(xl_ref, xc_ref, n_lat_tiles):
    return jnp.where(pl.program_id(0) < n_lat_tiles, xl_ref[...], xc_ref[...])


def _even_in_kernel(n_lat_tiles, xl_ref, xc_ref, g_ref, sh_ref, sc_ref, wqkv_ref, wug_ref, qt_ref, k_ref, vt_ref,
                    ug_ref):
    x = _stream_tile(xl_ref, xc_ref, n_lat_tiles)
    h = _norm_mod(x, g_ref[...], sh_ref[...], sc_ref[...]).astype(BF16)
    qkv = _dot(h, wqkv_ref[...])
    qt_ref[...] = (qkv[:, :A_WIDTH] * (NA_HEAD_DIM ** -0.5 * LOG2E)).T.astype(qt_ref.dtype)
    k_ref[...] = qkv[:, A_WIDTH:2 * A_WIDTH].astype(k_ref.dtype)
    vt_ref[...] = qkv[:, 2 * A_WIDTH:].T.astype(vt_ref.dtype)
    ug_ref[...] = _dot(h, wug_ref[...])


def _even_in(x_lat, x_ctx, g, sh, sc, wqkv, wug, tiles_per_batch, n_batch):
    d = x_lat.shape[1]
    t = x_lat.shape[0] + x_ctx.shape[0]
    tm = ROW_TILE
    mod = _mod_spec(d, tiles_per_batch, n_batch)
    return pl.pallas_call(
        functools.partial(_even_in_kernel, x_lat.shape[0] // tm),
        grid=(t // tm,),
        in_specs=_stream_specs(x_lat, x_ctx) + [_full((1, d)), mod, mod, _full(wqkv.shape), _full(wug.shape)],
        out_specs=[pl.BlockSpec((A_WIDTH, tm), lambda i: (0, i)), pl.BlockSpec((tm, A_WIDTH), lambda i: (i, 0)),
                   pl.BlockSpec((A_WIDTH, tm), lambda i: (0, i)), pl.BlockSpec((tm, wug.shape[1]), lambda i: (i, 0))],
        out_shape=[jax.ShapeDtypeStruct((A_WIDTH, t), BF16), jax.ShapeDtypeStruct((t, A_WIDTH), BF16),
                   jax.ShapeDtypeStruct((A_WIDTH, t), BF16), jax.ShapeDtypeStruct((t, wug.shape[1]), F32)],
        compiler_params=_params(("parallel",)),
        name="even_in",
    )(x_lat, x_ctx, g, sh, sc, wqkv, wug)


def _na_bias_table(rpb, rows):
    nblk = rows // NA_QROWS
    n_heads = rpb.shape[0]
    win_rows = NA_KB // GRID_W
    qc = np.arange(GRID_W)
    col_start = np.clip(qc - NA_WIN_C // 2, 0, GRID_W - NA_WIN_C)
    col_ok = (qc[None, :] >= col_start[:, None]) & (qc[None, :] < col_start[:, None] + NA_WIN_C)
    col_j = qc[None, :] - qc[:, None] + NA_WIN_C - 1
    sel_c = ((col_j[None] == np.arange(2 * NA_WIN_C - 1)[:, None, None]) & col_ok[None]).astype(np.float32)
    toeplitz = jnp.einsum("hij,jqk->hikq", rpb, sel_c, precision=lax.Precision.HIGHEST) * LOG2E
    toeplitz = jnp.where(col_ok.T[None, None], toeplitz, NEG)
    n_rel = 2 * NA_WIN_R - 1
    toeplitz = jnp.concatenate([toeplitz, jnp.full((n_heads, 1, GRID_W, GRID_W), NEG, F32)], axis=1)
    rel = np.full((3, NA_QROWS, win_rows), n_rel, np.int32)
    for kind, j in enumerate((0, 1, nblk - 1)):
        first_row = NA_QROWS * int(np.clip(j - 1, 0, nblk - 3))
        for a in range(NA_QROWS):
            r = NA_QROWS * j + a
            row_start = int(np.clip(r - NA_WIN_R // 2, 0, rows - NA_WIN_R))
            for b in range(win_rows):
                kr = first_row + b
                if row_start <= kr < row_start + NA_WIN_R:
                    rel[kind, a, b] = kr - r + NA_WIN_R - 1
    grid_spec = pltpu.PrefetchScalarGridSpec(
        num_scalar_prefetch=1,
        grid=(3, n_heads),
        in_specs=[pl.BlockSpec((None, n_rel + 1, GRID_W, GRID_W), lambda c, h, rel: (h, 0, 0, 0))],
        out_specs=pl.BlockSpec((None, None, NA_KB, NA_QB), lambda c, h, rel: (c, h, 0, 0)),
    )
    return pl.pallas_call(
        _na_bias_kernel,
        grid_spec=grid_spec,
        out_shape=jax.ShapeDtypeStruct((3, n_heads, NA_KB, NA_QB), F32),
        compiler_params=_params(("parallel", "parallel")),
        name="na_bias",
    )(jnp.asarray(rel.reshape(-1)), toeplitz)


def _na_bias_kernel(rel_ref, t_ref, o_ref):
    kind = pl.program_id(0)
    win_rows = NA_KB // GRID_W
    for b in range(win_rows):
        for ap in range(NA_QROWS // 2):
            rel = [rel_ref[(kind * NA_QROWS + 2 * ap + s) * win_rows + b] for s in range(2)]
            pair = jnp.concatenate([t_ref[rel[0]], t_ref[rel[1]]], axis=1)
            o_ref[GRID_W * b:GRID_W * (b + 1), 2 * GRID_W * ap:2 * GRID_W * (ap + 1)] = pair


def _col_reduce(x_t, combine, reduce):
    parts = [x_t[r:r + 64] for r in range(0, x_t.shape[0], 64)]
    while len(parts) > 1:
        parts = [combine(parts[j], parts[j + 1]) for j in range(0, len(parts) - 1, 2)] + parts[len(parts) & ~1:]
    return reduce(parts[0], axis=0, keepdims=True)


def _col_max(s_t):
    return _col_reduce(s_t, jnp.maximum, jnp.max)


def _col_sum(p_t):
    return _col_reduce(p_t, jnp.add, jnp.sum)


def _na_kernel(qt_ref, k0_ref, k1_ref, k2_ref, vt0_ref, vt1_ref, vt2_ref, kc_ref, vtc_ref, bias_ref, o_ref):
    jj = pl.program_id(2)
    row = lax.broadcasted_iota(jnp.int32, (NA_STEP_HEADS * NA_HEAD_DIM, 1), 0)
    qt = qt_ref[...]
    zero = jnp.zeros_like(qt)
    heads = range(NA_STEP_HEADS)

    def head_q(hh):
        mine = jnp.logical_and(row >= NA_HEAD_DIM * hh, row < NA_HEAD_DIM * (hh + 1))
        return jnp.where(mine, qt, zero)

    def attend(s_t, vt_refs, hh):
        m = functools.reduce(jnp.maximum, [_col_max(x) for x in s_t])
        p_t = [jnp.exp2(x - m) for x in s_t]
        l = functools.reduce(jnp.add, [_col_sum(x) for x in p_t])
        vrows = slice(NA_HEAD_DIM * hh, NA_HEAD_DIM * (hh + 1))
        o_t = functools.reduce(jnp.add, [_dot(vt[vrows, :], x.astype(BF16)) for vt, x in zip(vt_refs, p_t)])
        return o_t / l

    def finish(outs):
        o_ref[...] = jnp.concatenate(outs, axis=0).T.astype(o_ref.dtype)

    @pl.when(jj == 0)
    def _ctx_queries():
        finish([attend([_dot(kc_ref[...], head_q(hh))], [vtc_ref], hh) for hh in heads])

    @pl.when(jj > 0)
    def _latent_queries():
        k_refs = (k0_ref, k1_ref, k2_ref)
        s_all = []
        for hh in heads:
            qh = head_q(hh)
            s_t = [_dot(k_refs[i][...], qh) + bias_ref[hh, NA_QB * i:NA_QB * (i + 1), :] for i in range(3)]
            s_all.append(s_t + [_dot(kc_ref[...], qh)])
        finish([attend(s_all[hh], (vt0_ref, vt1_ref, vt2_ref, vtc_ref), hh) for hh in heads])


def _na_attention(qt, k, vt, bias, n_batch, seq, n_ctx):
    t = k.shape[0]
    assert n_ctx == NA_QB and seq % NA_QB == 0
    nblk = seq // NA_QB
    assert nblk >= 3
    hp = NA_HEADS // NA_STEP_HEADS
    width = NA_STEP_HEADS * NA_HEAD_DIM
    ctx_blk = n_batch * nblk

    def q_blk(b, j):
        return jnp.where(j == 0, ctx_blk + b, b * nblk + j - 1)

    def win_blk(b, j, i):
        return b * nblk + jnp.clip(j - 2, 0, nblk - 3) + i

    def bias_map(b, h, j):
        return (jnp.where(j <= 1, 0, jnp.where(j == nblk, 2, 1)), h, 0, 0)

    tok, feat = (NA_QB, width), (width, NA_QB)
    in_specs = ([pl.BlockSpec(feat, lambda b, h, j: (h, q_blk(b, j)))]
                + [pl.BlockSpec(tok, functools.partial(lambda i, b, h, j: (win_blk(b, j, i), h), i)) for i in range(3)]
                + [pl.BlockSpec(feat, functools.partial(lambda i, b, h, j: (h, win_blk(b, j, i)), i)) for i in range(3)]
                + [pl.BlockSpec(tok, lambda b, h, j: (ctx_blk + b, h)),
                   pl.BlockSpec(feat, lambda b, h, j: (h, ctx_blk + b)),
                   pl.BlockSpec((None, NA_STEP_HEADS, NA_KB, NA_QB), bias_map)])
    return pl.pallas_call(
        _na_kernel,
        grid=(n_batch, hp, nblk + 1),
        in_specs=in_specs,
        out_specs=pl.BlockSpec(tok, lambda b, h, j: (q_blk(b, j), h)),
        out_shape=jax.ShapeDtypeStruct((t, A_WIDTH), BF16),
        compiler_params=_params(("parallel", "parallel", "arbitrary")),
        name="na_attention",
    )(qt, k, k, k, vt, vt, vt, k, vt, bias)


def _seg_mean(x, avg):
    hi = x.astype(BF16)
    lo = (x - hi.astype(F32)).astype(BF16)
    return _dot(hi, avg) + _dot(lo, avg)


def _sg_kernel(ug_ref, ws_ref, bm_ref, ng_ref, avg_ref, o_ref):
    lane = lax.broadcasted_iota(jnp.int32, (1, LANES), 1)
    first = lane < SG_GROUP_DIM
    avg = avg_ref[...]
    g = _gelu_tanh(ug_ref[:, B_WIDTH:2 * B_WIDTH])
    d = g - _seg_mean(g, avg)
    var = _seg_mean(d * d, avg)
    gn = (d * lax.rsqrt(var + EPS) * ng_ref[...]).astype(BF16)
    for c in range(ROW_TILE // SG_CHUNK):
        rows = slice(SG_CHUNK * c, SG_CHUNK * (c + 1))
        parts = []
        for j in range(SG_GROUPS // 2):
            gj = gn[rows, LANES * j:LANES * (j + 1)]
            parts.append(jnp.where(first, _dot(ws_ref[2 * j], gj), _dot(ws_ref[2 * j + 1], gj)))
        mixed = jnp.concatenate(parts, axis=1) + bm_ref[...]
        o_ref[rows, :] = (_gelu_tanh(ug_ref[rows, 0:B_WIDTH]) * mixed).astype(o_ref.dtype)


def _spatial_gating(ug, ws, bmat, ng, avg):
    t = ug.shape[0]
    tm = ROW_TILE
    return pl.pallas_call(
        _sg_kernel,
        grid=(t // tm,),
        in_specs=[pl.BlockSpec((tm, 2 * B_WIDTH), lambda i: (i, 0)), _full(ws.shape), _full(bmat.shape),
                  _full(ng.shape), _full(avg.shape)],
        out_specs=pl.BlockSpec((tm, B_WIDTH), lambda i: (i, 0)),
        out_shape=jax.ShapeDtypeStruct((t, B_WIDTH), BF16),
        compiler_params=_params(("parallel",)),
        name="spatial_gating",
    )(ug, ws, bmat, ng, avg)


def _mixer_residual(x, gate_ref, a_refs, w_refs):
    y = functools.reduce(jnp.add, [_dot(a[...], w[...]) for a, w in zip(a_refs, w_refs)])
    return x + gate_ref[...] * y


def _ffn_kernel(n_chunks, n_in, n_lat_tiles, xl_ref, xc_ref, gate_m_ref, g_ref, sh_ref, sc_ref, gate_ref, *refs):
    a_refs, w_refs = refs[:n_in], refs[n_in:2 * n_in]
    wg_ref, wu_ref, wd_ref, o_ref = refs[2 * n_in:]
    x = _mixer_residual(_stream_tile(xl_ref, xc_ref, n_lat_tiles), gate_m_ref, a_refs, w_refs)
    h = _norm_mod(x, g_ref[...], sh_ref[...], sc_ref[...]).astype(BF16)
    fc = wg_ref.shape[1] // n_chunks
    acc = None
    for c in range(n_chunks):
        cs = slice(fc * c, fc * (c + 1))
        act = (_silu(_dot(h, wg_ref[:, cs])) * _dot(h, wu_ref[:, cs])).astype(BF16)
        part = _dot(act, wd_ref[cs, :])
        acc = part if acc is None else acc + part
    o_ref[...] = x + gate_ref[...] * acc


def _ffn(x_lat, x_ctx, gate_m, acts, weights, g, sh, sc, gate, wg, wu, wd, tiles_per_batch, n_batch):
    d = x_lat.shape[1]
    t = x_lat.shape[0] + x_ctx.shape[0]
    tm = ROW_TILE
    f = wg.shape[1]
    n_chunks = 2 if f % (2 * LANES) == 0 else 1
    mod = _mod_spec(d, tiles_per_batch, n_batch)
    row = lambda i: (i, 0)
    resident = lambda w: pl.BlockSpec(w.shape, lambda i: (0, 0), pipeline_mode=pl.Buffered(1))
    return pl.pallas_call(
        functools.partial(_ffn_kernel, n_chunks, len(acts), x_lat.shape[0] // tm),
        grid=(t // tm,),
        in_specs=(_stream_specs(x_lat, x_ctx) + [mod, _full((1, d)), mod, mod, mod]
                  + [pl.BlockSpec((tm, a.shape[1]), row) for a in acts] + [resident(w) for w in weights]
                  + [resident(wg), resident(wu), resident(wd)]),
        out_specs=pl.BlockSpec((tm, d), row),
        out_shape=jax.ShapeDtypeStruct((t, d), F32),
        compiler_params=_params(("parallel",)),
        name="ffn",
    )(x_lat, x_ctx, gate_m, g, sh, sc, gate, *acts, *weights, wg, wu, wd)


def _rope_tables(n_batch, seq, n_ctx):
    t = np.arange(seq)
    pos = np.stack([t // GRID_W, t % GRID_W], axis=-1).astype(np.float32)
    n_freq = MLA_ROPE // 4
    inv = jnp.power(ROPE_BASE, -jnp.arange(n_freq, dtype=F32) / n_freq)
    ang = jnp.asarray(pos)[:, :, None] * inv
    cos, sin = jnp.cos(ang), jnp.sin(ang)
    cos_r = jnp.concatenate([cos, cos], axis=-1).reshape(seq, MLA_ROPE)
    sin_r = jnp.concatenate([-sin, sin], axis=-1).reshape(seq, MLA_ROPE)
    ones = jnp.ones((seq, MLA_NOPE), F32)
    pad = LANES - MLA_NOPE - MLA_ROPE
    cos_l = jnp.concatenate([ones, cos_r, jnp.ones((seq, pad), F32)], axis=-1)
    sin_l = jnp.concatenate([0 * ones, sin_r, jnp.zeros((seq, pad), F32)], axis=-1)
    cos_all = jnp.concatenate([jnp.tile(cos_l, (n_batch, 1)), jnp.ones((n_batch * n_ctx, LANES), F32)])
    sin_all = jnp.concatenate([jnp.tile(sin_l, (n_batch, 1)), jnp.zeros((n_batch * n_ctx, LANES), F32)])
    return cos_all, sin_all


def _swap_rope_cols(w_rope):
    q = MLA_ROPE // 4
    return jnp.concatenate([w_rope[..., q:2 * q], w_rope[..., 0:q], w_rope[..., 3 * q:4 * q],
                            w_rope[..., 2 * q:3 * q]], axis=-1)


def _mla_weights(w_in, w_uq, w_ukv):
    d = w_in.shape[0]
    pad = LANES - MLA_NOPE - MLA_ROPE
    w_kr = w_in[:, MLA_Q_RANK + MLA_KV_RANK:]
    lanes_kr = lambda w: jnp.concatenate([jnp.zeros((d, MLA_NOPE), F32), w, jnp.zeros((d, pad), F32)], axis=1)
    win = jnp.concatenate([w_in[:, :MLA_Q_RANK + MLA_KV_RANK], lanes_kr(w_kr), lanes_kr(_swap_rope_cols(w_kr))],
                          axis=1)
    uq = w_uq.reshape(MLA_Q_RANK, MLA_HEADS, MLA_NOPE + MLA_ROPE)
    zq = jnp.zeros((MLA_Q_RANK, MLA_HEADS, pad), F32)
    uq_a = jnp.concatenate([uq, zq], axis=-1).reshape(MLA_Q_RANK, MLA_HEADS * LANES)
    uq_b = _swap_rope_cols(uq[..., MLA_NOPE:]).reshape(MLA_Q_RANK, MLA_HEADS * MLA_ROPE)
    wuq_t = jnp.concatenate([uq_a, uq_b], axis=1).T
    ukv = w_ukv.reshape(MLA_KV_RANK, MLA_HEADS, MLA_NOPE + MLA_V)
    zk = jnp.zeros((MLA_KV_RANK, MLA_HEADS, LANES - MLA_NOPE), F32)
    uk = jnp.concatenate([ukv[..., :MLA_NOPE], zk], axis=-1).reshape(MLA_KV_RANK, MLA_HEADS * LANES)
    uv_t = jnp.transpose(ukv[..., MLA_NOPE:], (1, 2, 0))
    uv_t = jnp.concatenate([uv_t, jnp.zeros((MLA_HEADS, MLA_VROWS - MLA_V, MLA_KV_RANK), F32)], axis=1)
    return (win.astype(BF16), wuq_t.astype(BF16), uk.astype(BF16),
            uv_t.reshape(MLA_HEADS * MLA_VROWS, MLA_KV_RANK).astype(BF16))


def _mla_pre_kernel(x_ref, g_ref, sh_ref, sc_ref, win_ref, qg_ref, kvg_ref, wuqt_ref, wuk_ref, wuvt_ref,
                    cos_ref, sin_ref, cost_ref, sint_ref, qt_ref, k_ref, vt_ref):
    h = _norm_mod(x_ref[...], g_ref[...], sh_ref[...], sc_ref[...]).astype(BF16)
    p = _dot(h, win_ref[...])
    cq = _rms(p[:, :MLA_Q_RANK], qg_ref[...])
    ckv = _rms(p[:, MLA_Q_RANK:MLA_Q_RANK + MLA_KV_RANK], kvg_ref[...])
    cq_t = cq.T.astype(BF16)
    ckv_t = ckv.T.astype(BF16)
    ckv_b = ckv.astype(BF16)
    o_kr = MLA_Q_RANK + MLA_KV_RANK
    kr_rot = p[:, o_kr:o_kr + LANES] * cos_ref[...] + p[:, o_kr + LANES:o_kr + 2 * LANES] * sin_ref[...]
    kr2 = jnp.concatenate([kr_rot, kr_rot], axis=1)
    rope = slice(MLA_NOPE, MLA_NOPE + MLA_ROPE)
    cos_r, sin_r = cost_ref[rope, :], sint_ref[rope, :]
    half = MLA_HEADS * LANES
    q_swapped = _dot(wuqt_ref[half:half + MLA_HEADS * MLA_ROPE, :], cq_t)
    for c in range(MLA_HEADS // 2):
        rs = slice(2 * LANES * c, 2 * LANES * (c + 1))
        q_pair = _dot(wuqt_ref[rs, :], cq_t)
        rows = []
        for hh in range(2):
            q_h = q_pair[LANES * hh:LANES * (hh + 1)]
            swapped = q_swapped[MLA_ROPE * (2 * c + hh):MLA_ROPE * (2 * c + hh + 1)]
            rows += [q_h[:MLA_NOPE], q_h[rope] * cos_r + swapped * sin_r, q_h[MLA_NOPE + MLA_ROPE:]]
        q_t = jnp.concatenate(rows, axis=0)
        qt_ref[rs, :] = (q_t * (MLA_SCALE * LOG2E)).astype(qt_ref.dtype)
        k_ref[:, rs] = (_dot(ckv_b, wuk_ref[:, rs]) + kr2).astype(k_ref.dtype)
    group = 4 * MLA_VROWS
    row = lax.broadcasted_iota(jnp.int32, (group, 1), 0)
    ones_rows = functools.reduce(jnp.add, [jnp.where(row == MLA_VROWS * j + MLA_V, 1.0, 0.0) for j in range(4)])
    for c in range(MLA_HEADS // 4):
        rv = slice(group * c, group * (c + 1))
        vt_ref[rv, :] = (_dot(wuvt_ref[rv, :], ckv_t) + ones_rows).astype(vt_ref.dtype)


def _mla_pre(x, g, sh, sc, win, qg, kvg, wuq_t, wuk, wuv_t, cos, sin, tiles_per_batch, n_batch):
    t, d = x.shape
    tm = ROW_TILE
    n = MLA_HEADS * LANES
    nv = MLA_HEADS * MLA_VROWS
    mod = _mod_spec(d, tiles_per_batch, n_batch)
    row = lambda i: (i, 0)
    col = lambda i: (0, i)
    return pl.pallas_call(
        _mla_pre_kernel,
        grid=(t // tm,),
        in_specs=[pl.BlockSpec((tm, d), row), _full((1, d)), mod, mod, _full(win.shape), _full(qg.shape),
                  _full(kvg.shape), _full(wuq_t.shape), _full(wuk.shape), _full(wuv_t.shape),
                  pl.BlockSpec((tm, LANES), row), pl.BlockSpec((tm, LANES), row),
                  pl.BlockSpec((LANES, tm), col), pl.BlockSpec((LANES, tm), col)],
        out_specs=[pl.BlockSpec((None, n, tm), lambda i: (i, 0, 0)), pl.BlockSpec((tm, n), row),
                   pl.BlockSpec((None, nv, tm), lambda i: (i, 0, 0))],
        out_shape=[jax.ShapeDtypeStruct((t // tm, n, tm), BF16), jax.ShapeDtypeStruct((t, n), BF16),
                   jax.ShapeDtypeStruct((t // tm, nv, tm), BF16)],
        compiler_params=_params(("parallel",)),
        name="mla_pre",
    )(x, g, sh, sc, win, qg, kvg, wuq_t, wuk, wuv_t, cos, sin, cos.T, sin.T)


def _flash_kernel(qts_ref, k_ref, vt_ref, kc_ref, vtc_ref, os_ref, *s_refs):
    for t in range(qts_ref.shape[0]):
        _flash_tile(qts_ref.at[t], k_ref, vt_ref, kc_ref, vtc_ref, os_ref.at[pl.ds(t * qts_ref.shape[2], qts_ref.shape[2])],
                    *s_refs[4 * t:4 * t + 4])


def _flash_tile(qt_ref, k_ref, vt_ref, kc_ref, vtc_ref, o_ref, sa0_ref, sa1_ref, sb0_ref, sb1_ref):
    tq = qt_ref.shape[1]
    n_blocks, _, tk = vt_ref.shape
    n_ctx = kc_ref.shape[0]
    heads = range(2)
    s_a, s_b = (sa0_ref, sa1_ref), (sb0_ref, sb1_ref)
    lanes = lambda hh: slice(LANES * hh, LANES * (hh + 1))
    vrows = lambda hh: slice(MLA_VROWS * hh, MLA_VROWS * (hh + 1))

    def scores(kb_of, bufs, rows):
        cmax = []
        for hh in heads:
            s_t = _dot(kb_of(hh), qt_ref[lanes(hh), :])
            bufs[hh][0:rows, :] = s_t
            cmax.append(_col_max(s_t))
        return tuple(cmax)

    def absorb(bufs, rows, cmax, vt_of, state):
        new = []
        for hh in heads:
            m, acc = state[hh]
            m_new = jnp.maximum(m, cmax[hh])
            p_t = jnp.exp2(bufs[hh][0:rows, :] - m_new).astype(BF16)
            new.append((m_new, acc * jnp.exp2(m - m_new) + _dot(vt_of(hh), p_t)))
        return tuple(new)

    def lat_k(blk):
        off = blk * tk if isinstance(blk, int) else pl.multiple_of(blk * tk, tk)
        return lambda hh: k_ref[pl.ds(off, tk), lanes(hh)]

    lat_v = lambda blk: (lambda hh: vt_ref[blk, vrows(hh), :])
    ctx_k = lambda hh: kc_ref[:, lanes(hh)]
    ctx_v = lambda hh: vtc_ref[vrows(hh), :]

    def pair(blk, cmax_a, state, last):
        cmax_b = scores(lat_k(blk + 1), s_b, tk)
        state = absorb(s_a, tk, cmax_a, lat_v(blk), state)
        cmax_a = scores(ctx_k, s_a, n_ctx) if last else scores(lat_k(blk + 2), s_a, tk)
        return cmax_a, absorb(s_b, tk, cmax_b, lat_v(blk + 1), state)

    def two_pairs(j, carry):
        return pair(4 * j + 2, *pair(4 * j, *carry, last=False), last=False)

    init = tuple((jnp.full((1, tq), NEG, F32), jnp.zeros((MLA_VROWS, tq), F32)) for _ in heads)
    carry = (scores(lat_k(0), s_a, tk), init)
    inner_pairs = n_blocks // 2 - 1
    carry = lax.fori_loop(0, inner_pairs // 2, two_pairs, carry)
    if inner_pairs % 2:
        carry = pair(n_blocks - 4, *carry, last=False)
    cmax_a, state = pair(n_blocks - 2, *carry, last=True)
    state = absorb(s_a, n_ctx, cmax_a, ctx_v, state)
    outs = [acc[:MLA_V] / acc[MLA_V:MLA_V + 1] for _, acc in state]
    o_ref[...] = jnp.concatenate(outs, axis=0).T.astype(o_ref.dtype)


def _mla_attention(qt, k, vt, vtc, n_batch, seq, n_ctx):
    tq = ROW_TILE
    nt = FLASH_QTILES
    assert seq % (nt * tq) == 0
    nq = seq // (nt * tq)
    ctx0 = n_batch * seq // n_ctx
    return pl.pallas_call(
        _flash_kernel,
        grid=(n_batch, MLA_HEADS // 2, nq),
        in_specs=[pl.BlockSpec((nt, 2 * LANES, tq), lambda b, h, i: (b * nq + i, h, 0)),
                  pl.BlockSpec((seq, 2 * LANES), lambda b, h, i: (b, h)),
                  pl.BlockSpec((seq // tq, 2 * MLA_VROWS, tq), lambda b, h, i: (b, h, 0)),
                  pl.BlockSpec((n_ctx, 2 * LANES), lambda b, h, i: (ctx0 + b, h)),
                  pl.BlockSpec((None, 2 * MLA_VROWS, n_ctx), lambda b, h, i: (b, h, 0))],
        out_specs=pl.BlockSpec((nt * tq, LANES), lambda b, h, i: (b * nq + i, h)),
        out_shape=jax.ShapeDtypeStruct((n_batch * seq, MLA_HEADS * MLA_V), BF16),
        scratch_shapes=[pltpu.VMEM((tq, tq), F32)] * (4 * nt),
        compiler_params=_params(("parallel", "parallel", "arbitrary")),
        name="mla_flash",
    )(qt, k, vt, k, vtc)


def _router_kernel(x_ref, gate_m_ref, o_ref, wo_ref, g_ref, sh_ref, sc_ref, wr_ref, br_ref, tri_ref,
                   x1_ref, h_ref, mi_ref, mf_ref, cnt_ref, run_ref):
    @pl.when(pl.program_id(0) == 0)
    def _init():
        run_ref[...] = jnp.zeros_like(run_ref)

    x1 = _mixer_residual(x_ref[...], gate_m_ref, [o_ref], [wo_ref])
    x1_ref[...] = x1
    h = _norm_mod(x1, g_ref[...], sh_ref[...], sc_ref[...])
    h_ref[...] = h
    h_hi = h.astype(BF16)
    h_lo = (h - h_hi.astype(F32)).astype(BF16)
    hw = _dot(h_hi, wr_ref[...])
    logits = hw[:, :LANES] + hw[:, LANES:] + _dot(h_lo, wr_ref[:, :LANES]) + br_ref[...]
    lane = lax.broadcasted_iota(jnp.int32, logits.shape, 1).astype(F32)
    m1 = jnp.max(logits, axis=-1, keepdims=True)
    i1 = jnp.min(jnp.where(logits == m1, lane, float(LANES)), axis=-1, keepdims=True)
    rest = jnp.where(lane == i1, NEG, logits)
    m2 = jnp.max(rest, axis=-1, keepdims=True)
    i2 = jnp.min(jnp.where(rest == m2, lane, float(LANES)), axis=-1, keepdims=True)
    e = jnp.exp(m2 - m1)
    w1 = 1.0 / (1.0 + e)
    w2 = e / (1.0 + e)
    hit1, hit2 = lane == i1, lane == i2
    onehot = jnp.where(jnp.logical_or(hit1, hit2), 1.0, 0.0)
    before = _dot(tri_ref[...], onehot.astype(BF16)) + run_ref[0:1, :]
    r1 = jnp.sum(jnp.where(hit1, before, 0.0), axis=-1, keepdims=True)
    r2 = jnp.sum(jnp.where(hit2, before, 0.0), axis=-1, keepdims=True)
    run_ref[...] = run_ref[...] + jnp.sum(onehot, axis=0, keepdims=True)
    meta = jnp.where(lane == 0.0, i1, jnp.where(lane == 1.0, i2, jnp.where(lane == 2.0, r1, jnp.where(lane == 3.0, r2, 0.0))))
    mi_ref[...] = meta.astype(jnp.int32)
    mf_ref[...] = jnp.where(lane == 0.0, w1, jnp.where(lane == 1.0, w2, 0.0))
    cnt_ref[...] = run_ref[...]


def _router(x, gate_m, o, wo, g, sh, sc, wr, br, tri, n_rows, tiles_per_batch, n_batch):
    d = x.shape[1]
    tm = ROW_TILE
    mod = _mod_spec(d, tiles_per_batch, n_batch)
    row = lambda i: (i, 0)
    return pl.pallas_call(
        _router_kernel,
        grid=(n_rows // tm,),
        in_specs=[pl.BlockSpec((tm, d), row), mod, pl.BlockSpec((tm, o.shape[1]), row), _full(wo.shape),
                  _full((1, d)), mod, mod, _full(wr.shape), _full(br.shape), _full(tri.shape)],
        out_specs=[pl.BlockSpec((tm, d), row), pl.BlockSpec((tm, d), row), pl.BlockSpec((tm, LANES), row),
                   pl.BlockSpec((tm, LANES), row), _full((8, LANES))],
        out_shape=[jax.ShapeDtypeStruct((n_rows, d), F32), jax.ShapeDtypeStruct((n_rows, d), F32),
                   jax.ShapeDtypeStruct((n_rows, LANES), jnp.int32), jax.ShapeDtypeStruct((n_rows, LANES), F32),
                   jax.ShapeDtypeStruct((8, LANES), F32)],
        scratch_shapes=[pltpu.VMEM((8, LANES), F32)],
        compiler_params=_params(("arbitrary",)),
        name="moe_router",
    )(x, gate_m, o, wo, g, sh, sc, wr, br, tri)


def _slot_owner_kernel(slot_ref, owner_ref):
    def clear(j, carry):
        owner_ref[j] = 0
        return carry

    def claim(j, carry):
        owner_ref[slot_ref[j]] = j
        return carry

    lax.fori_loop(0, owner_ref.shape[0], clear, 0, unroll=32)
    lax.fori_loop(0, slot_ref.shape[0], claim, 0, unroll=32)


def _slot_owner(slots, n_slots):
    smem = pl.BlockSpec(memory_space=pltpu.SMEM)
    return pl.pallas_call(
        _slot_owner_kernel,
        in_specs=[smem],
        out_specs=smem,
        out_shape=jax.ShapeDtypeStruct((n_slots,), jnp.int32),
        name="moe_slot_owner",
    )(slots)


def _moe_kernel(te_ref, src_ref, dst_ref, rows_ref, nv_ref, h_hbm, wg_ref, wu_ref, wd_ref, y_hbm, hbuf, ybuf,
                sem_g, sem_s):
    i, f = pl.program_id(0), pl.program_id(1)
    _, tm, _ = hbuf.shape
    half = tm // MOE_FCHUNKS
    nv = nv_ref[0]
    valid = i < nv
    slot = lax.rem(i, 2)
    other = 1 - slot

    def gather_row(idx, s, r):
        return pltpu.make_async_copy(h_hbm.at[pl.ds(idx, 1), :], hbuf.at[s, pl.ds(r, 1), :], sem_g.at[s])

    def scatter_row(idx, s, r):
        return pltpu.make_async_copy(ybuf.at[s, pl.ds(r, 1), :], y_hbm.at[pl.ds(idx, 1), :], sem_s.at[s])

    def wait_gather(s):
        pltpu.make_async_copy(h_hbm.at[pl.ds(0, tm), :], hbuf.at[s], sem_g.at[s]).wait()

    def wait_scatter(s, n):
        n8 = pl.multiple_of((n // 8) * 8, 8)

        @pl.when(n8 > 0)
        def _():
            pltpu.make_async_copy(ybuf.at[s, pl.ds(0, n8), :], y_hbm.at[pl.ds(0, n8), :], sem_s.at[s]).wait()

        def one_row(j, carry):
            scatter_row(0, s, 0).wait()
            return carry

        lax.fori_loop(0, n - n8, one_row, 0)

    @pl.when(jnp.logical_and(i == 0, f == 0))
    def _prologue():
        ybuf[...] = jnp.zeros_like(ybuf)

        def issue(j, carry):
            gather_row(src_ref[j], 0, j).start()
            return carry

        lax.fori_loop(0, tm, issue, 0)

    @pl.when(jnp.logical_and(f == 0, i <= nv))
    def _rows_ready():
        wait_gather(slot)

    @pl.when(jnp.logical_and(f == 0, jnp.logical_and(i >= 1, i <= nv)))
    def _slot_free():
        wait_scatter(slot, rows_ref[jnp.maximum(i - 1, 0)])

    @pl.when(valid)
    def _compute():
        hb = hbuf[slot].astype(BF16)
        r0 = f * half
        n_prev = rows_ref[i]
        for j in range(half):
            gather_row(src_ref[(i + 1) * tm + r0 + j], other, r0 + j).start()

            @pl.when(r0 + j < n_prev)
            def _():
                scatter_row(dst_ref[i * tm + r0 + j], other, r0 + j).start()
        act = (_silu(_dot(hb, wg_ref[...])) * _dot(hb, wu_ref[...])).astype(BF16)
        part = _dot(act, wd_ref[...])
        ybuf[slot] = jnp.where(f == 0, part, ybuf[slot] + part)

    @pl.when(jnp.logical_and(i == nv, f == 0))
    def _flush():
        n_last = rows_ref[i]

        def issue(j, carry):
            scatter_row(dst_ref[i * tm + j], other, j).start()
            return carry

        lax.fori_loop(0, n_last, issue, 0)
        wait_scatter(other, n_last)


def _moe_experts(tile_expert, src, dst, rows, n_valid, h, wg, wu, wd, n_tiles, y_rows):
    d = h.shape[1]
    tm = MOE_TILE
    nf = MOE_FCHUNKS
    fc = wg.shape[2] // nf

    def f_idx(i, f, nv):
        return jnp.where(i < nv[0], f, nf - 1)

    grid_spec = pltpu.PrefetchScalarGridSpec(
        num_scalar_prefetch=5,
        grid=(n_tiles + 1, nf),
        in_specs=[pl.BlockSpec(memory_space=pl.ANY),
                  pl.BlockSpec((None, d, fc), lambda i, f, te, src, dst, rows, nv: (te[i], 0, f_idx(i, f, nv))),
                  pl.BlockSpec((None, d, fc), lambda i, f, te, src, dst, rows, nv: (te[i], 0, f_idx(i, f, nv))),
                  pl.BlockSpec((None, fc, d), lambda i, f, te, src, dst, rows, nv: (te[i], f_idx(i, f, nv), 0))],
        out_specs=pl.BlockSpec(memory_space=pl.ANY),
        scratch_shapes=[pltpu.VMEM((2, tm, d), F32), pltpu.VMEM((2, tm, d), F32),
                        pltpu.SemaphoreType.DMA((2,)), pltpu.SemaphoreType.DMA((2,))],
    )
    return pl.pallas_call(
        _moe_kernel,
        grid_spec=grid_spec,
        out_shape=jax.ShapeDtypeStruct((y_rows, d), F32),
        compiler_params=_params(("arbitrary", "arbitrary")),
        name="moe_experts",
    )(tile_expert, src, dst, rows, n_valid, h, wg, wu, wd)


def _combine_kernel(x_ref, gate_ref, mf_ref, fg_ref, y1_ref, y2_ref, o_ref):
    w = mf_ref[...]
    y = w[:, 0:1] * y1_ref[...] + w[:, 1:2] * y2_ref[...]
    o_ref[...] = _rms(x_ref[...] + gate_ref[...] * y, fg_ref[...])


def _moe_combine(x, gate, mf, fg, y, tiles_per_batch, n_batch):
    t, d = x.shape
    tm = ROW_TILE
    row = lambda i: (i, 0)
    return pl.pallas_call(
        _combine_kernel,
        grid=(t // tm,),
        in_specs=[pl.BlockSpec((tm, d), row), _mod_spec(d, tiles_per_batch, n_batch), pl.BlockSpec((tm, LANES), row),
                  _full((1, d)), pl.BlockSpec((tm, d), row), pl.BlockSpec((tm, d), lambda i: (t // tm + i, 0))],
        out_specs=pl.BlockSpec((tm, d), row),
        out_shape=jax.ShapeDtypeStruct((t, d), F32),
        compiler_params=_params(("parallel",)),
        name="moe_combine",
    )(x, gate, mf, fg, y, y)


def kernel(x, c, ctx, c_ctx, ada_w, ada_b, norm_g, final_g, na_w_in, na_rpb, sg_w, sg_b, sg_norm_g, even_w_out,
           ffn_w_gate, ffn_w_up, ffn_w_down, mla_w_in, mla_q_norm_g, mla_kv_norm_g, mla_w_uq, mla_w_ukv, mla_w_out,
           moe_w_router, moe_b_router, moe_w_gate, moe_w_up, moe_w_down):
    n_batch, seq, d = x.shape
    n_ctx = ctx.shape[1]
    n_lat = n_batch * seq
    assert ada_w.shape[0] == 2 and seq % ROW_TILE == 0 and n_batch * n_ctx == ROW_TILE
    assert n_batch + 1 <= 8 and seq % GRID_W == 0
    tpb = seq // ROW_TILE

    cond8 = jnp.concatenate([c, c_ctx[None, :], jnp.zeros((8 - n_batch - 1, d), F32)], axis=0)
    mods = _ada_mod(cond8, ada_w, ada_b)[:, :n_batch + 1]
    mod = lambda layer, k: mods[layer, :, k * d:(k + 1) * d].reshape(n_batch + 1, 1, d)
    x_lat, x_ctx = x.reshape(n_lat, d), ctx.reshape(n_batch * n_ctx, d)

    w_in = na_w_in[0].astype(BF16)
    qt, k, vt, ug = _even_in(x_lat, x_ctx, norm_g[0, 0][None], mod(0, 0), mod(0, 1), w_in[:, :3 * A_WIDTH],
                             w_in[:, 3 * A_WIDTH:], tpb, n_batch)
    bias = _na_bias_table(na_rpb[0], seq // GRID_W)
    attn = _na_attention(qt, k, vt, bias, n_batch, seq, n_ctx)
    bmat = jnp.repeat(sg_b[0].T, SG_GROUP_DIM, axis=1)
    avg = jnp.asarray(np.kron(np.eye(SG_GROUPS), np.full((SG_GROUP_DIM, SG_GROUP_DIM), 1.0 / SG_GROUP_DIM)), BF16)
    gated = _spatial_gating(ug, sg_w[0].astype(BF16), bmat, sg_norm_g[0][None], avg)
    w_out = even_w_out[0].astype(BF16)
    xs = _ffn(x_lat, x_ctx, mod(0, 2), [attn, gated], [w_out[:A_WIDTH], w_out[A_WIDTH:]], norm_g[0, 1][None], mod(0, 3),
              mod(0, 4), mod(0, 5), ffn_w_gate[0].astype(BF16), ffn_w_up[0].astype(BF16), ffn_w_down[0].astype(BF16),
              tpb, n_batch)

    win, wuq_t, wuk, wuv_t = _mla_weights(mla_w_in[0], mla_w_uq[0], mla_w_ukv[0])
    cos, sin = _rope_tables(n_batch, seq, n_ctx)
    qt, k, vt = _mla_pre(xs, norm_g[1, 0][None], mod(1, 0), mod(1, 1), win, mla_q_norm_g[0][None],
                         mla_kv_norm_g[0][None], wuq_t, wuk, wuv_t, cos, sin, tpb, n_batch)
    vtc = jnp.transpose(vt[n_batch * tpb].reshape(-1, n_batch, n_ctx), (1, 0, 2))
    o = _mla_attention(qt, k, vt, vtc, n_batch, seq, n_ctx)
    wr = jnp.concatenate([moe_w_router[0], jnp.zeros((d, LANES - N_EXPERTS), F32)], axis=1)
    wr_hi = wr.astype(BF16)
    wr = jnp.concatenate([wr_hi, (wr - wr_hi.astype(F32)).astype(BF16)], axis=1)
    br = jnp.concatenate([moe_b_router[0], jnp.full((LANES - N_EXPERTS,), NEG, F32)])[None]
    tri = jnp.asarray(np.tril(np.ones((ROW_TILE, ROW_TILE), np.float32), -1), BF16)
    x1, h, mi, mf, cnt = _router(xs, mod(1, 2), o, mla_w_out[0].astype(BF16), norm_g[1, 1][None], mod(1, 3),
                                 mod(1, 4), wr, br, tri, n_lat, tpb, n_batch)

    counts = cnt[0, :N_EXPERTS].astype(jnp.int32)
    tiles_e = (counts + MOE_TILE - 1) // MOE_TILE
    tile_end = jnp.cumsum(tiles_e)
    start = (tile_end - tiles_e) * MOE_TILE
    slot12 = jnp.concatenate([start[mi[:, 0]] + mi[:, 2], start[mi[:, 1]] + mi[:, 3]])
    n_tiles = 2 * n_lat // MOE_TILE + N_EXPERTS
    n_slots = n_tiles * MOE_TILE
    dst = _slot_owner(slot12.astype(jnp.int32), n_slots)
    spare_tile = jnp.zeros((MOE_TILE,), jnp.int32)
    src = jnp.concatenate([dst % n_lat, spare_tile])
    dst = jnp.concatenate([spare_tile, dst])
    n_valid = tile_end[-1:]
    last_tile = jnp.maximum(n_valid[0] - 1, 0)
    all_tiles = jnp.arange(n_tiles + 1, dtype=jnp.int32)
    tile_ids = jnp.minimum(all_tiles, last_tile)
    tile_expert = jnp.minimum(jnp.sum((tile_ids[:, None] >= tile_end[None, :]).astype(jnp.int32), axis=1),
                              N_EXPERTS - 1)
    filled = counts[tile_expert] - (all_tiles - (tile_end - tiles_e)[tile_expert]) * MOE_TILE
    rows = jnp.where(all_tiles < n_valid[0], jnp.clip(filled, 0, MOE_TILE), 0)
    rows = jnp.concatenate([jnp.zeros((1,), jnp.int32), rows]).astype(jnp.int32)
    y = _moe_experts(tile_expert, src, dst, rows, n_valid.astype(jnp.int32), h, moe_w_gate[0].astype(BF16),
                     moe_w_up[0].astype(BF16), moe_w_down[0].astype(BF16), n_tiles, 2 * n_lat)
    out = _moe_combine(x1, mod(1, 5), mf, final_g[None], y, tpb, n_batch)
    return out.reshape(n_batch, seq, d)
```

```python
import functools

import numpy as np
import jax
import jax.numpy as jnp
from jax import lax
from jax.experimental import pallas as pl
from jax.experimental.pallas import tpu as pltpu

F32 = jnp.float32
BF16 = jnp.bfloat16
EPS = 1e-6
NEG = -1e30

LANES = 128
SUBLANES = 8
VMEM_LIMIT_BYTES = 56 * 1024 * 1024

GRID_W = 64
NA_HEADS = 8
NA_HEAD_DIM = 64
NA_WIN_R = 8
NA_WIN_C = 16
SG_GROUPS = 8
SG_GROUP_DIM = 64
SG_CHUNK = 128
A_WIDTH = NA_HEADS * NA_HEAD_DIM
B_WIDTH = SG_GROUPS * SG_GROUP_DIM
MLA_HEADS = 16
MLA_NOPE = 64
MLA_ROPE = 32
MLA_V = 64
MLA_Q_RANK = 384
MLA_KV_RANK = 256
MLA_SCALE = (MLA_NOPE + MLA_ROPE) ** -0.5
ROPE_BASE = 10000.0
N_EXPERTS = 8

ROW_TILE = 512
NA_QROWS = 4
NA_QB = NA_QROWS * GRID_W
NA_KB = 3 * NA_QB
NA_STEP_HEADS = 4
MLA_VROWS = 80
FLASH_QTILES = 4
LOG2E = 1.4426950408889634
MOE_TILE = 512
MOE_FCHUNKS = 2


def _params(sem):
    return pltpu.CompilerParams(dimension_semantics=sem, vmem_limit_bytes=VMEM_LIMIT_BYTES)


def _dot(a, b):
    return jnp.dot(a, b, preferred_element_type=F32)


def _silu(x):
    return x / (1.0 + jnp.exp(-x))


def _gelu_tanh(x):
    return 0.5 * x * (1.0 + jnp.tanh(0.7978845608028654 * (x + 0.044715 * (x * x * x))))


def _rms(x, g):
    return x * lax.rsqrt(jnp.mean(x * x, axis=-1, keepdims=True) + EPS) * g


def _norm_mod(x, g, sh, sc):
    return _rms(x, g) * (1.0 + sc) + sh


def _full(shape):
    n = len(shape)
    return pl.BlockSpec(shape, lambda *_: (0,) * n)


def _ada_kernel(cond_ref, w_ref, b_ref, o_ref):
    c = cond_ref[...]
    o_ref[...] = jnp.dot(_silu(c), w_ref[...], preferred_element_type=F32,
                         precision=lax.Precision.HIGHEST) + b_ref[...]


def _ada_mod(cond8, ada_w, ada_b):
    depth, d, n = ada_w.shape
    tn = n // 4
    return pl.pallas_call(
        _ada_kernel,
        grid=(depth, n // tn),
        in_specs=[_full((SUBLANES, d)),
                  pl.BlockSpec((None, d, tn), lambda l, j: (l, 0, j)),
                  pl.BlockSpec((None, 1, tn), lambda l, j: (l, 0, j))],
        out_specs=pl.BlockSpec((None, SUBLANES, tn), lambda l, j: (l, 0, j)),
        out_shape=jax.ShapeDtypeStruct((depth, SUBLANES, n), F32),
        compiler_params=_params(("parallel", "parallel")),
        name="ada_mod",
    )(cond8, ada_w, ada_b.reshape(depth, 1, n))


def _group_map(tiles_per_batch, n_batch):
    return lambda t: (jnp.minimum(t // tiles_per_batch, n_batch), 0, 0)


def _mod_spec(d, tiles_per_batch, n_batch):
    return pl.BlockSpec((None, 1, d), _group_map(tiles_per_batch, n_batch))


def _stream_specs(x_lat, x_ctx):
    tm, d = ROW_TILE, x_lat.shape[1]
    assert x_ctx.shape[0] == tm
    last = x_lat.shape[0] // tm - 1
    return [pl.BlockSpec((tm, d), lambda i: (jnp.minimum(i, last), 0)), pl.BlockSpec((tm, d), lambda i: (0, 0))]


def _stream_tile(xl_ref, xc_ref, n_lat_tiles):
    return jnp.where(pl.program_id(0) < n_lat_tiles, xl_ref[...], xc_ref[...])


def _even_in_kernel(n_lat_tiles, xl_ref, xc_ref, g_ref, sh_ref, sc_ref, wqkv_ref, wug_ref, qt_ref, k_ref, vt_ref,
                    ug_ref):
    x = _stream_tile(xl_ref, xc_ref, n_lat_tiles)
    h = _norm_mod(x, g_ref[...], sh_ref[...], sc_ref[...]).astype(BF16)
    qkv = _dot(h, wqkv_ref[...])
    qt_ref[...] = (qkv[:, :A_WIDTH] * (NA_HEAD_DIM ** -0.5 * LOG2E)).T.astype(qt_ref.dtype)
    k_ref[...] = qkv[:, A_WIDTH:2 * A_WIDTH].astype(k_ref.dtype)
    vt_ref[...] = qkv[:, 2 * A_WIDTH:].T.astype(vt_ref.dtype)
    ug_ref[...] = _dot(h, wug_ref[...])


def _even_in(x_lat, x_ctx, g, sh, sc, wqkv, wug, tiles_per_batch, n_batch):
    d = x_lat.shape[1]
    t = x_lat.shape[0] + x_ctx.shape[0]
    tm = ROW_TILE
    mod = _mod_spec(d, tiles_per_batch, n_batch)
    return pl.pallas_call(
        functools.partial(_even_in_kernel, x_lat.shape[0] // tm),
        grid=(t // tm,),
        in_specs=_stream_specs(x_lat, x_ctx) + [_full((1, d)), mod, mod, _full(wqkv.shape), _full(wug.shape)],
        out_specs=[pl.BlockSpec((A_WIDTH, tm), lambda i: (0, i)), pl.BlockSpec((tm, A_WIDTH), lambda i: (i, 0)),
                   pl.BlockSpec((A_WIDTH, tm), lambda i: (0, i)), pl.BlockSpec((tm, wug.shape[1]), lambda i: (i, 0))],
        out_shape=[jax.ShapeDtypeStruct((A_WIDTH, t), BF16), jax.ShapeDtypeStruct((t, A_WIDTH), BF16),
                   jax.ShapeDtypeStruct((A_WIDTH, t), BF16), jax.ShapeDtypeStruct((t, wug.shape[1]), F32)],
        compiler_params=_params(("parallel",)),
        name="even_in",
    )(x_lat, x_ctx, g, sh, sc, wqkv, wug)


def _na_bias_table(rpb, rows):
    nblk = rows // NA_QROWS
    n_heads = rpb.shape[0]
    win_rows = NA_KB // GRID_W
    qc = np.arange(GRID_W)
    col_start = np.clip(qc - NA_WIN_C // 2, 0, GRID_W - NA_WIN_C)
    col_ok = (qc[None, :] >= col_start[:, None]) & (qc[None, :] < col_start[:, None] + NA_WIN_C)
    col_j = qc[None, :] - qc[:, None] + NA_WIN_C - 1
    sel_c = ((col_j[None] == np.arange(2 * NA_WIN_C - 1)[:, None, None]) & col_ok[None]).astype(np.float32)
    toeplitz = jnp.einsum("hij,jqk->hikq", rpb, sel_c, precision=lax.Precision.HIGHEST) * LOG2E
    toeplitz = jnp.where(col_ok.T[None, None], toeplitz, NEG)
    n_rel = 2 * NA_WIN_R - 1
    toeplitz = jnp.concatenate([toeplitz, jnp.full((n_heads, 1, GRID_W, GRID_W), NEG, F32)], axis=1)
    rel = np.full((3, NA_QROWS, win_rows), n_rel, np.int32)
    for kind, j in enumerate((0, 1, nblk - 1)):
        first_row = NA_QROWS * int(np.clip(j - 1, 0, nblk - 3))
        for a in range(NA_QROWS):
            r = NA_QROWS * j + a
            row_start = int(np.clip(r - NA_WIN_R // 2, 0, rows - NA_WIN_R))
            for b in range(win_rows):
                kr = first_row + b
                if row_start <= kr < row_start + NA_WIN_R:
                    rel[kind, a, b] = kr - r + NA_WIN_R - 1
    grid_spec = pltpu.PrefetchScalarGridSpec(
        num_scalar_prefetch=1,
        grid=(3, n_heads),
        in_specs=[pl.BlockSpec((None, n_rel + 1, GRID_W, GRID_W), lambda c, h, rel: (h, 0, 0, 0))],
        out_specs=pl.BlockSpec((None, None, NA_KB, NA_QB), lambda c, h, rel: (c, h, 0, 0)),
    )
    return pl.pallas_call(
        _na_bias_kernel,
        grid_spec=grid_spec,
        out_shape=jax.ShapeDtypeStruct((3, n_heads, NA_KB, NA_QB), F32),
        compiler_params=_params(("parallel", "parallel")),
        name="na_bias",
    )(jnp.asarray(rel.reshape(-1)), toeplitz)


def _na_bias_kernel(rel_ref, t_ref, o_ref):
    kind = pl.program_id(0)
    win_rows = NA_KB // GRID_W
    for b in range(win_rows):
        for ap in range(NA_QROWS // 2):
            rel = [rel_ref[(kind * NA_QROWS + 2 * ap + s) * win_rows + b] for s in range(2)]
            pair = jnp.concatenate([t_ref[rel[0]], t_ref[rel[1]]], axis=1)
            o_ref[GRID_W * b:GRID_W * (b + 1), 2 * GRID_W * ap:2 * GRID_W * (ap + 1)] = pair


def _col_reduce(x_t, combine, reduce):
    slab = 8 * SUBLANES
    parts = [x_t[r:r + slab] for r in range(0, x_t.shape[0], slab)]
    while len(parts) > 1:
        parts = [combine(parts[j], parts[j + 1]) for j in range(0, len(parts) - 1, 2)] + parts[len(parts) & ~1:]
    return reduce(parts[0], axis=0, keepdims=True)


def _col_max(s_t):
    return _col_reduce(s_t, jnp.maximum, jnp.max)


def _col_sum(p_t):
    return _col_reduce(p_t, jnp.add, jnp.sum)


def _na_kernel(qt_ref, k0_ref, k1_ref, k2_ref, vt0_ref, vt1_ref, vt2_ref, kc_ref, vtc_ref, bias_ref, o_ref):
    jj = pl.program_id(2)
    row = lax.broadcasted_iota(jnp.int32, (NA_STEP_HEADS * NA_HEAD_DIM, 1), 0)
    qt = qt_ref[...]
    zero = jnp.zeros_like(qt)
    heads = range(NA_STEP_HEADS)

    def head_q(hh):
        mine = jnp.logical_and(row >= NA_HEAD_DIM * hh, row < NA_HEAD_DIM * (hh + 1))
        return jnp.where(mine, qt, zero)

    def attend(s_t, vt_refs, hh):
        m = functools.reduce(jnp.maximum, [_col_max(x) for x in s_t])
        p_t = [jnp.exp2(x - m) for x in s_t]
        l = functools.reduce(jnp.add, [_col_sum(x) for x in p_t])
        vrows = slice(NA_HEAD_DIM * hh, NA_HEAD_DIM * (hh + 1))
        o_t = functools.reduce(jnp.add, [_dot(vt[vrows, :], x.astype(BF16)) for vt, x in zip(vt_refs, p_t)])
        return o_t / l

    def finish(outs):
        o_ref[...] = jnp.concatenate(outs, axis=0).T.astype(o_ref.dtype)

    @pl.when(jj == 0)
    def _ctx_queries():
        finish([attend([_dot(kc_ref[...], head_q(hh))], [vtc_ref], hh) for hh in heads])

    @pl.when(jj > 0)
    def _latent_queries():
        k_refs = (k0_ref, k1_ref, k2_ref)
        s_all = []
        for hh in heads:
            qh = head_q(hh)
            s_t = [_dot(k_refs[i][...], qh) + bias_ref[hh, NA_QB * i:NA_QB * (i + 1), :] for i in range(3)]
            s_all.append(s_t + [_dot(kc_ref[...], qh)])
        finish([attend(s_all[hh], (vt0_ref, vt1_ref, vt2_ref, vtc_ref), hh) for hh in heads])


def _na_attention(qt, k, vt, bias, n_batch, seq, n_ctx):
    t = k.shape[0]
    assert n_ctx == NA_QB and seq % NA_QB == 0
    nblk = seq // NA_QB
    assert nblk >= 3
    hp = NA_HEADS // NA_STEP_HEADS
    width = NA_STEP_HEADS * NA_HEAD_DIM
    ctx_blk = n_batch * nblk

    def q_blk(b, j):
        return jnp.where(j == 0, ctx_blk + b, b * nblk + j - 1)

    def win_blk(b, j, i):
        return b * nblk + jnp.clip(j - 2, 0, nblk - 3) + i

    def bias_map(b, h, j):
        return (jnp.where(j <= 1, 0, jnp.where(j == nblk, 2, 1)), h, 0, 0)

    tok, feat = (NA_QB, width), (width, NA_QB)
    in_specs = ([pl.BlockSpec(feat, lambda b, h, j: (h, q_blk(b, j)))]
                + [pl.BlockSpec(tok, functools.partial(lambda i, b, h, j: (win_blk(b, j, i), h), i)) for i in range(3)]
                + [pl.BlockSpec(feat, functools.partial(lambda i, b, h, j: (h, win_blk(b, j, i)), i)) for i in range(3)]
                + [pl.BlockSpec(tok, lambda b, h, j: (ctx_blk + b, h)),
                   pl.BlockSpec(feat, lambda b, h, j: (h, ctx_blk + b)),
                   pl.BlockSpec((None, NA_STEP_HEADS, NA_KB, NA_QB), bias_map)])
    return pl.pallas_call(
        _na_kernel,
        grid=(n_batch, hp, nblk + 1),
        in_specs=in_specs,
        out_specs=pl.BlockSpec(tok, lambda b, h, j: (q_blk(b, j), h)),
        out_shape=jax.ShapeDtypeStruct((t, A_WIDTH), BF16),
        compiler_params=_params(("parallel", "parallel", "arbitrary")),
        name="na_attention",
    )(qt, k, k, k, vt, vt, vt, k, vt, bias)


def _seg_mean(x, avg):
    hi = x.astype(BF16)
    lo = (x - hi.astype(F32)).astype(BF16)
    return _dot(hi, avg) + _dot(lo, avg)


def _sg_kernel(ug_ref, ws_ref, bm_ref, ng_ref, avg_ref, o_ref):
    lane = lax.broadcasted_iota(jnp.int32, (1, LANES), 1)
    first = lane < SG_GROUP_DIM
    avg = avg_ref[...]
    g = _gelu_tanh(ug_ref[:, B_WIDTH:2 * B_WIDTH])
    d = g - _seg_mean(g, avg)
    var = _seg_mean(d * d, avg)
    gn = (d * lax.rsqrt(var + EPS) * ng_ref[...]).astype(BF16)
    for c in range(ROW_TILE // SG_CHUNK):
        rows = slice(SG_CHUNK * c, SG_CHUNK * (c + 1))
        parts = []
        for j in range(SG_GROUPS // 2):
            gj = gn[rows, LANES * j:LANES * (j + 1)]
            parts.append(jnp.where(first, _dot(ws_ref[2 * j], gj), _dot(ws_ref[2 * j + 1], gj)))
        mixed = jnp.concatenate(parts, axis=1) + bm_ref[...]
        o_ref[rows, :] = (_gelu_tanh(ug_ref[rows, 0:B_WIDTH]) * mixed).astype(o_ref.dtype)


def _spatial_gating(ug, ws, bmat, ng, avg):
    t = ug.shape[0]
    tm = ROW_TILE
    return pl.pallas_call(
        _sg_kernel,
        grid=(t // tm,),
        in_specs=[pl.BlockSpec((tm, 2 * B_WIDTH), lambda i: (i, 0)), _full(ws.shape), _full(bmat.shape),
                  _full(ng.shape), _full(avg.shape)],
        out_specs=pl.BlockSpec((tm, B_WIDTH), lambda i: (i, 0)),
        out_shape=jax.ShapeDtypeStruct((t, B_WIDTH), BF16),
        compiler_params=_params(("parallel",)),
        name="spatial_gating",
    )(ug, ws, bmat, ng, avg)


def _mixer_residual(x, gate_ref, a_refs, w_refs):
    y = functools.reduce(jnp.add, [_dot(a[...], w[...]) for a, w in zip(a_refs, w_refs)])
    return x + gate_ref[...] * y


def _ffn_kernel(n_chunks, n_in, n_lat_tiles, xl_ref, xc_ref, gate_m_ref, g_ref, sh_ref, sc_ref, gate_ref, *refs):
    a_refs, w_refs = refs[:n_in], refs[n_in:2 * n_in]
    wg_ref, wu_ref, wd_ref, o_ref = refs[2 * n_in:]
    x = _mixer_residual(_stream_tile(xl_ref, xc_ref, n_lat_tiles), gate_m_ref, a_refs, w_refs)
    h = _norm_mod(x, g_ref[...], sh_ref[...], sc_ref[...]).astype(BF16)
    fc = wg_ref.shape[1] // n_chunks
    acc = None
    for c in range(n_chunks):
        cs = slice(fc * c, fc * (c + 1))
        act = (_silu(_dot(h, wg_ref[:, cs])) * _dot(h, wu_ref[:, cs])).astype(BF16)
        part = _dot(act, wd_ref[cs, :])
        acc = part if acc is None else acc + part
    o_ref[...] = x + gate_ref[...] * acc


def _ffn(x_lat, x_ctx, gate_m, acts, weights, g, sh, sc, gate, wg, wu, wd, tiles_per_batch, n_batch):
    d = x_lat.shape[1]
    t = x_lat.shape[0] + x_ctx.shape[0]
    tm = ROW_TILE
    f = wg.shape[1]
    n_chunks = 2 if f % (2 * LANES) == 0 else 1
    mod = _mod_spec(d, tiles_per_batch, n_batch)
    row = lambda i: (i, 0)
    resident = lambda w: pl.BlockSpec(w.shape, lambda i: (0, 0), pipeline_mode=pl.Buffered(1))
    return pl.pallas_call(
        functools.partial(_ffn_kernel, n_chunks, len(acts), x_lat.shape[0] // tm),
        grid=(t // tm,),
        in_specs=(_stream_specs(x_lat, x_ctx) + [mod, _full((1, d)), mod, mod, mod]
                  + [pl.BlockSpec((tm, a.shape[1]), row) for a in acts] + [resident(w) for w in weights]
                  + [resident(wg), resident(wu), resident(wd)]),
        out_specs=pl.BlockSpec((tm, d), row),
        out_shape=jax.ShapeDtypeStruct((t, d), F32),
        compiler_params=_params(("parallel",)),
        name="ffn",
    )(x_lat, x_ctx, gate_m, g, sh, sc, gate, *acts, *weights, wg, wu, wd)


def _rope_tables(n_batch, seq, n_ctx):
    t = np.arange(seq)
    pos = np.stack([t // GRID_W, t % GRID_W], axis=-1).astype(np.float32)
    n_freq = MLA_ROPE // 4
    inv = jnp.power(ROPE_BASE, -jnp.arange(n_freq, dtype=F32) / n_freq)
    ang = jnp.asarray(pos)[:, :, None] * inv
    cos, sin = jnp.cos(ang), jnp.sin(ang)
    cos_r = jnp.concatenate([cos, cos], axis=-1).reshape(seq, MLA_ROPE)
    sin_r = jnp.concatenate([-sin, sin], axis=-1).reshape(seq, MLA_ROPE)
    ones = jnp.ones((seq, MLA_NOPE), F32)
    pad = LANES - MLA_NOPE - MLA_ROPE
    cos_l = jnp.concatenate([ones, cos_r, jnp.ones((seq, pad), F32)], axis=-1)
    sin_l = jnp.concatenate([0 * ones, sin_r, jnp.zeros((seq, pad), F32)], axis=-1)
    cos_all = jnp.concatenate([jnp.tile(cos_l, (n_batch, 1)), jnp.ones((n_batch * n_ctx, LANES), F32)])
    sin_all = jnp.concatenate([jnp.tile(sin_l, (n_batch, 1)), jnp.zeros((n_batch * n_ctx, LANES), F32)])
    return cos_all, sin_all


def _swap_rope_cols(w_rope):
    q = MLA_ROPE // 4
    return jnp.concatenate([w_rope[..., q:2 * q], w_rope[..., 0:q], w_rope[..., 3 * q:4 * q],
                            w_rope[..., 2 * q:3 * q]], axis=-1)


def _mla_weights(w_in, w_uq, w_ukv):
    d = w_in.shape[0]
    pad = LANES - MLA_NOPE - MLA_ROPE
    w_kr = w_in[:, MLA_Q_RANK + MLA_KV_RANK:]
    lanes_kr = lambda w: jnp.concatenate([jnp.zeros((d, MLA_NOPE), F32), w, jnp.zeros((d, pad), F32)], axis=1)
    win = jnp.concatenate([w_in[:, :MLA_Q_RANK + MLA_KV_RANK], lanes_kr(w_kr), lanes_kr(_swap_rope_cols(w_kr))],
                          axis=1)
    uq = w_uq.reshape(MLA_Q_RANK, MLA_HEADS, MLA_NOPE + MLA_ROPE)
    zq = jnp.zeros((MLA_Q_RANK, MLA_HEADS, pad), F32)
    uq_a = jnp.concatenate([uq, zq], axis=-1).reshape(MLA_Q_RANK, MLA_HEADS * LANES)
    uq_b = _swap_rope_cols(uq[..., MLA_NOPE:]).reshape(MLA_Q_RANK, MLA_HEADS * MLA_ROPE)
    wuq_t = jnp.concatenate([uq_a, uq_b], axis=1).T
    ukv = w_ukv.reshape(MLA_KV_RANK, MLA_HEADS, MLA_NOPE + MLA_V)
    zk = jnp.zeros((MLA_KV_RANK, MLA_HEADS, LANES - MLA_NOPE), F32)
    uk = jnp.concatenate([ukv[..., :MLA_NOPE], zk], axis=-1).reshape(MLA_KV_RANK, MLA_HEADS * LANES)
    uv_t = jnp.transpose(ukv[..., MLA_NOPE:], (1, 2, 0))
    uv_t = jnp.concatenate([uv_t, jnp.zeros((MLA_HEADS, MLA_VROWS - MLA_V, MLA_KV_RANK), F32)], axis=1)
    return (win.astype(BF16), wuq_t.astype(BF16), uk.astype(BF16),
            uv_t.reshape(MLA_HEADS * MLA_VROWS, MLA_KV_RANK).astype(BF16))


def _mla_pre_kernel(x_ref, g_ref, sh_ref, sc_ref, win_ref, qg_ref, kvg_ref, wuqt_ref, wuk_ref, wuvt_ref,
                    cos_ref, sin_ref, cost_ref, sint_ref, qt_ref, k_ref, vt_ref):
    h = _norm_mod(x_ref[...], g_ref[...], sh_ref[...], sc_ref[...]).astype(BF16)
    p = _dot(h, win_ref[...])
    cq = _rms(p[:, :MLA_Q_RANK], qg_ref[...])
    ckv = _rms(p[:, MLA_Q_RANK:MLA_Q_RANK + MLA_KV_RANK], kvg_ref[...])
    cq_t = cq.T.astype(BF16)
    ckv_t = ckv.T.astype(BF16)
    ckv_b = ckv.astype(BF16)
    o_kr = MLA_Q_RANK + MLA_KV_RANK
    kr_rot = p[:, o_kr:o_kr + LANES] * cos_ref[...] + p[:, o_kr + LANES:o_kr + 2 * LANES] * sin_ref[...]
    kr2 = jnp.concatenate([kr_rot, kr_rot], axis=1)
    rope = slice(MLA_NOPE, MLA_NOPE + MLA_ROPE)
    cos_r, sin_r = cost_ref[rope, :], sint_ref[rope, :]
    half = MLA_HEADS * LANES
    q_swapped = _dot(wuqt_ref[half:half + MLA_HEADS * MLA_ROPE, :], cq_t)
    for c in range(MLA_HEADS // 2):
        rs = slice(2 * LANES * c, 2 * LANES * (c + 1))
        q_pair = _dot(wuqt_ref[rs, :], cq_t)
        rows = []
        for hh in range(2):
            q_h = q_pair[LANES * hh:LANES * (hh + 1)]
            swapped = q_swapped[MLA_ROPE * (2 * c + hh):MLA_ROPE * (2 * c + hh + 1)]
            rows += [q_h[:MLA_NOPE], q_h[rope] * cos_r + swapped * sin_r, q_h[MLA_NOPE + MLA_ROPE:]]
        q_t = jnp.concatenate(rows, axis=0)
        qt_ref[rs, :] = (q_t * (MLA_SCALE * LOG2E)).astype(qt_ref.dtype)
        k_ref[:, rs] = (_dot(ckv_b, wuk_ref[:, rs]) + kr2).astype(k_ref.dtype)
    group = 4 * MLA_VROWS
    row = lax.broadcasted_iota(jnp.int32, (group, 1), 0)
    ones_rows = functools.reduce(jnp.add, [jnp.where(row == MLA_VROWS * j + MLA_V, 1.0, 0.0) for j in range(4)])
    for c in range(MLA_HEADS // 4):
        rv = slice(group * c, group * (c + 1))
        vt_ref[rv, :] = (_dot(wuvt_ref[rv, :], ckv_t) + ones_rows).astype(vt_ref.dtype)


def _mla_pre(x, g, sh, sc, win, qg, kvg, wuq_t, wuk, wuv_t, cos, sin, tiles_per_batch, n_batch):
    t, d = x.shape
    tm = ROW_TILE
    n = MLA_HEADS * LANES
    nv = MLA_HEADS * MLA_VROWS
    mod = _mod_spec(d, tiles_per_batch, n_batch)
    row = lambda i: (i, 0)
    col = lambda i: (0, i)
    return pl.pallas_call(
        _mla_pre_kernel,
        grid=(t // tm,),
        in_specs=[pl.BlockSpec((tm, d), row), _full((1, d)), mod, mod, _full(win.shape), _full(qg.shape),
                  _full(kvg.shape), _full(wuq_t.shape), _full(wuk.shape), _full(wuv_t.shape),
                  pl.BlockSpec((tm, LANES), row), pl.BlockSpec((tm, LANES), row),
                  pl.BlockSpec((LANES, tm), col), pl.BlockSpec((LANES, tm), col)],
        out_specs=[pl.BlockSpec((None, n, tm), lambda i: (i, 0, 0)), pl.BlockSpec((tm, n), row),
                   pl.BlockSpec((None, nv, tm), lambda i: (i, 0, 0))],
        out_shape=[jax.ShapeDtypeStruct((t // tm, n, tm), BF16), jax.ShapeDtypeStruct((t, n), BF16),
                   jax.ShapeDtypeStruct((t // tm, nv, tm), BF16)],
        compiler_params=_params(("parallel",)),
        name="mla_pre",
    )(x, g, sh, sc, win, qg, kvg, wuq_t, wuk, wuv_t, cos, sin, cos.T, sin.T)


def _flash_kernel(qts_ref, k_ref, vt_ref, kc_ref, vtc_ref, os_ref, *s_refs):
    for t in range(qts_ref.shape[0]):
        _flash_tile(qts_ref.at[t], k_ref, vt_ref, kc_ref, vtc_ref, os_ref.at[pl.ds(t * qts_ref.shape[2], qts_ref.shape[2])],
                    *s_refs[4 * t:4 * t + 4])


def _flash_tile(qt_ref, k_ref, vt_ref, kc_ref, vtc_ref, o_ref, sa0_ref, sa1_ref, sb0_ref, sb1_ref):
    tq = qt_ref.shape[1]
    n_blocks, _, tk = vt_ref.shape
    n_ctx = kc_ref.shape[0]
    heads = range(2)
    s_a, s_b = (sa0_ref, sa1_ref), (sb0_ref, sb1_ref)
    lanes = lambda hh: slice(LANES * hh, LANES * (hh + 1))
    vrows = lambda hh: slice(MLA_VROWS * hh, MLA_VROWS * (hh + 1))

    def scores(kb_of, bufs, rows):
        cmax = []
        for hh in heads:
            s_t = _dot(kb_of(hh), qt_ref[lanes(hh), :])
            bufs[hh][0:rows, :] = s_t
            cmax.append(_col_max(s_t))
        return tuple(cmax)

    def absorb(bufs, rows, cmax, vt_of, state):
        new = []
        for hh in heads:
            m, acc = state[hh]
            m_new = jnp.maximum(m, cmax[hh])
            p_t = jnp.exp2(bufs[hh][0:rows, :] - m_new).astype(BF16)
            new.append((m_new, acc * jnp.exp2(m - m_new) + _dot(vt_of(hh), p_t)))
        return tuple(new)

    def lat_k(blk):
        off = blk * tk if isinstance(blk, int) else pl.multiple_of(blk * tk, tk)
        return lambda hh: k_ref[pl.ds(off, tk), lanes(hh)]

    lat_v = lambda blk: (lambda hh: vt_ref[blk, vrows(hh), :])
    ctx_k = lambda hh: kc_ref[:, lanes(hh)]
    ctx_v = lambda hh: vtc_ref[vrows(hh), :]

    def pair(blk, cmax_a, state, last):
        cmax_b = scores(lat_k(blk + 1), s_b, tk)
        state = absorb(s_a, tk, cmax_a, lat_v(blk), state)
        cmax_a = scores(ctx_k, s_a, n_ctx) if last else scores(lat_k(blk + 2), s_a, tk)
        return cmax_a, absorb(s_b, tk, cmax_b, lat_v(blk + 1), state)

    def two_pairs(j, carry):
        return pair(4 * j + 2, *pair(4 * j, *carry, last=False), last=False)

    init = tuple((jnp.full((1, tq), NEG, F32), jnp.zeros((MLA_VROWS, tq), F32)) for _ in heads)
    carry = (scores(lat_k(0), s_a, tk), init)
    inner_pairs = n_blocks // 2 - 1
    carry = lax.fori_loop(0, inner_pairs // 2, two_pairs, carry)
    if inner_pairs % 2:
        carry = pair(n_blocks - 4, *carry, last=False)
    cmax_a, state = pair(n_blocks - 2, *carry, last=True)
    state = absorb(s_a, n_ctx, cmax_a, ctx_v, state)
    outs = [acc[:MLA_V] / acc[MLA_V:MLA_V + 1] for _, acc in state]
    o_ref[...] = jnp.concatenate(outs, axis=0).T.astype(o_ref.dtype)


def _mla_attention(qt, k, vt, vtc, n_batch, seq, n_ctx):
    tq = ROW_TILE
    nt = FLASH_QTILES
    assert seq % (nt * tq) == 0
    nq = seq // (nt * tq)
    ctx0 = n_batch * seq // n_ctx
    return pl.pallas_call(
        _flash_kernel,
        grid=(n_batch, MLA_HEADS // 2, nq),
        in_specs=[pl.BlockSpec((nt, 2 * LANES, tq), lambda b, h, i: (b * nq + i, h, 0)),
                  pl.BlockSpec((seq, 2 * LANES), lambda b, h, i: (b, h)),
                  pl.BlockSpec((seq // tq, 2 * MLA_VROWS, tq), lambda b, h, i: (b, h, 0)),
                  pl.BlockSpec((n_ctx, 2 * LANES), lambda b, h, i: (ctx0 + b, h)),
                  pl.BlockSpec((None, 2 * MLA_VROWS, n_ctx), lambda b, h, i: (b, h, 0))],
        out_specs=pl.BlockSpec((nt * tq, LANES), lambda b, h, i: (b * nq + i, h)),
        out_shape=jax.ShapeDtypeStruct((n_batch * seq, MLA_HEADS * MLA_V), BF16),
        scratch_shapes=[pltpu.VMEM((tq, tq), F32)] * (4 * nt),
        compiler_params=_params(("parallel", "parallel", "arbitrary")),
        name="mla_flash",
    )(qt, k, vt, k, vtc)


def _router_kernel(x_ref, gate_m_ref, o_ref, wo_ref, g_ref, sh_ref, sc_ref, wr_ref, br_ref, tri_ref,
                   x1_ref, h_ref, mi_ref, mf_ref, cnt_ref, run_ref):
    @pl.when(pl.program_id(0) == 0)
    def _init():
        run_ref[...] = jnp.zeros_like(run_ref)

    x1 = _mixer_residual(x_ref[...], gate_m_ref, [o_ref], [wo_ref])
    x1_ref[...] = x1
    h = _norm_mod(x1, g_ref[...], sh_ref[...], sc_ref[...])
    h_ref[...] = h
    h_hi = h.astype(BF16)
    h_lo = (h - h_hi.astype(F32)).astype(BF16)
    hw = _dot(h_hi, wr_ref[...])
    logits = hw[:, :LANES] + hw[:, LANES:] + _dot(h_lo, wr_ref[:, :LANES]) + br_ref[...]
    lane = lax.broadcasted_iota(jnp.int32, logits.shape, 1).astype(F32)
    m1 = jnp.max(logits, axis=-1, keepdims=True)
    i1 = jnp.min(jnp.where(logits == m1, lane, float(LANES)), axis=-1, keepdims=True)
    rest = jnp.where(lane == i1, NEG, logits)
    m2 = jnp.max(rest, axis=-1, keepdims=True)
    i2 = jnp.min(jnp.where(rest == m2, lane, float(LANES)), axis=-1, keepdims=True)
    e = jnp.exp(m2 - m1)
    w1 = 1.0 / (1.0 + e)
    w2 = e / (1.0 + e)
    hit1, hit2 = lane == i1, lane == i2
    onehot = jnp.where(jnp.logical_or(hit1, hit2), 1.0, 0.0)
    before = _dot(tri_ref[...], onehot.astype(BF16)) + run_ref[0:1, :]
    r1 = jnp.sum(jnp.where(hit1, before, 0.0), axis=-1, keepdims=True)
    r2 = jnp.sum(jnp.where(hit2, before, 0.0), axis=-1, keepdims=True)
    run_ref[...] = run_ref[...] + jnp.sum(onehot, axis=0, keepdims=True)
    meta = jnp.where(lane == 0.0, i1, jnp.where(lane == 1.0, i2, jnp.where(lane == 2.0, r1, jnp.where(lane == 3.0, r2, 0.0))))
    mi_ref[...] = meta.astype(jnp.int32)
    mf_ref[...] = jnp.where(lane == 0.0, w1, jnp.where(lane == 1.0, w2, 0.0))
    cnt_ref[...] = run_ref[...]


def _router(x, gate_m, o, wo, g, sh, sc, wr, br, tri, n_rows, tiles_per_batch, n_batch):
    d = x.shape[1]
    tm = ROW_TILE
    mod = _mod_spec(d, tiles_per_batch, n_batch)
    row = lambda i: (i, 0)
    return pl.pallas_call(
        _router_kernel,
        grid=(n_rows // tm,),
        in_specs=[pl.BlockSpec((tm, d), row), mod, pl.BlockSpec((tm, o.shape[1]), row), _full(wo.shape),
                  _full((1, d)), mod, mod, _full(wr.shape), _full(br.shape), _full(tri.shape)],
        out_specs=[pl.BlockSpec((tm, d), row), pl.BlockSpec((tm, d), row), pl.BlockSpec((tm, LANES), row),
                   pl.BlockSpec((tm, LANES), row), _full((SUBLANES, LANES))],
        out_shape=[jax.ShapeDtypeStruct((n_rows, d), F32), jax.ShapeDtypeStruct((n_rows, d), F32),
                   jax.ShapeDtypeStruct((n_rows, LANES), jnp.int32), jax.ShapeDtypeStruct((n_rows, LANES), F32),
                   jax.ShapeDtypeStruct((SUBLANES, LANES), F32)],
        scratch_shapes=[pltpu.VMEM((SUBLANES, LANES), F32)],
        compiler_params=_params(("arbitrary",)),
        name="moe_router",
    )(x, gate_m, o, wo, g, sh, sc, wr, br, tri)


def _slot_owner_kernel(slot_ref, zeros_hbm, owner_ref, sem):
    fill = pltpu.make_async_copy(zeros_hbm, owner_ref, sem)
    fill.start()
    fill.wait()

    def claim(j, carry):
        owner_ref[slot_ref[j]] = j
        return carry

    lax.fori_loop(0, slot_ref.shape[0], claim, 0, unroll=32)


def _slot_owner(slots, n_slots):
    smem = pl.BlockSpec(memory_space=pltpu.SMEM)
    return pl.pallas_call(
        _slot_owner_kernel,
        in_specs=[smem, pl.BlockSpec(memory_space=pl.ANY)],
        out_specs=smem,
        out_shape=jax.ShapeDtypeStruct((n_slots,), jnp.int32),
        scratch_shapes=[pltpu.SemaphoreType.DMA(())],
        name="moe_slot_owner",
    )(slots, jnp.zeros((n_slots,), jnp.int32))


def _moe_kernel(te_ref, src_ref, dst_ref, rows_ref, nv_ref, h_hbm, wg_ref, wu_ref, wd_ref, y_hbm, hbuf, ybuf,
                sem_g, sem_s):
    i, f = pl.program_id(0), pl.program_id(1)
    _, tm, _ = hbuf.shape
    half = tm // MOE_FCHUNKS
    nv = nv_ref[0]
    valid = i < nv
    slot = lax.rem(i, 2)
    other = 1 - slot

    def gather_row(idx, s, r):
        return pltpu.make_async_copy(h_hbm.at[pl.ds(idx, 1), :], hbuf.at[s, pl.ds(r, 1), :], sem_g.at[s])

    def scatter_row(idx, s, r):
        return pltpu.make_async_copy(ybuf.at[s, pl.ds(r, 1), :], y_hbm.at[pl.ds(idx, 1), :], sem_s.at[s])

    def wait_gather(s):
        pltpu.make_async_copy(h_hbm.at[pl.ds(0, tm), :], hbuf.at[s], sem_g.at[s]).wait()

    def wait_scatter(s, n):
        n8 = pl.multiple_of((n // 8) * 8, 8)

        @pl.when(n8 > 0)
        def _():
            pltpu.make_async_copy(ybuf.at[s, pl.ds(0, n8), :], y_hbm.at[pl.ds(0, n8), :], sem_s.at[s]).wait()

        def one_row(j, carry):
            scatter_row(0, s, 0).wait()
            return carry

        lax.fori_loop(0, n - n8, one_row, 0)

    @pl.when(jnp.logical_and(i == 0, f == 0))
    def _prologue():
        ybuf[...] = jnp.zeros_like(ybuf)

        def issue(j, carry):
            gather_row(src_ref[j], 0, j).start()
            return carry

        lax.fori_loop(0, tm, issue, 0)

    @pl.when(jnp.logical_and(f == 0, i <= nv))
    def _rows_ready():
        wait_gather(slot)

    @pl.when(jnp.logical_and(f == 0, jnp.logical_and(i >= 1, i <= nv)))
    def _slot_free():
        wait_scatter(slot, rows_ref[jnp.maximum(i - 1, 0)])

    @pl.when(valid)
    def _compute():
        hb = hbuf[slot].astype(BF16)
        r0 = f * half
        n_prev = rows_ref[i]
        for j in range(half):
            gather_row(src_ref[(i + 1) * tm + r0 + j], other, r0 + j).start()

            @pl.when(r0 + j < n_prev)
            def _():
                scatter_row(dst_ref[i * tm + r0 + j], other, r0 + j).start()
        act = (_silu(_dot(hb, wg_ref[...])) * _dot(hb, wu_ref[...])).astype(BF16)
        part = _dot(act, wd_ref[...])
        ybuf[slot] = jnp.where(f == 0, part, ybuf[slot] + part)

    @pl.when(jnp.logical_and(i == nv, f == 0))
    def _flush():
        n_last = rows_ref[i]

        def issue(j, carry):
            scatter_row(dst_ref[i * tm + j], other, j).start()
            return carry

        lax.fori_loop(0, n_last, issue, 0)
        wait_scatter(other, n_last)


def _moe_experts(tile_expert, src, dst, rows, n_valid, h, wg, wu, wd, n_tiles, y_rows):
    d = h.shape[1]
    tm = MOE_TILE
    nf = MOE_FCHUNKS
    fc = wg.shape[2] // nf

    def f_idx(i, f, nv):
        return jnp.where(i < nv[0], f, nf - 1)

    grid_spec = pltpu.PrefetchScalarGridSpec(
        num_scalar_prefetch=5,
        grid=(n_tiles + 1, nf),
        in_specs=[pl.BlockSpec(memory_space=pl.ANY),
                  pl.BlockSpec((None, d, fc), lambda i, f, te, src, dst, rows, nv: (te[i], 0, f_idx(i, f, nv))),
                  pl.BlockSpec((None, d, fc), lambda i, f, te, src, dst, rows, nv: (te[i], 0, f_idx(i, f, nv))),
                  pl.BlockSpec((None, fc, d), lambda i, f, te, src, dst, rows, nv: (te[i], f_idx(i, f, nv), 0))],
        out_specs=pl.BlockSpec(memory_space=pl.ANY),
        scratch_shapes=[pltpu.VMEM((2, tm, d), F32), pltpu.VMEM((2, tm, d), F32),
                        pltpu.SemaphoreType.DMA((2,)), pltpu.SemaphoreType.DMA((2,))],
    )
    return pl.pallas_call(
        _moe_kernel,
        grid_spec=grid_spec,
        out_shape=jax.ShapeDtypeStruct((y_rows, d), F32),
        compiler_params=_params(("arbitrary", "arbitrary")),
        name="moe_experts",
    )(tile_expert, src, dst, rows, n_valid, h, wg, wu, wd)


def _combine_kernel(x_ref, gate_ref, mf_ref, fg_ref, y1_ref, y2_ref, o_ref):
    w = mf_ref[...]
    y = w[:, 0:1] * y1_ref[...] + w[:, 1:2] * y2_ref[...]
    o_ref[...] = _rms(x_ref[...] + gate_ref[...] * y, fg_ref[...])


def _moe_combine(x, gate, mf, fg, y, tiles_per_batch, n_batch):
    t, d = x.shape
    tm = ROW_TILE
    row = lambda i: (i, 0)
    return pl.pallas_call(
        _combine_kernel,
        grid=(t // tm,),
        in_specs=[pl.BlockSpec((tm, d), row), _mod_spec(d, tiles_per_batch, n_batch), pl.BlockSpec((tm, LANES), row),
                  _full((1, d)), pl.BlockSpec((tm, d), row), pl.BlockSpec((tm, d), lambda i: (t // tm + i, 0))],
        out_specs=pl.BlockSpec((tm, d), row),
        out_shape=jax.ShapeDtypeStruct((t, d), F32),
        compiler_params=_params(("parallel",)),
        name="moe_combine",
    )(x, gate, mf, fg, y, y)


def kernel(x, c, ctx, c_ctx, ada_w, ada_b, norm_g, final_g, na_w_in, na_rpb, sg_w, sg_b, sg_norm_g, even_w_out,
           ffn_w_gate, ffn_w_up, ffn_w_down, mla_w_in, mla_q_norm_g, mla_kv_norm_g, mla_w_uq, mla_w_ukv, mla_w_out,
           moe_w_router, moe_b_router, moe_w_gate, moe_w_up, moe_w_down):
    n_batch, seq, d = x.shape
    n_ctx = ctx.shape[1]
    n_lat = n_batch * seq
    assert ada_w.shape[0] == 2 and seq % ROW_TILE == 0 and n_batch * n_ctx == ROW_TILE
    assert n_batch + 1 <= SUBLANES and seq % GRID_W == 0
    tpb = seq // ROW_TILE

    cond8 = jnp.concatenate([c, c_ctx[None, :], jnp.zeros((SUBLANES - n_batch - 1, d), F32)], axis=0)
    mods = _ada_mod(cond8, ada_w, ada_b)[:, :n_batch + 1]
    mod = lambda layer, k: mods[layer, :, k * d:(k + 1) * d].reshape(n_batch + 1, 1, d)
    x_lat, x_ctx = x.reshape(n_lat, d), ctx.reshape(n_batch * n_ctx, d)

    w_in = na_w_in[0].astype(BF16)
    qt, k, vt, ug = _even_in(x_lat, x_ctx, norm_g[0, 0][None], mod(0, 0), mod(0, 1), w_in[:, :3 * A_WIDTH],
                             w_in[:, 3 * A_WIDTH:], tpb, n_batch)
    bias = _na_bias_table(na_rpb[0], seq // GRID_W)
    attn = _na_attention(qt, k, vt, bias, n_batch, seq, n_ctx)
    bmat = jnp.repeat(sg_b[0].T, SG_GROUP_DIM, axis=1)
    avg = jnp.asarray(np.kron(np.eye(SG_GROUPS), np.full((SG_GROUP_DIM, SG_GROUP_DIM), 1.0 / SG_GROUP_DIM)), BF16)
    gated = _spatial_gating(ug, sg_w[0].astype(BF16), bmat, sg_norm_g[0][None], avg)
    w_out = even_w_out[0].astype(BF16)
    xs = _ffn(x_lat, x_ctx, mod(0, 2), [attn, gated], [w_out[:A_WIDTH], w_out[A_WIDTH:]], norm_g[0, 1][None], mod(0, 3),
              mod(0, 4), mod(0, 5), ffn_w_gate[0].astype(BF16), ffn_w_up[0].astype(BF16), ffn_w_down[0].astype(BF16),
              tpb, n_batch)

    win, wuq_t, wuk, wuv_t = _mla_weights(mla_w_in[0], mla_w_uq[0], mla_w_ukv[0])
    cos, sin = _rope_tables(n_batch, seq, n_ctx)
    qt, k, vt = _mla_pre(xs, norm_g[1, 0][None], mod(1, 0), mod(1, 1), win, mla_q_norm_g[0][None],
                         mla_kv_norm_g[0][None], wuq_t, wuk, wuv_t, cos, sin, tpb, n_batch)
    vtc = jnp.transpose(vt[n_batch * tpb].reshape(-1, n_batch, n_ctx), (1, 0, 2))
    o = _mla_attention(qt, k, vt, vtc, n_batch, seq, n_ctx)
    wr = jnp.concatenate([moe_w_router[0], jnp.zeros((d, LANES - N_EXPERTS), F32)], axis=1)
    wr_hi = wr.astype(BF16)
    wr = jnp.concatenate([wr_hi, (wr - wr_hi.astype(F32)).astype(BF16)], axis=1)
    br = jnp.concatenate([moe_b_router[0], jnp.full((LANES - N_EXPERTS,), NEG, F32)])[None]
    tri = jnp.asarray(np.tril(np.ones((ROW_TILE, ROW_TILE), np.float32), -1), BF16)
    x1, h, mi, mf, cnt = _router(xs, mod(1, 2), o, mla_w_out[0].astype(BF16), norm_g[1, 1][None], mod(1, 3),
                                 mod(1, 4), wr, br, tri, n_lat, tpb, n_batch)

    counts = cnt[0, :N_EXPERTS].astype(jnp.int32)
    tiles_e = (counts + MOE_TILE - 1) // MOE_TILE
    tile_end = jnp.cumsum(tiles_e)
    start = (tile_end - tiles_e) * MOE_TILE
    slot12 = jnp.concatenate([start[mi[:, 0]] + mi[:, 2], start[mi[:, 1]] + mi[:, 3]])
    n_tiles = 2 * n_lat // MOE_TILE + N_EXPERTS
    n_slots = n_tiles * MOE_TILE
    dst = _slot_owner(slot12.astype(jnp.int32), n_slots)
    spare_tile = jnp.zeros((MOE_TILE,), jnp.int32)
    src = jnp.concatenate([dst % n_lat, spare_tile])
    dst = jnp.concatenate([spare_tile, dst])
    n_valid = tile_end[-1:]
    last_tile = jnp.maximum(n_valid[0] - 1, 0)
    all_tiles = jnp.arange(n_tiles + 1, dtype=jnp.int32)
    tile_ids = jnp.minimum(all_tiles, last_tile)
    tile_expert = jnp.minimum(jnp.sum((tile_ids[:, None] >= tile_end[None, :]).astype(jnp.int32), axis=1),
                              N_EXPERTS - 1)
    filled = counts[tile_expert] - (all_tiles - (tile_end - tiles_e)[tile_expert]) * MOE_TILE
    rows = jnp.where(all_tiles < n_valid[0], jnp.clip(filled, 0, MOE_TILE), 0)
    rows = jnp.concatenate([jnp.zeros((1,), jnp.int32), rows]).astype(jnp.int32)
    y = _moe_experts(tile_expert, src, dst, rows, n_valid.astype(jnp.int32), h, moe_w_gate[0].astype(BF16),
                     moe_w_up[0].astype(BF16), moe_w_down[0].astype(BF16), n_tiles, 2 * n_lat)
    out = _moe_combine(x1, mod(1, 5), mf, final_g[None], y, tpb, n_batch)
    return out.reshape(n_batch, seq, d)
```

```python
import functools

import numpy as np
import jax
import jax.numpy as jnp
from jax import lax
from jax.experimental import pallas as pl
from jax.experimental.pallas import tpu as pltpu

F32 = jnp.float32
BF16 = jnp.bfloat16
EPS = 1e-6
NEG = -1e30

LANES = 128
SUBLANES = 8
VMEM_LIMIT_BYTES = 56 * 1024 * 1024

GRID_W = 64
NA_HEADS = 8
NA_HEAD_DIM = 64
NA_WIN_R = 8
NA_WIN_C = 16
SG_GROUPS = 8
SG_GROUP_DIM = 64
SG_CHUNK = 128
A_WIDTH = NA_HEADS * NA_HEAD_DIM
B_WIDTH = SG_GROUPS * SG_GROUP_DIM
MLA_HEADS = 16
MLA_NOPE = 64
MLA_ROPE = 32
MLA_V = 64
MLA_Q_RANK = 384
MLA_KV_RANK = 256
MLA_SCALE = (MLA_NOPE + MLA_ROPE) ** -0.5
ROPE_BASE = 10000.0
N_EXPERTS = 8

ROW_TILE = 512
NA_QROWS = 4
NA_QB = NA_QROWS * GRID_W
NA_KB = 3 * NA_QB
NA_STEP_HEADS = 4
MLA_VROWS = 80
FLASH_QTILES = 4
LOG2E = 1.4426950408889634
MOE_TILE = 512
MOE_FCHUNKS = 2


def _params(sem):
    return pltpu.CompilerParams(dimension_semantics=sem, vmem_limit_bytes=VMEM_LIMIT_BYTES)


def _dot(a, b):
    return jnp.dot(a, b, preferred_element_type=F32)


def _silu(x):
    return x / (1.0 + jnp.exp(-x))


def _gelu_tanh(x):
    return 0.5 * x * (1.0 + jnp.tanh(0.7978845608028654 * (x + 0.044715 * (x * x * x))))


def _rms(x, g):
    return x * lax.rsqrt(jnp.mean(x * x, axis=-1, keepdims=True) + EPS) * g


def _norm_mod(x, g, sh, sc):
    return _rms(x, g) * (1.0 + sc) + sh


def _full(shape):
    n = len(shape)
    return pl.BlockSpec(shape, lambda *_: (0,) * n)


def _ada_kernel(cond_ref, w_ref, b_ref, o_ref):
    c = cond_ref[...]
    o_ref[...] = jnp.dot(_silu(c), w_ref[...], preferred_element_type=F32,
                         precision=lax.Precision.HIGHEST) + b_ref[...]


def _ada_mod(cond8, ada_w, ada_b):
    depth, d, n = ada_w.shape
    tn = n // 4
    return pl.pallas_call(
        _ada_kernel,
        grid=(depth, n // tn),
        in_specs=[_full((SUBLANES, d)),
                  pl.BlockSpec((None, d, tn), lambda l, j: (l, 0, j)),
                  pl.BlockSpec((None, 1, tn), lambda l, j: (l, 0, j))],
        out_specs=pl.BlockSpec((None, SUBLANES, tn), lambda l, j: (l, 0, j)),
        out_shape=jax.ShapeDtypeStruct((depth, SUBLANES, n), F32),
        compiler_params=_params(("parallel", "parallel")),
        name="ada_mod",
    )(cond8, ada_w, ada_b.reshape(depth, 1, n))


def _group_map(tiles_per_batch, n_batch):
    return lambda t: (jnp.minimum(t // tiles_per_batch, n_batch), 0, 0)


def _mod_spec(d, tiles_per_batch, n_batch):
    return pl.BlockSpec((None, 1, d), _group_map(tiles_per_batch, n_batch))


def _stream_specs(x_lat, x_ctx):
    tm, d = ROW_TILE, x_lat.shape[1]
    assert x_ctx.shape[0] == tm
    last = x_lat.shape[0] // tm - 1
    return [pl.BlockSpec((tm, d), lambda i: (jnp.minimum(i, last), 0)), pl.BlockSpec((tm, d), lambda i: (0, 0))]


def _stream_tile(xl_ref, xc_ref, n_lat_tiles):
    return jnp.where(pl.program_id(0) < n_lat_tiles, xl_ref[...], xc_ref[...])


def _even_in_kernel(n_lat_tiles, xl_ref, xc_ref, g_ref, sh_ref, sc_ref, wqkv_ref, wug_ref, qt_ref, k_ref, vt_ref,
                    ug_ref):
    x = _stream_tile(xl_ref, xc_ref, n_lat_tiles)
    h = _norm_mod(x, g_ref[...], sh_ref[...], sc_ref[...]).astype(BF16)
    qkv = _dot(h, wqkv_ref[...])
    qt_ref[...] = (qkv[:, :A_WIDTH] * (NA_HEAD_DIM ** -0.5 * LOG2E)).T.astype(qt_ref.dtype)
    k_ref[...] = qkv[:, A_WIDTH:2 * A_WIDTH].astype(k_ref.dtype)
    vt_ref[...] = qkv[:, 2 * A_WIDTH:].T.astype(vt_ref.dtype)
    ug_ref[...] = _dot(h, wug_ref[...])


def _even_in(x_lat, x_ctx, g, sh, sc, wqkv, wug, tiles_per_batch, n_batch):
    d = x_lat.shape[1]
    t = x_lat.shape[0] + x_ctx.shape[0]
    tm = ROW_TILE
    mod = _mod_spec(d, tiles_per_batch, n_batch)
    return pl.pallas_call(
        functools.partial(_even_in_kernel, x_lat.shape[0] // tm),
        grid=(t // tm,),
        in_specs=_stream_specs(x_lat, x_ctx) + [_full((1, d)), mod, mod, _full(wqkv.shape), _full(wug.shape)],
        out_specs=[pl.BlockSpec((A_WIDTH, tm), lambda i: (0, i)), pl.BlockSpec((tm, A_WIDTH), lambda i: (i, 0)),
                   pl.BlockSpec((A_WIDTH, tm), lambda i: (0, i)), pl.BlockSpec((tm, wug.shape[1]), lambda i: (i, 0))],
        out_shape=[jax.ShapeDtypeStruct((A_WIDTH, t), BF16), jax.ShapeDtypeStruct((t, A_WIDTH), BF16),
                   jax.ShapeDtypeStruct((A_WIDTH, t), BF16), jax.ShapeDtypeStruct((t, wug.shape[1]), F32)],
        compiler_params=_params(("parallel",)),
        name="even_in",
    )(x_lat, x_ctx, g, sh, sc, wqkv, wug)


def _na_bias_table(rpb, rows):
    nblk = rows // NA_QROWS
    n_heads = rpb.shape[0]
    win_rows = NA_KB // GRID_W
    qc = np.arange(GRID_W)
    col_start = np.clip(qc - NA_WIN_C // 2, 0, GRID_W - NA_WIN_C)
    col_ok = (qc[None, :] >= col_start[:, None]) & (qc[None, :] < col_start[:, None] + NA_WIN_C)
    col_j = qc[None, :] - qc[:, None] + NA_WIN_C - 1
    sel_c = ((col_j[None] == np.arange(2 * NA_WIN_C - 1)[:, None, None]) & col_ok[None]).astype(np.float32)
    toeplitz = jnp.einsum("hij,jqk->hikq", rpb, sel_c, precision=lax.Precision.HIGHEST) * LOG2E
    toeplitz = jnp.where(col_ok.T[None, None], toeplitz, NEG)
    n_rel = 2 * NA_WIN_R - 1
    toeplitz = jnp.concatenate([toeplitz, jnp.full((n_heads, 1, GRID_W, GRID_W), NEG, F32)], axis=1)
    rel = np.full((3, NA_QROWS, win_rows), n_rel, np.int32)
    for kind, j in enumerate((0, 1, nblk - 1)):
        first_row = NA_QROWS * int(np.clip(j - 1, 0, nblk - 3))
        for a in range(NA_QROWS):
            r = NA_QROWS * j + a
            row_start = int(np.clip(r - NA_WIN_R // 2, 0, rows - NA_WIN_R))
            for b in range(win_rows):
                kr = first_row + b
                if row_start <= kr < row_start + NA_WIN_R:
                    rel[kind, a, b] = kr - r + NA_WIN_R - 1
    grid_spec = pltpu.PrefetchScalarGridSpec(
        num_scalar_prefetch=1,
        grid=(3, n_heads),
        in_specs=[pl.BlockSpec((None, n_rel + 1, GRID_W, GRID_W), lambda c, h, rel: (h, 0, 0, 0))],
        out_specs=pl.BlockSpec((None, None, NA_KB, NA_QB), lambda c, h, rel: (c, h, 0, 0)),
    )
    return pl.pallas_call(
        _na_bias_kernel,
        grid_spec=grid_spec,
        out_shape=jax.ShapeDtypeStruct((3, n_heads, NA_KB, NA_QB), F32),
        compiler_params=_params(("parallel", "parallel")),
        name="na_bias",
    )(jnp.asarray(rel.reshape(-1)), toeplitz)


def _na_bias_kernel(rel_ref, t_ref, o_ref):
    kind = pl.program_id(0)
    win_rows = NA_KB // GRID_W
    for b in range(win_rows):
        for ap in range(NA_QROWS // 2):
            rel = [rel_ref[(kind * NA_QROWS + 2 * ap + s) * win_rows + b] for s in range(2)]
            pair = jnp.concatenate([t_ref[rel[0]], t_ref[rel[1]]], axis=1)
            o_ref[GRID_W * b:GRID_W * (b + 1), 2 * GRID_W * ap:2 * GRID_W * (ap + 1)] = pair


def _col_reduce(x_t, combine, reduce):
    slab = 8 * SUBLANES
    parts = [x_t[r:r + slab] for r in range(0, x_t.shape[0], slab)]
    while len(parts) > 1:
        parts = [combine(parts[j], parts[j + 1]) for j in range(0, len(parts) - 1, 2)] + parts[len(parts) & ~1:]
    return reduce(parts[0], axis=0, keepdims=True)


def _col_max(s_t):
    return _col_reduce(s_t, jnp.maximum, jnp.max)


def _col_sum(p_t):
    return _col_reduce(p_t, jnp.add, jnp.sum)


def _na_kernel(qt_ref, k0_ref, k1_ref, k2_ref, vt0_ref, vt1_ref, vt2_ref, kc_ref, vtc_ref, bias_ref, o_ref):
    jj = pl.program_id(2)
    row = lax.broadcasted_iota(jnp.int32, (NA_STEP_HEADS * NA_HEAD_DIM, 1), 0)
    qt = qt_ref[...]
    zero = jnp.zeros_like(qt)
    heads = range(NA_STEP_HEADS)

    def head_q(hh):
        mine = jnp.logical_and(row >= NA_HEAD_DIM * hh, row < NA_HEAD_DIM * (hh + 1))
        return jnp.where(mine, qt, zero)

    def attend(s_t, vt_refs, hh):
        m = functools.reduce(jnp.maximum, [_col_max(x) for x in s_t])
        p_t = [jnp.exp2(x - m) for x in s_t]
        l = functools.reduce(jnp.add, [_col_sum(x) for x in p_t])
        vrows = slice(NA_HEAD_DIM * hh, NA_HEAD_DIM * (hh + 1))
        o_t = functools.reduce(jnp.add, [_dot(vt[vrows, :], x.astype(BF16)) for vt, x in zip(vt_refs, p_t)])
        return o_t / l

    def finish(outs):
        o_ref[...] = jnp.concatenate(outs, axis=0).T.astype(o_ref.dtype)

    @pl.when(jj == 0)
    def _ctx_queries():
        finish([attend([_dot(kc_ref[...], head_q(hh))], [vtc_ref], hh) for hh in heads])

    @pl.when(jj > 0)
    def _latent_queries():
        k_refs = (k0_ref, k1_ref, k2_ref)
        s_all = []
        for hh in heads:
            qh = head_q(hh)
            s_t = [_dot(k_refs[i][...], qh) + bias_ref[hh, NA_QB * i:NA_QB * (i + 1), :] for i in range(3)]
            s_all.append(s_t + [_dot(kc_ref[...], qh)])
        finish([attend(s_all[hh], (vt0_ref, vt1_ref, vt2_ref, vtc_ref), hh) for hh in heads])


def _na_attention(qt, k, vt, bias, n_batch, seq, n_ctx):
    t = k.shape[0]
    assert n_ctx == NA_QB and seq % NA_QB == 0
    nblk = seq // NA_QB
    assert nblk >= 3
    hp = NA_HEADS // NA_STEP_HEADS
    width = NA_STEP_HEADS * NA_HEAD_DIM
    ctx_blk = n_batch * nblk

    def q_blk(b, j):
        return jnp.where(j == 0, ctx_blk + b, b * nblk + j - 1)

    def win_blk(b, j, i):
        return b * nblk + jnp.clip(j - 2, 0, nblk - 3) + i

    def bias_map(b, h, j):
        return (jnp.where(j <= 1, 0, jnp.where(j == nblk, 2, 1)), h, 0, 0)

    tok, feat = (NA_QB, width), (width, NA_QB)
    in_specs = ([pl.BlockSpec(feat, lambda b, h, j: (h, q_blk(b, j)))]
                + [pl.BlockSpec(tok, functools.partial(lambda i, b, h, j: (win_blk(b, j, i), h), i)) for i in range(3)]
                + [pl.BlockSpec(feat, functools.partial(lambda i, b, h, j: (h, win_blk(b, j, i)), i)) for i in range(3)]
                + [pl.BlockSpec(tok, lambda b, h, j: (ctx_blk + b, h)),
                   pl.BlockSpec(feat, lambda b, h, j: (h, ctx_blk + b)),
                   pl.BlockSpec((None, NA_STEP_HEADS, NA_KB, NA_QB), bias_map)])
    return pl.pallas_call(
        _na_kernel,
        grid=(n_batch, hp, nblk + 1),
        in_specs=in_specs,
        out_specs=pl.BlockSpec(tok, lambda b, h, j: (q_blk(b, j), h)),
        out_shape=jax.ShapeDtypeStruct((t, A_WIDTH), BF16),
        compiler_params=_params(("parallel", "parallel", "arbitrary")),
        name="na_attention",
    )(qt, k, k, k, vt, vt, vt, k, vt, bias)


def _seg_mean(x, avg):
    hi = x.astype(BF16)
    lo = (x - hi.astype(F32)).astype(BF16)
    return _dot(hi, avg) + _dot(lo, avg)


def _sg_kernel(ug_ref, ws_ref, bm_ref, ng_ref, avg_ref, o_ref):
    lane = lax.broadcasted_iota(jnp.int32, (1, LANES), 1)
    first = lane < SG_GROUP_DIM
    avg = avg_ref[...]
    g = _gelu_tanh(ug_ref[:, B_WIDTH:2 * B_WIDTH])
    d = g - _seg_mean(g, avg)
    var = _seg_mean(d * d, avg)
    gn = (d * lax.rsqrt(var + EPS) * ng_ref[...]).astype(BF16)
    for c in range(ROW_TILE // SG_CHUNK):
        rows = slice(SG_CHUNK * c, SG_CHUNK * (c + 1))
        parts = []
        for j in range(SG_GROUPS // 2):
            gj = gn[rows, LANES * j:LANES * (j + 1)]
            parts.append(jnp.where(first, _dot(ws_ref[2 * j], gj), _dot(ws_ref[2 * j + 1], gj)))
        mixed = jnp.concatenate(parts, axis=1) + bm_ref[...]
        o_ref[rows, :] = (_gelu_tanh(ug_ref[rows, 0:B_WIDTH]) * mixed).astype(o_ref.dtype)


def _spatial_gating(ug, ws, bmat, ng, avg):
    t = ug.shape[0]
    tm = ROW_TILE
    return pl.pallas_call(
        _sg_kernel,
        grid=(t // tm,),
        in_specs=[pl.BlockSpec((tm, 2 * B_WIDTH), lambda i: (i, 0)), _full(ws.shape), _full(bmat.shape),
                  _full(ng.shape), _full(avg.shape)],
        out_specs=pl.BlockSpec((tm, B_WIDTH), lambda i: (i, 0)),
        out_shape=jax.ShapeDtypeStruct((t, B_WIDTH), BF16),
        compiler_params=_params(("parallel",)),
        name="spatial_gating",
    )(ug, ws, bmat, ng, avg)


def _mixer_residual(x, gate_ref, a_refs, w_refs):
    y = functools.reduce(jnp.add, [_dot(a[...], w[...]) for a, w in zip(a_refs, w_refs)])
    return x + gate_ref[...] * y


def _ffn_kernel(n_chunks, n_in, n_lat_tiles, xl_ref, xc_ref, gate_m_ref, g_ref, sh_ref, sc_ref, gate_ref, *refs):
    a_refs, w_refs = refs[:n_in], refs[n_in:2 * n_in]
    wg_ref, wu_ref, wd_ref, o_ref = refs[2 * n_in:]
    x = _mixer_residual(_stream_tile(xl_ref, xc_ref, n_lat_tiles), gate_m_ref, a_refs, w_refs)
    h = _norm_mod(x, g_ref[...], sh_ref[...], sc_ref[...]).astype(BF16)
    fc = wg_ref.shape[1] // n_chunks
    acc = None
    for c in range(n_chunks):
        cs = slice(fc * c, fc * (c + 1))
        act = (_silu(_dot(h, wg_ref[:, cs])) * _dot(h, wu_ref[:, cs])).astype(BF16)
        part = _dot(act, wd_ref[cs, :])
        acc = part if acc is None else acc + part
    o_ref[...] = x + gate_ref[...] * acc


def _ffn(x_lat, x_ctx, gate_m, acts, weights, g, sh, sc, gate, wg, wu, wd, tiles_per_batch, n_batch):
    d = x_lat.shape[1]
    t = x_lat.shape[0] + x_ctx.shape[0]
    tm = ROW_TILE
    f = wg.shape[1]
    n_chunks = 2 if f % (2 * LANES) == 0 else 1
    mod = _mod_spec(d, tiles_per_batch, n_batch)
    row = lambda i: (i, 0)
    resident = lambda w: pl.BlockSpec(w.shape, lambda i: (0, 0), pipeline_mode=pl.Buffered(1))
    return pl.pallas_call(
        functools.partial(_ffn_kernel, n_chunks, len(acts), x_lat.shape[0] // tm),
        grid=(t // tm,),
        in_specs=(_stream_specs(x_lat, x_ctx) + [mod, _full((1, d)), mod, mod, mod]
                  + [pl.BlockSpec((tm, a.shape[1]), row) for a in acts] + [resident(w) for w in weights]
                  + [resident(wg), resident(wu), resident(wd)]),
        out_specs=pl.BlockSpec((tm, d), row),
        out_shape=jax.ShapeDtypeStruct((t, d), F32),
        compiler_params=_params(("parallel",)),
        name="ffn",
    )(x_lat, x_ctx, gate_m, g, sh, sc, gate, *acts, *weights, wg, wu, wd)


def _rope_tables(n_batch, seq, n_ctx):
    t = np.arange(seq)
    pos = np.stack([t // GRID_W, t % GRID_W], axis=-1).astype(np.float32)
    n_freq = MLA_ROPE // 4
    inv = jnp.power(ROPE_BASE, -jnp.arange(n_freq, dtype=F32) / n_freq)
    ang = jnp.asarray(pos)[:, :, None] * inv
    cos, sin = jnp.cos(ang), jnp.sin(ang)
    cos_r = jnp.concatenate([cos, cos], axis=-1).reshape(seq, MLA_ROPE)
    sin_r = jnp.concatenate([-sin, sin], axis=-1).reshape(seq, MLA_ROPE)
    ones = jnp.ones((seq, MLA_NOPE), F32)
    pad = LANES - MLA_NOPE - MLA_ROPE
    cos_l = jnp.concatenate([ones, cos_r, jnp.ones((seq, pad), F32)], axis=-1)
    sin_l = jnp.concatenate([0 * ones, sin_r, jnp.zeros((seq, pad), F32)], axis=-1)
    cos_all = jnp.concatenate([jnp.tile(cos_l, (n_batch, 1)), jnp.ones((n_batch * n_ctx, LANES), F32)])
    sin_all = jnp.concatenate([jnp.tile(sin_l, (n_batch, 1)), jnp.zeros((n_batch * n_ctx, LANES), F32)])
    return cos_all, sin_all


def _swap_rope_cols(w_rope):
    q = MLA_ROPE // 4
    return jnp.concatenate([w_rope[..., q:2 * q], w_rope[..., 0:q], w_rope[..., 3 * q:4 * q],
                            w_rope[..., 2 * q:3 * q]], axis=-1)


def _mla_weights(w_in, w_uq, w_ukv):
    d = w_in.shape[0]
    pad = LANES - MLA_NOPE - MLA_ROPE
    w_kr = w_in[:, MLA_Q_RANK + MLA_KV_RANK:]
    lanes_kr = lambda w: jnp.concatenate([jnp.zeros((d, MLA_NOPE), F32), w, jnp.zeros((d, pad), F32)], axis=1)
    win = jnp.concatenate([w_in[:, :MLA_Q_RANK + MLA_KV_RANK], lanes_kr(w_kr), lanes_kr(_swap_rope_cols(w_kr))],
                          axis=1)
    uq = w_uq.reshape(MLA_Q_RANK, MLA_HEADS, MLA_NOPE + MLA_ROPE)
    zq = jnp.zeros((MLA_Q_RANK, MLA_HEADS, pad), F32)
    uq_a = jnp.concatenate([uq, zq], axis=-1).reshape(MLA_Q_RANK, MLA_HEADS * LANES)
    uq_b = _swap_rope_cols(uq[..., MLA_NOPE:]).reshape(MLA_Q_RANK, MLA_HEADS * MLA_ROPE)
    wuq_t = jnp.concatenate([uq_a, uq_b], axis=1).T
    ukv = w_ukv.reshape(MLA_KV_RANK, MLA_HEADS, MLA_NOPE + MLA_V)
    zk = jnp.zeros((MLA_KV_RANK, MLA_HEADS, LANES - MLA_NOPE), F32)
    uk = jnp.concatenate([ukv[..., :MLA_NOPE], zk], axis=-1).reshape(MLA_KV_RANK, MLA_HEADS * LANES)
    uv_t = jnp.transpose(ukv[..., MLA_NOPE:], (1, 2, 0))
    uv_t = jnp.concatenate([uv_t, jnp.zeros((MLA_HEADS, MLA_VROWS - MLA_V, MLA_KV_RANK), F32)], axis=1)
    return (win.astype(BF16), wuq_t.astype(BF16), uk.astype(BF16),
            uv_t.reshape(MLA_HEADS * MLA_VROWS, MLA_KV_RANK).astype(BF16))


def _mla_pre_kernel(x_ref, g_ref, sh_ref, sc_ref, win_ref, qg_ref, kvg_ref, wuqt_ref, wuk_ref, wuvt_ref,
                    cos_ref, sin_ref, cost_ref, sint_ref, qt_ref, k_ref, vt_ref):
    h = _norm_mod(x_ref[...], g_ref[...], sh_ref[...], sc_ref[...]).astype(BF16)
    p = _dot(h, win_ref[...])
    cq = _rms(p[:, :MLA_Q_RANK], qg_ref[...])
    ckv = _rms(p[:, MLA_Q_RANK:MLA_Q_RANK + MLA_KV_RANK], kvg_ref[...])
    cq_t = cq.T.astype(BF16)
    ckv_t = ckv.T.astype(BF16)
    ckv_b = ckv.astype(BF16)
    o_kr = MLA_Q_RANK + MLA_KV_RANK
    kr_rot = p[:, o_kr:o_kr + LANES] * cos_ref[...] + p[:, o_kr + LANES:o_kr + 2 * LANES] * sin_ref[...]
    kr2 = jnp.concatenate([kr_rot, kr_rot], axis=1)
    rope = slice(MLA_NOPE, MLA_NOPE + MLA_ROPE)
    cos_r, sin_r = cost_ref[rope, :], sint_ref[rope, :]
    half = MLA_HEADS * LANES
    q_swapped = _dot(wuqt_ref[half:half + MLA_HEADS * MLA_ROPE, :], cq_t)
    for c in range(MLA_HEADS // 2):
        rs = slice(2 * LANES * c, 2 * LANES * (c + 1))
        q_pair = _dot(wuqt_ref[rs, :], cq_t)
        rows = []
        for hh in range(2):
            q_h = q_pair[LANES * hh:LANES * (hh + 1)]
            swapped = q_swapped[MLA_ROPE * (2 * c + hh):MLA_ROPE * (2 * c + hh + 1)]
            rows += [q_h[:MLA_NOPE], q_h[rope] * cos_r + swapped * sin_r, q_h[MLA_NOPE + MLA_ROPE:]]
        q_t = jnp.concatenate(rows, axis=0)
        qt_ref[rs, :] = (q_t * (MLA_SCALE * LOG2E)).astype(qt_ref.dtype)
        k_ref[:, rs] = (_dot(ckv_b, wuk_ref[:, rs]) + kr2).astype(k_ref.dtype)
    group = 4 * MLA_VROWS
    row = lax.broadcasted_iota(jnp.int32, (group, 1), 0)
    ones_rows = functools.reduce(jnp.add, [jnp.where(row == MLA_VROWS * j + MLA_V, 1.0, 0.0) for j in range(4)])
    for c in range(MLA_HEADS // 4):
        rv = slice(group * c, group * (c + 1))
        vt_ref[rv, :] = (_dot(wuvt_ref[rv, :], ckv_t) + ones_rows).astype(vt_ref.dtype)


def _mla_pre(x, g, sh, sc, win, qg, kvg, wuq_t, wuk, wuv_t, cos, sin, tiles_per_batch, n_batch):
    t, d = x.shape
    tm = ROW_TILE
    n = MLA_HEADS * LANES
    nv = MLA_HEADS * MLA_VROWS
    mod = _mod_spec(d, tiles_per_batch, n_batch)
    row = lambda i: (i, 0)
    col = lambda i: (0, i)
    return pl.pallas_call(
        _mla_pre_kernel,
        grid=(t // tm,),
        in_specs=[pl.BlockSpec((tm, d), row), _full((1, d)), mod, mod, _full(win.shape), _full(qg.shape),
                  _full(kvg.shape), _full(wuq_t.shape), _full(wuk.shape), _full(wuv_t.shape),
                  pl.BlockSpec((tm, LANES), row), pl.BlockSpec((tm, LANES), row),
                  pl.BlockSpec((LANES, tm), col), pl.BlockSpec((LANES, tm), col)],
        out_specs=[pl.BlockSpec((None, n, tm), lambda i: (i, 0, 0)), pl.BlockSpec((tm, n), row),
                   pl.BlockSpec((None, nv, tm), lambda i: (i, 0, 0))],
        out_shape=[jax.ShapeDtypeStruct((t // tm, n, tm), BF16), jax.ShapeDtypeStruct((t, n), BF16),
                   jax.ShapeDtypeStruct((t // tm, nv, tm), BF16)],
        compiler_params=_params(("parallel",)),
        name="mla_pre",
    )(x, g, sh, sc, win, qg, kvg, wuq_t, wuk, wuv_t, cos, sin, cos.T, sin.T)


def _flash_kernel(qts_ref, k_ref, vt_ref, kc_ref, vtc_ref, os_ref, *s_refs):
    for t in range(qts_ref.shape[0]):
        _flash_tile(qts_ref.at[t], k_ref, vt_ref, kc_ref, vtc_ref, os_ref.at[pl.ds(t * qts_ref.shape[2], qts_ref.shape[2])],
                    *s_refs[4 * t:4 * t + 4])


def _flash_tile(qt_ref, k_ref, vt_ref, kc_ref, vtc_ref, o_ref, sa0_ref, sa1_ref, sb0_ref, sb1_ref):
    tq = qt_ref.shape[1]
    n_blocks, _, tk = vt_ref.shape
    n_ctx = kc_ref.shape[0]
    heads = range(2)
    s_a, s_b = (sa0_ref, sa1_ref), (sb0_ref, sb1_ref)
    lanes = lambda hh: slice(LANES * hh, LANES * (hh + 1))
    vrows = lambda hh: slice(MLA_VROWS * hh, MLA_VROWS * (hh + 1))

    def scores(kb_of, bufs, rows):
        cmax = []
        for hh in heads:
            s_t = _dot(kb_of(hh), qt_ref[lanes(hh), :])
            bufs[hh][0:rows, :] = s_t
            cmax.append(_col_max(s_t))
        return tuple(cmax)

    def absorb(bufs, rows, cmax, vt_of, state):
        new = []
        for hh in heads:
            m, acc = state[hh]
            m_new = jnp.maximum(m, cmax[hh])
            p_t = jnp.exp2(bufs[hh][0:rows, :] - m_new).astype(BF16)
            new.append((m_new, acc * jnp.exp2(m - m_new) + _dot(vt_of(hh), p_t)))
        return tuple(new)

    def lat_k(blk):
        off = blk * tk if isinstance(blk, int) else pl.multiple_of(blk * tk, tk)
        return lambda hh: k_ref[pl.ds(off, tk), lanes(hh)]

    lat_v = lambda blk: (lambda hh: vt_ref[blk, vrows(hh), :])
    ctx_k = lambda hh: kc_ref[:, lanes(hh)]
    ctx_v = lambda hh: vtc_ref[vrows(hh), :]

    def pair(blk, cmax_a, state, last):
        cmax_b = scores(lat_k(blk + 1), s_b, tk)
        state = absorb(s_a, tk, cmax_a, lat_v(blk), state)
        cmax_a = scores(ctx_k, s_a, n_ctx) if last else scores(lat_k(blk + 2), s_a, tk)
        return cmax_a, absorb(s_b, tk, cmax_b, lat_v(blk + 1), state)

    def two_pairs(j, carry):
        return pair(4 * j + 2, *pair(4 * j, *carry, last=False), last=False)

    init = tuple((jnp.full((1, tq), NEG, F32), jnp.zeros((MLA_VROWS, tq), F32)) for _ in heads)
    carry = (scores(lat_k(0), s_a, tk), init)
    inner_pairs = n_blocks // 2 - 1
    carry = lax.fori_loop(0, inner_pairs // 2, two_pairs, carry)
    if inner_pairs % 2:
        carry = pair(n_blocks - 4, *carry, last=False)
    cmax_a, state = pair(n_blocks - 2, *carry, last=True)
    state = absorb(s_a, n_ctx, cmax_a, ctx_v, state)
    outs = [acc[:MLA_V] / acc[MLA_V:MLA_V + 1] for _, acc in state]
    o_ref[...] = jnp.concatenate(outs, axis=0).T.astype(o_ref.dtype)


def _mla_attention(qt, k, vt, vtc, n_batch, seq, n_ctx):
    tq = ROW_TILE
    nt = FLASH_QTILES
    assert seq % (nt * tq) == 0
    nq = seq // (nt * tq)
    ctx0 = n_batch * seq // n_ctx
    return pl.pallas_call(
        _flash_kernel,
        grid=(n_batch, MLA_HEADS // 2, nq),
        in_specs=[pl.BlockSpec((nt, 2 * LANES, tq), lambda b, h, i: (b * nq + i, h, 0)),
                  pl.BlockSpec((seq, 2 * LANES), lambda b, h, i: (b, h)),
                  pl.BlockSpec((seq // tq, 2 * MLA_VROWS, tq), lambda b, h, i: (b, h, 0)),
                  pl.BlockSpec((n_ctx, 2 * LANES), lambda b, h, i: (ctx0 + b, h)),
                  pl.BlockSpec((None, 2 * MLA_VROWS, n_ctx), lambda b, h, i: (b, h, 0))],
        out_specs=pl.BlockSpec((nt * tq, LANES), lambda b, h, i: (b * nq + i, h)),
        out_shape=jax.ShapeDtypeStruct((n_batch * seq, MLA_HEADS * MLA_V), BF16),
        scratch_shapes=[pltpu.VMEM((tq, tq), F32)] * (4 * nt),
        compiler_params=_params(("parallel", "parallel", "arbitrary")),
        name="mla_flash",
    )(qt, k, vt, k, vtc)


def _router_kernel(x_ref, gate_m_ref, o_ref, wo_ref, g_ref, sh_ref, sc_ref, wr_ref, br_ref, tri_ref,
                   x1_ref, h_ref, mi_ref, mf_ref, cnt_ref, run_ref):
    @pl.when(pl.program_id(0) == 0)
    def _init():
        run_ref[...] = jnp.zeros_like(run_ref)

    x1 = _mixer_residual(x_ref[...], gate_m_ref, [o_ref], [wo_ref])
    x1_ref[...] = x1
    h = _norm_mod(x1, g_ref[...], sh_ref[...], sc_ref[...])
    h_ref[...] = h
    h_hi = h.astype(BF16)
    h_lo = (h - h_hi.astype(F32)).astype(BF16)
    hw = _dot(h_hi, wr_ref[...])
    logits = hw[:, :LANES] + hw[:, LANES:] + _dot(h_lo, wr_ref[:, :LANES]) + br_ref[...]
    lane = lax.broadcasted_iota(jnp.int32, logits.shape, 1).astype(F32)
    m1 = jnp.max(logits, axis=-1, keepdims=True)
    i1 = jnp.min(jnp.where(logits == m1, lane, float(LANES)), axis=-1, keepdims=True)
    rest = jnp.where(lane == i1, NEG, logits)
    m2 = jnp.max(rest, axis=-1, keepdims=True)
    i2 = jnp.min(jnp.where(rest == m2, lane, float(LANES)), axis=-1, keepdims=True)
    e = jnp.exp(m2 - m1)
    w1 = 1.0 / (1.0 + e)
    w2 = e / (1.0 + e)
    hit1, hit2 = lane == i1, lane == i2
    onehot = jnp.where(jnp.logical_or(hit1, hit2), 1.0, 0.0)
    before = _dot(tri_ref[...], onehot.astype(BF16)) + run_ref[0:1, :]
    r1 = jnp.sum(jnp.where(hit1, before, 0.0), axis=-1, keepdims=True)
    r2 = jnp.sum(jnp.where(hit2, before, 0.0), axis=-1, keepdims=True)
    run_ref[...] = run_ref[...] + jnp.sum(onehot, axis=0, keepdims=True)
    meta = jnp.where(lane == 0.0, i1, jnp.where(lane == 1.0, i2, jnp.where(lane == 2.0, r1, jnp.where(lane == 3.0, r2, 0.0))))
    mi_ref[...] = meta.astype(jnp.int32)
    mf_ref[...] = jnp.where(lane == 0.0, w1, jnp.where(lane == 1.0, w2, 0.0))
    cnt_ref[...] = run_ref[...]


def _router(x, gate_m, o, wo, g, sh, sc, wr, br, tri, n_rows, tiles_per_batch, n_batch):
    d = x.shape[1]
    tm = ROW_TILE
    mod = _mod_spec(d, tiles_per_batch, n_batch)
    row = lambda i: (i, 0)
    return pl.pallas_call(
        _router_kernel,
        grid=(n_rows // tm,),
        in_specs=[pl.BlockSpec((tm, d), row), mod, pl.BlockSpec((tm, o.shape[1]), row), _full(wo.shape),
                  _full((1, d)), mod, mod, _full(wr.shape), _full(br.shape), _full(tri.shape)],
        out_specs=[pl.BlockSpec((tm, d), row), pl.BlockSpec((tm, d), row), pl.BlockSpec((tm, LANES), row),
                   pl.BlockSpec((tm, LANES), row), _full((SUBLANES, LANES))],
        out_shape=[jax.ShapeDtypeStruct((n_rows, d), F32), jax.ShapeDtypeStruct((n_rows, d), F32),
                   jax.ShapeDtypeStruct((n_rows, LANES), jnp.int32), jax.ShapeDtypeStruct((n_rows, LANES), F32),
                   jax.ShapeDtypeStruct((SUBLANES, LANES), F32)],
        scratch_shapes=[pltpu.VMEM((SUBLANES, LANES), F32)],
        compiler_params=_params(("arbitrary",)),
        name="moe_router",
    )(x, gate_m, o, wo, g, sh, sc, wr, br, tri)


def _slot_owner_kernel(slot_ref, zeros_hbm, owner_ref, sem):
    fill = pltpu.make_async_copy(zeros_hbm, owner_ref, sem)
    fill.start()
    fill.wait()

    def claim(j, carry):
        owner_ref[slot_ref[j]] = j
        return carry

    lax.fori_loop(0, slot_ref.shape[0], claim, 0, unroll=32)


def _slot_owner(slots, n_slots):
    smem = pl.BlockSpec(memory_space=pltpu.SMEM)
    return pl.pallas_call(
        _slot_owner_kernel,
        in_specs=[smem, pl.BlockSpec(memory_space=pl.ANY)],
        out_specs=smem,
        out_shape=jax.ShapeDtypeStruct((n_slots,), jnp.int32),
        scratch_shapes=[pltpu.SemaphoreType.DMA(())],
        name="moe_slot_owner",
    )(slots, jnp.zeros((n_slots,), jnp.int32))


def _moe_kernel(te_ref, src_ref, dst_ref, rows_ref, nv_ref, h_hbm, wg_ref, wu_ref, wd_ref, y_hbm, hbuf, ybuf,
                sem_g, sem_s):
    i, f = pl.program_id(0), pl.program_id(1)
    _, tm, _ = hbuf.shape
    half = tm // MOE_FCHUNKS
    nv = nv_ref[0]
    valid = i < nv
    slot = lax.rem(i, 2)
    other = 1 - slot

    def gather_row(idx, s, r):
        return pltpu.make_async_copy(h_hbm.at[pl.ds(idx, 1), :], hbuf.at[s, pl.ds(r, 1), :], sem_g.at[s])

    def scatter_row(idx, s, r):
        return pltpu.make_async_copy(ybuf.at[s, pl.ds(r, 1), :], y_hbm.at[pl.ds(idx, 1), :], sem_s.at[s])

    def wait_gather(s):
        pltpu.make_async_copy(h_hbm.at[pl.ds(0, tm), :], hbuf.at[s], sem_g.at[s]).wait()

    def wait_scatter(s, n):
        n8 = pl.multiple_of((n // 8) * 8, 8)

        @pl.when(n8 > 0)
        def _():
            pltpu.make_async_copy(ybuf.at[s, pl.ds(0, n8), :], y_hbm.at[pl.ds(0, n8), :], sem_s.at[s]).wait()

        def one_row(j, carry):
            scatter_row(0, s, 0).wait()
            return carry

        lax.fori_loop(0, n - n8, one_row, 0)

    @pl.when(jnp.logical_and(i == 0, f == 0))
    def _prologue():
        ybuf[...] = jnp.zeros_like(ybuf)

        def issue(j, carry):
            gather_row(src_ref[j], 0, j).start()
            return carry

        lax.fori_loop(0, tm, issue, 0)

    @pl.when(jnp.logical_and(f == 0, i <= nv))
    def _rows_ready():
        wait_gather(slot)

    @pl.when(jnp.logical_and(f == 0, jnp.logical_and(i >= 1, i <= nv)))
    def _slot_free():
        wait_scatter(slot, rows_ref[jnp.maximum(i - 1, 0)])

    @pl.when(valid)
    def _compute():
        hb = hbuf[slot].astype(BF16)
        r0 = f * half
        n_prev = rows_ref[i]
        for j in range(half):
            gather_row(src_ref[(i + 1) * tm + r0 + j], other, r0 + j).start()

            @pl.when(r0 + j < n_prev)
            def _():
                scatter_row(dst_ref[i * tm + r0 + j], other, r0 + j).start(priority=1)
        act = (_silu(_dot(hb, wg_ref[...])) * _dot(hb, wu_ref[...])).astype(BF16)
        part = _dot(act, wd_ref[...])
        ybuf[slot] = jnp.where(f == 0, part, ybuf[slot] + part)

    @pl.when(jnp.logical_and(i == nv, f == 0))
    def _flush():
        n_last = rows_ref[i]

        def issue(j, carry):
            scatter_row(dst_ref[i * tm + j], other, j).start()
            return carry

        lax.fori_loop(0, n_last, issue, 0)
        wait_scatter(other, n_last)


def _moe_experts(tile_expert, src, dst, rows, n_valid, h, wg, wu, wd, n_tiles, y_rows):
    d = h.shape[1]
    tm = MOE_TILE
    nf = MOE_FCHUNKS
    fc = wg.shape[2] // nf

    def f_idx(i, f, nv):
        return jnp.where(i < nv[0], f, nf - 1)

    grid_spec = pltpu.PrefetchScalarGridSpec(
        num_scalar_prefetch=5,
        grid=(n_tiles + 1, nf),
        in_specs=[pl.BlockSpec(memory_space=pl.ANY),
                  pl.BlockSpec((None, d, fc), lambda i, f, te, src, dst, rows, nv: (te[i], 0, f_idx(i, f, nv))),
                  pl.BlockSpec((None, d, fc), lambda i, f, te, src, dst, rows, nv: (te[i], 0, f_idx(i, f, nv))),
                  pl.BlockSpec((None, fc, d), lambda i, f, te, src, dst, rows, nv: (te[i], f_idx(i, f, nv), 0))],
        out_specs=pl.BlockSpec(memory_space=pl.ANY),
        scratch_shapes=[pltpu.VMEM((2, tm, d), F32), pltpu.VMEM((2, tm, d), F32),
                        pltpu.SemaphoreType.DMA((2,)), pltpu.SemaphoreType.DMA((2,))],
    )
    return pl.pallas_call(
        _moe_kernel,
        grid_spec=grid_spec,
        out_shape=jax.ShapeDtypeStruct((y_rows, d), F32),
        compiler_params=_params(("arbitrary", "arbitrary")),
        name="moe_experts",
    )(tile_expert, src, dst, rows, n_valid, h, wg, wu, wd)


def _combine_kernel(x_ref, gate_ref, mf_ref, fg_ref, y1_ref, y2_ref, o_ref):
    w = mf_ref[...]
    y = w[:, 0:1] * y1_ref[...] + w[:, 1:2] * y2_ref[...]
    o_ref[...] = _rms(x_ref[...] + gate_ref[...] * y, fg_ref[...])


def _moe_combine(x, gate, mf, fg, y, tiles_per_batch, n_batch):
    t, d = x.shape
    tm = ROW_TILE
    row = lambda i: (i, 0)
    return pl.pallas_call(
        _combine_kernel,
        grid=(t // tm,),
        in_specs=[pl.BlockSpec((tm, d), row), _mod_spec(d, tiles_per_batch, n_batch), pl.BlockSpec((tm, LANES), row),
                  _full((1, d)), pl.BlockSpec((tm, d), row), pl.BlockSpec((tm, d), lambda i: (t // tm + i, 0))],
        out_specs=pl.BlockSpec((tm, d), row),
        out_shape=jax.ShapeDtypeStruct((t, d), F32),
        compiler_params=_params(("parallel",)),
        name="moe_combine",
    )(x, gate, mf, fg, y, y)


def kernel(x, c, ctx, c_ctx, ada_w, ada_b, norm_g, final_g, na_w_in, na_rpb, sg_w, sg_b, sg_norm_g, even_w_out,
           ffn_w_gate, ffn_w_up, ffn_w_down, mla_w_in, mla_q_norm_g, mla_kv_norm_g, mla_w_uq, mla_w_ukv, mla_w_out,
           moe_w_router, moe_b_router, moe_w_gate, moe_w_up, moe_w_down):
    n_batch, seq, d = x.shape
    n_ctx = ctx.shape[1]
    n_lat = n_batch * seq
    assert ada_w.shape[0] == 2 and seq % ROW_TILE == 0 and n_batch * n_ctx == ROW_TILE
    assert n_batch + 1 <= SUBLANES and seq % GRID_W == 0
    tpb = seq // ROW_TILE

    cond8 = jnp.concatenate([c, c_ctx[None, :], jnp.zeros((SUBLANES - n_batch - 1, d), F32)], axis=0)
    mods = _ada_mod(cond8, ada_w, ada_b)[:, :n_batch + 1]
    mod = lambda layer, k: mods[layer, :, k * d:(k + 1) * d].reshape(n_batch + 1, 1, d)
    x_lat, x_ctx = x.reshape(n_lat, d), ctx.reshape(n_batch * n_ctx, d)

    w_in = na_w_in[0].astype(BF16)
    qt, k, vt, ug = _even_in(x_lat, x_ctx, norm_g[0, 0][None], mod(0, 0), mod(0, 1), w_in[:, :3 * A_WIDTH],
                             w_in[:, 3 * A_WIDTH:], tpb, n_batch)
    bias = _na_bias_table(na_rpb[0], seq // GRID_W)
    attn = _na_attention(qt, k, vt, bias, n_batch, seq, n_ctx)
    bmat = jnp.repeat(sg_b[0].T, SG_GROUP_DIM, axis=1)
    avg = jnp.asarray(np.kron(np.eye(SG_GROUPS), np.full((SG_GROUP_DIM, SG_GROUP_DIM), 1.0 / SG_GROUP_DIM)), BF16)
    gated = _spatial_gating(ug, sg_w[0].astype(BF16), bmat, sg_norm_g[0][None], avg)
    w_out = even_w_out[0].astype(BF16)
    xs = _ffn(x_lat, x_ctx, mod(0, 2), [attn, gated], [w_out[:A_WIDTH], w_out[A_WIDTH:]], norm_g[0, 1][None], mod(0, 3),
              mod(0, 4), mod(0, 5), ffn_w_gate[0].astype(BF16), ffn_w_up[0].astype(BF16), ffn_w_down[0].astype(BF16),
              tpb, n_batch)

    win, wuq_t, wuk, wuv_t = _mla_weights(mla_w_in[0], mla_w_uq[0], mla_w_ukv[0])
    cos, sin = _rope_tables(n_batch, seq, n_ctx)
    qt, k, vt = _mla_pre(xs, norm_g[1, 0][None], mod(1, 0), mod(1, 1), win, mla_q_norm_g[0][None],
                         mla_kv_norm_g[0][None], wuq_t, wuk, wuv_t, cos, sin, tpb, n_batch)
    vtc = jnp.transpose(vt[n_batch * tpb].reshape(-1, n_batch, n_ctx), (1, 0, 2))
    o = _mla_attention(qt, k, vt, vtc, n_batch, seq, n_ctx)
    wr = jnp.concatenate([moe_w_router[0], jnp.zeros((d, LANES - N_EXPERTS), F32)], axis=1)
    wr_hi = wr.astype(BF16)
    wr = jnp.concatenate([wr_hi, (wr - wr_hi.astype(F32)).astype(BF16)], axis=1)
    br = jnp.concatenate([moe_b_router[0], jnp.full((LANES - N_EXPERTS,), NEG, F32)])[None]
    tri = jnp.asarray(np.tril(np.ones((ROW_TILE, ROW_TILE), np.float32), -1), BF16)
    x1, h, mi, mf, cnt = _router(xs, mod(1, 2), o, mla_w_out[0].astype(BF16), norm_g[1, 1][None], mod(1, 3),
                                 mod(1, 4), wr, br, tri, n_lat, tpb, n_batch)

    counts = cnt[0, :N_EXPERTS].astype(jnp.int32)
    tiles_e = (counts + MOE_TILE - 1) // MOE_TILE
    tile_end = jnp.cumsum(tiles_e)
    start = (tile_end - tiles_e) * MOE_TILE
    slot12 = jnp.concatenate([start[mi[:, 0]] + mi[:, 2], start[mi[:, 1]] + mi[:, 3]])
    n_tiles = 2 * n_lat // MOE_TILE + N_EXPERTS
    n_slots = n_tiles * MOE_TILE
    dst = _slot_owner(slot12.astype(jnp.int32), n_slots)
    spare_tile = jnp.zeros((MOE_TILE,), jnp.int32)
    src = jnp.concatenate([dst % n_lat, spare_tile])
    dst = jnp.concatenate([spare_tile, dst])
    n_valid = tile_end[-1:]
    last_tile = jnp.maximum(n_valid[0] - 1, 0)
    all_tiles = jnp.arange(n_tiles + 1, dtype=jnp.int32)
    tile_ids = jnp.minimum(all_tiles, last_tile)
    tile_expert = jnp.minimum(jnp.sum((tile_ids[:, None] >= tile_end[None, :]).astype(jnp.int32), axis=1),
                              N_EXPERTS - 1)
    filled = counts[tile_expert] - (all_tiles - (tile_end - tiles_e)[tile_expert]) * MOE_TILE
    rows = jnp.where(all_tiles < n_valid[0], jnp.clip(filled, 0, MOE_TILE), 0)
    rows = jnp.concatenate([jnp.zeros((1,), jnp.int32), rows]).astype(jnp.int32)
    y = _moe_experts(tile_expert, src, dst, rows, n_valid.astype(jnp.int32), h, moe_w_gate[0].astype(BF16),
                     moe_w_up[0].astype(BF16), moe_w_down[0].astype(BF16), n_tiles, 2 * n_lat)
    out = _moe_combine(x1, mod(1, 5), mf, final_g[None], y, tpb, n_batch)
    return out.reshape(n_batch, seq, d)
```
